```python
import math
import jax, jax.numpy as jnp
from jax import lax
import numpy as np

D_MODEL = 1024
BATCH = 2
SEQ = 8192
DEPTH = 2
DEC_BATCH = 128
DEC_SEQ = 8
PAST_LEN = 16384
PAGE_SIZE = 128

F32 = jnp.float32
EPS = 1e-6
ROPE_BASE = 10000.0
RET_HEADS = 4
RET_DK = 64
RET_DV = 64
RET_WIDTH = RET_HEADS * RET_DV
RET_CHUNK = 128
LRU_WIDTH = 256
LRU_BLOCKS = 4
LRU_BLOCK = LRU_WIDTH // LRU_BLOCKS
CONV_W = 4
LRU_C = 8.0
MLA_HEADS = 8
MLA_NOPE = 64
MLA_ROPE = 32
MLA_V = 64
KV_RANK = 128
MLA_WIDTH = MLA_HEADS * MLA_V
Q_BLOCK = 128
D_MIX = RET_WIDTH + LRU_WIDTH + MLA_WIDTH
IN_SIZES = (RET_HEADS * RET_DK, RET_HEADS * RET_DK, RET_WIDTH, RET_WIDTH,
            LRU_WIDTH, LRU_WIDTH,
            MLA_HEADS * (MLA_NOPE + MLA_ROPE), KV_RANK, MLA_ROPE)
N_IN = 2 * RET_HEADS * RET_DK + 2 * RET_WIDTH + 2 * LRU_WIDTH + MLA_HEADS * (MLA_NOPE + MLA_ROPE) + KV_RANK + MLA_ROPE
N_EXPERTS = 64
TOP_K = 8
N_GROUPS = 8
TOPK_GROUPS = 4
D_EXPERT = 256
D_SHARED = 256
ROUTED_SCALE = 2.5
MOE_BLOCK = 128

kernel_name = "hymba_ret_rglru_mla_moe_step"


def rmsnorm(x, g):
    xf = x.astype(F32)
    y = xf * lax.rsqrt(jnp.mean(xf * xf, axis=-1, keepdims=True) + EPS)
    return (y * g.astype(F32)).astype(x.dtype)


def rope(x, pos):
    d = x.shape[-1]
    inv = ROPE_BASE ** (-jnp.arange(0, d, 2, dtype=F32) / d)
    ang = pos.astype(F32)[:, None] * inv[None, :]
    cos, sin = jnp.cos(ang)[:, None, :], jnp.sin(ang)[:, None, :]
    xf = x.astype(F32)
    x1, x2 = xf[..., : d // 2], xf[..., d // 2:]
    return jnp.concatenate([x1 * cos - x2 * sin, x1 * sin + x2 * cos], axis=-1).astype(x.dtype)


def split_cols(proj):
    idx, acc = [], 0
    for s in IN_SIZES[:-1]:
        acc += s
        idx.append(acc)
    return jnp.split(proj, idx, axis=-1)


def retention(q, k, v, r0):
    B, L = q.shape[:2]
    C = RET_CHUNK if L % RET_CHUNK == 0 else L
    n = L // C
    log_g = jnp.log1p(-jnp.exp2(-5.0 - jnp.arange(RET_HEADS, dtype=F32)))
    i = jnp.arange(C, dtype=F32)
    diff = i[:, None] - i[None, :]
    dmat = jnp.where(diff >= 0, jnp.exp(jnp.maximum(diff, 0.0)[None] * log_g[:, None, None]), 0.0)
    xi = jnp.exp((i[:, None] + 1.0) * log_g[None, :])
    zeta = jnp.exp((C - 1.0 - i)[:, None] * log_g[None, :])
    qc = q.astype(F32).reshape(B, n, C, RET_HEADS, RET_DK)
    kc = k.astype(F32).reshape(B, n, C, RET_HEADS, RET_DK) * (RET_DK ** -0.5)
    vc = v.astype(F32).reshape(B, n, C, RET_HEADS, RET_DV)
    inner = jnp.einsum('bnihd,bnjhd->bnhij', qc, kc) * dmat
    inner = jnp.einsum('bnhij,bnjhe->bnihe', inner, vc)
    kv = jnp.einsum('bnjhd,bnjhe,jh->nbhde', kc, vc, zeta)
    decay_c = jnp.exp(C * log_g)[None, :, None, None]

    def step(r, kv_n):
        return decay_c * r + kv_n, r

    r_fin, r_prev = lax.scan(step, r0.astype(F32), kv)
    cross = jnp.einsum('bnihd,nbhde->bnihe', qc, r_prev) * xi[None, None, :, :, None]
    return (inner + cross).reshape(B, L, RET_HEADS, RET_DV), r_fin


def head_group_norm(o, gain):
    B, L = o.shape[:2]
    mu = jnp.mean(o, axis=-1, keepdims=True)
    var = jnp.mean(jnp.square(o - mu), axis=-1, keepdims=True)
    return ((o - mu) * lax.rsqrt(var + EPS)).reshape(B, L, -1) * gain.astype(F32)


def linear_scan(a, b, h0):
    b = b.at[:, 0].add(a[:, 0] * h0)

    def comb(lhs, rhs):
        a1, b1 = lhs
        a2, b2 = rhs
        return a1 * a2, a2 * b1 + b2

    _, h = lax.associative_scan(comb, (a, b), axis=1)
    return h


def rglru_group(xb, gate, buf, h0, conv_w, conv_b, w_a, b_a, w_i, b_i, lam):
    B, L, W = xb.shape
    xin = jnp.concatenate([buf.astype(xb.dtype), xb], axis=1)
    xc = conv_b + sum(xin[:, k:k + L] * conv_w[k] for k in range(CONV_W))
    new_buf = xin[:, L:]
    xg = xc.reshape(B, L, LRU_BLOCKS, LRU_BLOCK)
    r = jax.nn.sigmoid((jnp.einsum('blnd,nde->blne', xg, w_a).reshape(B, L, W) + b_a).astype(F32))
    ig = jax.nn.sigmoid((jnp.einsum('blnd,nde->blne', xg, w_i).reshape(B, L, W) + b_i).astype(F32))
    log_a = -LRU_C * r * jax.nn.softplus(-lam.astype(F32))
    a = jnp.exp(log_a)
    bt = jnp.sqrt(-jnp.expm1(2.0 * log_a)) * ig * xc.astype(F32)
    h = linear_scan(a, bt, h0.astype(F32))
    y = h * jax.nn.gelu(gate.astype(F32))
    return y, new_buf, h[:, -1]


def mla_prompt(q_nope, q_rope, ckv, kr, w_uk, w_uv):
    B, S = q_nope.shape[:2]
    QB = Q_BLOCK if S % Q_BLOCK == 0 else S
    nq = S // QB
    scale = (MLA_NOPE + MLA_ROPE) ** -0.5
    k_nope = jnp.einsum('bsc,chn->bshn', ckv, w_uk)
    v = jnp.einsum('bsc,chv->bshv', ckv, w_uv)
    qn = q_nope.reshape(B, nq, QB, MLA_HEADS, MLA_NOPE).swapaxes(0, 1)
    qr = q_rope.reshape(B, nq, QB, MLA_HEADS, MLA_ROPE).swapaxes(0, 1)
    kpos = jnp.arange(S)

    def block(args):
        qn_b, qr_b, bi = args
        s = (jnp.einsum('bqhn,bkhn->bhqk', qn_b, k_nope) + jnp.einsum('bqhr,bkr->bhqk', qr_b, kr)).astype(F32) * scale
        qpos = bi * QB + jnp.arange(QB)
        s = jnp.where(kpos[None, :] <= qpos[:, None], s, -jnp.inf)
        p = jax.nn.softmax(s, axis=-1)
        return jnp.einsum('bhqk,bkhv->bqhv', p.astype(v.dtype), v)

    o = lax.map(block, (qn, qr, jnp.arange(nq)))
    return o.swapaxes(0, 1).reshape(B, S, MLA_WIDTH)


def mla_sample(q_nope, q_rope, ckv, kr, cache_ckv, cache_kr, page_table, li, w_uk, w_uv):
    B, T = q_nope.shape[:2]
    scale = (MLA_NOPE + MLA_ROPE) ** -0.5
    q_lat = jnp.einsum('bthn,chn->bthc', q_nope.astype(F32), w_uk.astype(F32)) * scale
    q_r = q_rope.astype(F32) * scale
    ckv_f, kr_f = ckv.astype(F32), kr.astype(F32)
    s = jnp.einsum('bthc,buc->bhtu', q_lat, ckv_f) + jnp.einsum('bthr,bur->bhtu', q_r, kr_f)
    causal = jnp.arange(T)[:, None] >= jnp.arange(T)[None, :]
    s = jnp.where(causal, s, -jnp.inf)
    m0 = jnp.max(s, axis=-1)
    p = jnp.exp(s - m0[..., None])
    carry0 = (m0, jnp.sum(p, axis=-1), jnp.einsum('bhtu,buc->bhtc', p, ckv_f))

    def step(carry, pt):
        m, l_sum, acc = carry
        ck = cache_ckv[li, pt].astype(F32)
        kp = cache_kr[li, pt].astype(F32)
        sp = jnp.einsum('bthc,bpc->bhtp', q_lat, ck) + jnp.einsum('bthr,bpr->bhtp', q_r, kp)
        m_new = jnp.maximum(m, jnp.max(sp, axis=-1))
        corr = jnp.exp(m - m_new)
        pp = jnp.exp(sp - m_new[..., None])
        acc = acc * corr[..., None] + jnp.einsum('bhtp,bpc->bhtc', pp, ck)
        return (m_new, l_sum * corr + jnp.sum(pp, axis=-1), acc), None

    (_, l_sum, acc), _ = lax.scan(step, carry0, page_table.T)
    o = jnp.einsum('bhtc,chv->bthv', acc / l_sum[..., None], w_uv.astype(F32))
    return o.reshape(B, T, MLA_WIDTH)


def moe_ffn(x, w_router, b_router, w_exp_gu, w_exp_down, w_sh_gu, w_sh_down):
    T, D = x.shape
    E = N_EXPERTS
    scores = jax.nn.sigmoid((x @ w_router).astype(F32))
    biased = scores + b_router.astype(F32)
    grp = biased.reshape(T, N_GROUPS, E // N_GROUPS)
    grp_score = jnp.sum(lax.top_k(grp, 2)[0], axis=-1)
    gidx = lax.top_k(grp_score, TOPK_GROUPS)[1]
    gmask = jnp.sum(jax.nn.one_hot(gidx, N_GROUPS, dtype=F32), axis=1) > 0
    emask = jnp.repeat(gmask, E // N_GROUPS, axis=1)
    topi = lax.top_k(jnp.where(emask, biased, -jnp.inf), TOP_K)[1]
    gate = jnp.take_along_axis(scores, topi, axis=1)
    gate = gate / jnp.sum(gate, axis=-1, keepdims=True) * ROUTED_SCALE
    N = T * TOP_K
    flat_e = topi.reshape(N)
    flat_tok = jnp.arange(N) // TOP_K
    order = jnp.argsort(flat_e)
    se, stok, sgate = flat_e[order], flat_tok[order], gate.reshape(N)[order]
    counts = jnp.bincount(flat_e, length=E)
    starts = jnp.cumsum(counts) - counts
    padded = (counts + MOE_BLOCK - 1) // MOE_BLOCK * MOE_BLOCK
    pends = jnp.cumsum(padded)
    dest = pends[se] - padded[se] + jnp.arange(N) - starts[se]
    n_blocks = -(-N // MOE_BLOCK) + E
    rows = jnp.zeros((n_blocks * MOE_BLOCK, D), x.dtype).at[dest].set(x[stok])
    block_e = jnp.minimum(jnp.searchsorted(pends, jnp.arange(n_blocks) * MOE_BLOCK, side='right'), E - 1)

    def expert_block(args):
        xb, e = args
        g, u = jnp.split(xb @ w_exp_gu[e], 2, axis=-1)
        return (jax.nn.silu(g) * u) @ w_exp_down[e]

    out_rows = lax.map(expert_block, (rows.reshape(n_blocks, MOE_BLOCK, D), block_e)).reshape(-1, D)
    routed = jax.ops.segment_sum(out_rows[dest] * sgate[:, None].astype(x.dtype), stok, num_segments=T)
    sg, su = jnp.split(x @ w_sh_gu, 2, axis=-1)
    return routed + (jax.nn.silu(sg) * su) @ w_sh_down


def trunk_layer(x, c, pos, li, p, past):
    B, L, D = x.shape
    mod = (jax.nn.silu(c) @ p['w_ada'] + p['b_ada'])[:, None, :]
    sh1, sc1, gt1, sh2, sc2, gt2 = jnp.split(mod, 6, axis=-1)
    g = p['norm_gains']
    gain = p['grp_gain']
    h = rmsnorm(x, g[0]) * (1 + sc1) + sh1
    rq, rk, rv, rg, lx, lg, mq, mc, mk = split_cols(h @ p['w_in'])
    rq = rope(rq.reshape(B, L, RET_HEADS, RET_DK), pos)
    rk = rope(rk.reshape(B, L, RET_HEADS, RET_DK), pos)
    r0 = jnp.zeros((B, RET_HEADS, RET_DK, RET_DV), F32) if past is None else past['ret']
    o_ret, r_new = retention(rq, rk, rv.reshape(B, L, RET_HEADS, RET_DV), r0)
    ret_out = head_group_norm(o_ret, gain[:RET_WIDTH]) * jax.nn.silu(rg.astype(F32))
    buf0 = jnp.zeros((B, CONV_W - 1, LRU_WIDTH), x.dtype) if past is None else past['conv']
    h0 = jnp.zeros((B, LRU_WIDTH), F32) if past is None else past['lru']
    y_lru, conv_new, h_new = rglru_group(lx, lg, buf0, h0, p['conv_w'], p['conv_b'], p['w_lru_a'], p['b_lru_a'],
                                         p['w_lru_i'], p['b_lru_i'], p['lru_lambda'])
    lru_out = rmsnorm(y_lru, gain[RET_WIDTH:RET_WIDTH + LRU_WIDTH])
    mq = mq.reshape(B, L, MLA_HEADS, MLA_NOPE + MLA_ROPE)
    q_nope = mq[..., :MLA_NOPE]
    q_rope = rope(mq[..., MLA_NOPE:], pos)
    ckv = rmsnorm(mc, p['kv_norm'])
    kr = rope(mk[:, :, None, :], pos)[:, :, 0]
    w_uk, w_uv = p['w_ukv'][..., :MLA_NOPE], p['w_ukv'][..., MLA_NOPE:]
    if past is None:
        o_mla = mla_prompt(q_nope, q_rope, ckv, kr, w_uk, w_uv)
    else:
        o_mla = mla_sample(q_nope, q_rope, ckv, kr, past['cache_ckv'], past['cache_kr'], past['page_table'], li,
                           w_uk, w_uv)
    mla_out = rmsnorm(o_mla, gain[RET_WIDTH + LRU_WIDTH:])
    mix = jnp.concatenate([ret_out.astype(x.dtype), lru_out.astype(x.dtype), mla_out.astype(x.dtype)], axis=-1)
    x = x + gt1 * rmsnorm(mix @ p['w_out'], g[1])
    h2 = rmsnorm(x, g[2]) * (1 + sc2) + sh2
    f = moe_ffn(h2.reshape(B * L, D), p['w_router'], p['b_router'], p['w_exp_gu'], p['w_exp_down'],
                p['w_sh_gu'], p['w_sh_down']).reshape(B, L, D)
    x = x + gt2 * rmsnorm(f, g[3])
    return x, (ckv, kr, r_new, conv_new, h_new)


def setup_inputs(seed: int = 0) -> dict:
    key = jax.random.key(seed)
    ks = jax.random.split(key, 31)
    n_pages = PAST_LEN // PAGE_SIZE
    n_used = DEC_BATCH * n_pages
    n_pool = n_used + (n_used + 3) // 4

    def nrm(k, shape, scale=1.0):
        return jax.random.normal(k, shape, F32) * scale

    u = jax.random.uniform(ks[22], (DEPTH, LRU_WIDTH), F32, minval=0.9, maxval=0.999)
    s = u ** (1.0 / LRU_C)
    lru_lambda = jnp.log(s) - jnp.log1p(-s)
    page_table = jax.random.permutation(ks[7], n_pool)[:n_used].reshape(DEC_BATCH, n_pages).astype(jnp.int32)
    return {
        "x_prompt": nrm(ks[0], (BATCH, SEQ, D_MODEL)),
        "x_sample": nrm(ks[1], (DEC_BATCH, DEC_SEQ, D_MODEL)),
        "cache_ckv": nrm(ks[2], (DEPTH, n_pool, PAGE_SIZE, KV_RANK)),
        "cache_krope": nrm(ks[3], (DEPTH, n_pool, PAGE_SIZE, MLA_ROPE)),
        "state_ret": nrm(ks[4], (DEPTH, DEC_BATCH, RET_HEADS, RET_DK, RET_DV), 0.5),
        "state_conv": nrm(ks[5], (DEPTH, DEC_BATCH, CONV_W - 1, LRU_WIDTH)),
        "state_lru": nrm(ks[6], (DEPTH, DEC_BATCH, LRU_WIDTH), 0.5),
        "page_table": page_table,
        "c_prompt": nrm(ks[8], (BATCH, D_MODEL)),
        "c_sample": nrm(ks[9], (DEC_BATCH, D_MODEL)),
        "w_ada": nrm(ks[10], (DEPTH, D_MODEL, 6 * D_MODEL), 0.5 * D_MODEL ** -0.5),
        "b_ada": nrm(ks[11], (DEPTH, 6 * D_MODEL), 0.01),
        "norm_gains": 1.0 + nrm(ks[12], (DEPTH, 4, D_MODEL), 0.02),
        "w_in": nrm(ks[13], (DEPTH, D_MODEL, N_IN), D_MODEL ** -0.5),
        "w_out": nrm(ks[14], (DEPTH, D_MIX, D_MODEL), D_MIX ** -0.5),
        "grp_gain": 1.0 + nrm(ks[15], (DEPTH, D_MIX), 0.02),
        "conv_w": nrm(ks[16], (DEPTH, CONV_W, LRU_WIDTH), CONV_W ** -0.5),
        "conv_b": nrm(ks[17], (DEPTH, LRU_WIDTH), 0.01),
        "w_lru_a": nrm(ks[18], (DEPTH, LRU_BLOCKS, LRU_BLOCK, LRU_BLOCK), LRU_BLOCK ** -0.5),
        "b_lru_a": nrm(ks[19], (DEPTH, LRU_WIDTH), 0.01),
        "w_lru_i": nrm(ks[20], (DEPTH, LRU_BLOCKS, LRU_BLOCK, LRU_BLOCK), LRU_BLOCK ** -0.5),
        "b_lru_i": nrm(ks[21], (DEPTH, LRU_WIDTH), 0.01),
        "lru_lambda": lru_lambda,
        "kv_norm": 1.0 + nrm(ks[23], (DEPTH, KV_RANK), 0.02),
        "w_ukv": nrm(ks[24], (DEPTH, KV_RANK, MLA_HEADS, MLA_NOPE + MLA_V), KV_RANK ** -0.5),
        "w_router": nrm(ks[25], (DEPTH, D_MODEL, N_EXPERTS), D_MODEL ** -0.5),
        "b_router": nrm(ks[26], (DEPTH, N_EXPERTS), 0.01),
        "w_exp_gu": nrm(ks[27], (DEPTH, N_EXPERTS, D_MODEL, 2 * D_EXPERT), D_MODEL ** -0.5),
        "w_exp_down": nrm(ks[28], (DEPTH, N_EXPERTS, D_EXPERT, D_MODEL), D_EXPERT ** -0.5),
        "w_sh_gu": nrm(ks[29], (DEPTH, D_MODEL, 2 * D_SHARED), D_MODEL ** -0.5),
        "w_sh_down": nrm(ks[30], (DEPTH, D_SHARED, D_MODEL), D_SHARED ** -0.5),
    }


def reference(x_prompt, x_sample, cache_ckv, cache_krope, state_ret, state_conv, state_lru, page_table,
              c_prompt, c_sample, w_ada, b_ada, norm_gains, w_in, w_out, grp_gain, conv_w, conv_b,
              w_lru_a, b_lru_a, w_lru_i, b_lru_i, lru_lambda, kv_norm, w_ukv, w_router, b_router,
              w_exp_gu, w_exp_down, w_sh_gu, w_sh_down):
    past_len = page_table.shape[1] * cache_ckv.shape[2]
    pos_p = jnp.arange(x_prompt.shape[1])
    pos_s = past_len + jnp.arange(x_sample.shape[1])
    y_prompt, y_sample = x_prompt, x_sample
    st_p, st_s = [], []
    for li in range(DEPTH):
        p = dict(w_ada=w_ada[li], b_ada=b_ada[li], norm_gains=norm_gains[li], w_in=w_in[li], w_out=w_out[li],
                 grp_gain=grp_gain[li], conv_w=conv_w[li], conv_b=conv_b[li], w_lru_a=w_lru_a[li],
                 b_lru_a=b_lru_a[li], w_lru_i=w_lru_i[li], b_lru_i=b_lru_i[li], lru_lambda=lru_lambda[li],
                 kv_norm=kv_norm[li], w_ukv=w_ukv[li], w_router=w_router[li], b_router=b_router[li],
                 w_exp_gu=w_exp_gu[li], w_exp_down=w_exp_down[li], w_sh_gu=w_sh_gu[li], w_sh_down=w_sh_down[li])
        y_prompt, sp = trunk_layer(y_prompt, c_prompt, pos_p, li, p, None)
        past = dict(ret=state_ret[li], conv=state_conv[li], lru=state_lru[li], cache_ckv=cache_ckv,
                    cache_kr=cache_krope, page_table=page_table)
        y_sample, ss = trunk_layer(y_sample, c_sample, pos_s, li, p, past)
        st_p.append(sp)
        st_s.append(ss)
    new_ckv_prompt = jnp.stack([s[0] for s in st_p])
    new_krope_prompt = jnp.stack([s[1] for s in st_p])
    new_ret_prompt = jnp.stack([s[2] for s in st_p])
    new_conv_prompt = jnp.stack([s[3] for s in st_p])
    new_lru_prompt = jnp.stack([s[4] for s in st_p])
    new_ckv_sample = jnp.stack([s[0] for s in st_s])
    new_krope_sample = jnp.stack([s[1] for s in st_s])
    new_ret_sample = jnp.stack([s[2] for s in st_s])
    new_conv_sample = jnp.stack([s[3] for s in st_s])
    new_lru_sample = jnp.stack([s[4] for s in st_s])
    return (y_prompt, y_sample, new_ckv_prompt, new_krope_prompt, new_ret_prompt, new_conv_prompt, new_lru_prompt,
            new_ckv_sample, new_krope_sample, new_ret_sample, new_conv_sample, new_lru_sample)
```

```python
import functools

import numpy as np
import jax
import jax.numpy as jnp
from jax import lax
from jax.experimental import pallas as pl
from jax.experimental.pallas import tpu as pltpu

F32 = jnp.float32
BF16 = jnp.bfloat16
EPS = 1e-6
ROPE_BASE = 10000.0

RET_HEADS = 4
RET_DK = 64
RET_DV = 64
RET_WIDTH = RET_HEADS * RET_DV
RET_CHUNK = 128
LRU_WIDTH = 256
LRU_BLOCKS = 4
LRU_BLOCK = LRU_WIDTH // LRU_BLOCKS
CONV_W = 4
LRU_C = 8.0
MLA_HEADS = 8
MLA_NOPE = 64
MLA_ROPE = 32
MLA_V = 64
KV_RANK = 128
MLA_WIDTH = MLA_HEADS * MLA_V
N_EXPERTS = 64
TOP_K = 8
N_GROUPS = 8
GROUP_SIZE = N_EXPERTS // N_GROUPS
TOPK_GROUPS = 4
D_EXPERT = 256
ROUTED_SCALE = 2.5

LANES = 128
SUBLANES = 8
QCAT = 2 * LANES
C_RQ, C_RK, C_RV, C_RG, C_LX, C_LG = 0, 256, 512, 768, 1024, 1280
C_QN = 1536
C_QR = C_QN + MLA_HEADS * MLA_NOPE
C_MC = C_QR + MLA_HEADS * MLA_ROPE
C_MK = C_MC + KV_RANK
N_PROJ = C_MK + LANES


def _cparams(sem, vmem_mb=48):
    return pltpu.CompilerParams(dimension_semantics=sem, vmem_limit_bytes=vmem_mb * 2 ** 20)


def _rms(x, g):
    return x * lax.rsqrt(jnp.mean(x * x, axis=-1, keepdims=True) + EPS) * g


def _silu(x):
    return x * jax.nn.sigmoid(x)


def _adaln_kernel(c_ref, w_ref, b_ref, o_ref):
    c = c_ref[...]
    s = _silu(c).astype(BF16)
    o_ref[...] = jnp.dot(s, w_ref[...].astype(BF16), preferred_element_type=F32) + b_ref[...]


def _adaln(c, w_ada, b_ada):
    depth, d, n = w_ada.shape
    bp = c.shape[0]
    tn = 768
    return pl.pallas_call(
        _adaln_kernel,
        grid=(depth, n // tn),
        in_specs=[pl.BlockSpec((bp, d), lambda l, j: (0, 0)),
                  pl.BlockSpec((None, d, tn), lambda l, j: (l, 0, j)),
                  pl.BlockSpec((None, 1, tn), lambda l, j: (l, 0, j))],
        out_specs=pl.BlockSpec((None, bp, tn), lambda l, j: (l, 0, j)),
        out_shape=jax.ShapeDtypeStruct((depth, bp, n), F32),
        compiler_params=_cparams(("arbitrary", "arbitrary")),
        name="adaln",
    )(c, w_ada, b_ada)


def _rope_slab(v, cos, sin_signed, half):
    lane = lax.broadcasted_iota(jnp.int32, v.shape, 1)
    first = (lane % (2 * half)) < half
    partner = jnp.where(first, pltpu.roll(v, LANES - half, 1), pltpu.roll(v, half, 1))
    return v * cos + partner * sin_signed


def _rope(v, cos, sin_signed, half):
    n = v.shape[1] // LANES
    outs = [_rope_slab(v[:, i * LANES:(i + 1) * LANES], cos, sin_signed, half) for i in range(n)]
    return outs[0] if n == 1 else jnp.concatenate(outs, axis=1)


def _inproj_kernel(x_ref, sc_ref, sh_ref, g_ref, w_ref, wuk_ref, kvg_ref, c64_ref, s64_ref, c32_ref, s32_ref,
                   rq_ref, rk_ref, rv_ref, rg_ref, lx_ref, lg_ref, qcat_ref, kcat_ref, ckv_ref, kr_ref):
    bt, lt, d = x_ref.shape
    rows = bt * lt
    h = _rms(x_ref[...], g_ref[...]) * (1.0 + sc_ref[...]) + sh_ref[...]
    hb = h.reshape(rows, d).astype(BF16)

    def proj(a, b):
        return jnp.dot(hb, w_ref[:, a:b], preferred_element_type=F32)

    def table(ref):
        t = ref[...]
        if bt > 1:
            t = jnp.broadcast_to(t[None], (bt, lt, LANES)).reshape(rows, LANES)
        return t

    c64, s64, c32, s32 = table(c64_ref), table(s64_ref), table(c32_ref), table(s32_ref)
    rq_ref[...] = _rope(proj(C_RQ, C_RK), c64, s64, RET_DK // 2).reshape(bt, lt, RET_WIDTH)
    rk_ref[...] = _rope(proj(C_RK, C_RV), c64, s64, RET_DK // 2).reshape(bt, lt, RET_WIDTH)
    rv_ref[...] = proj(C_RV, C_RG).reshape(bt, lt, RET_WIDTH)
    rg_ref[...] = proj(C_RG, C_LX).reshape(bt, lt, RET_WIDTH)
    lx_ref[...] = proj(C_LX, C_LG).reshape(bt, lt, LRU_WIDTH)
    lg_ref[...] = proj(C_LG, C_QN).reshape(bt, lt, LRU_WIDTH)

    scale = (MLA_NOPE + MLA_ROPE) ** -0.5
    q_lat = jnp.dot(proj(C_QN, C_QR).astype(BF16), wuk_ref[...], preferred_element_type=F32) * scale
    q_rope = _rope(proj(C_QR, C_MC), c32, s32, MLA_ROPE // 2) * scale
    lane = lax.broadcasted_iota(jnp.int32, (rows, LANES), 1)
    heads_per_slab = LANES // MLA_ROPE
    for hd in range(MLA_HEADS):
        slab = q_rope[:, (hd // heads_per_slab) * LANES:(hd // heads_per_slab + 1) * LANES]
        own = jnp.where(lane // MLA_ROPE == hd % heads_per_slab, slab, 0.0)
        qh = jnp.concatenate([q_lat[:, hd * KV_RANK:(hd + 1) * KV_RANK], own], axis=1)
        qcat_ref[:, hd, :, :] = qh.reshape(bt, lt, QCAT).astype(qcat_ref.dtype)

    ckv = _rms(proj(C_MC, C_MK), kvg_ref[...])
    kr4 = _rope(proj(C_MK, N_PROJ), c32, s32, MLA_ROPE // 2)
    ckv_ref[...] = ckv.reshape(bt, lt, KV_RANK)
    kr_ref[...] = kr4[:, :MLA_ROPE].reshape(bt, lt, MLA_ROPE)
    kcat_ref[...] = jnp.concatenate([ckv, kr4], axis=1).reshape(bt, lt, QCAT).astype(kcat_ref.dtype)


def _inproj(x, sc, sh, g, w_in_p, wuk_bd, kv_gain, tabs, bt, lt, qdtype):
    b, l, d = x.shape
    grid = (b // bt, l // lt)
    c64, s64, c32, s32 = tabs
    tok = lambda w: pl.BlockSpec((bt, lt, w), lambda i, j: (i, j, 0))
    mod = pl.BlockSpec((bt, 1, d), lambda i, j: (i, 0, 0))
    const2 = lambda a: pl.BlockSpec(a.shape, lambda i, j: (0, 0))
    tab = pl.BlockSpec((lt, LANES), lambda i, j: (j, 0))
    out_shape = [jax.ShapeDtypeStruct((b, l, 256), F32)] * 6 + [
        jax.ShapeDtypeStruct((b, MLA_HEADS, l, QCAT), qdtype),
        jax.ShapeDtypeStruct((b, l, QCAT), qdtype),
        jax.ShapeDtypeStruct((b, l, KV_RANK), F32),
        jax.ShapeDtypeStruct((b, l, MLA_ROPE), F32)]
    out_specs = [tok(256)] * 6 + [
        pl.BlockSpec((bt, MLA_HEADS, lt, QCAT), lambda i, j: (i, 0, j, 0)),
        tok(QCAT), tok(KV_RANK), tok(MLA_ROPE)]
    return pl.pallas_call(
        _inproj_kernel,
        grid=grid,
        in_specs=[tok(d), mod, mod, pl.BlockSpec((1, 1, d), lambda i, j: (0, 0, 0)),
                  const2(w_in_p), const2(wuk_bd), const2(kv_gain), tab, tab, tab, tab],
        out_specs=out_specs,
        out_shape=out_shape,
        compiler_params=_cparams(("arbitrary", "arbitrary")),
        name="inproj",
    )(x, sc, sh, g, w_in_p, wuk_bd, kv_gain, c64, s64, c32, s32)


def _retention_kernel(q_ref, k_ref, v_ref, g_ref, r0_ref, dmat_ref, xi_ref, zeta_ref, dec_ref, gain_ref,
                      o_ref, rfin_ref, r_sc):
    ci = pl.program_id(1)

    @pl.when(ci == 0)
    def _():
        r_sc[...] = r0_ref[0]

    q = q_ref[0]
    k = k_ref[0] * (RET_DK ** -0.5)
    v = v_ref[0]
    outs = []
    for hd in range(RET_HEADS):
        sl = slice(hd * RET_DK, (hd + 1) * RET_DK)
        qh = q[:, sl].astype(BF16)
        kh = k[:, sl]
        vh = v[:, sl].astype(BF16)
        s = lax.dot_general(qh, kh.astype(BF16), (((1,), (1,)), ((), ())), preferred_element_type=F32)
        inner = jnp.dot((s * dmat_ref[hd]).astype(BF16), vh, preferred_element_type=F32)
        r = r_sc[hd]
        cross = jnp.dot(qh, r.astype(BF16), preferred_element_type=F32) * xi_ref[:, hd:hd + 1]
        o = inner + cross
        mu = jnp.mean(o, axis=-1, keepdims=True)
        oc = o - mu
        var = jnp.mean(oc * oc, axis=-1, keepdims=True)
        outs.append(oc * lax.rsqrt(var + EPS))
        kz = (kh * zeta_ref[:, hd:hd + 1]).astype(BF16)
        kv = lax.dot_general(kz, vh, (((0,), (0,)), ((), ())), preferred_element_type=F32)
        r_sc[hd] = dec_ref[hd] * r + kv
    o_ref[0] = jnp.concatenate(outs, axis=1) * gain_ref[...] * _silu(g_ref[0])

    @pl.when(ci == pl.num_programs(1) - 1)
    def _():
        rfin_ref[0] = r_sc[...]


def _retention_consts(c):
    log_g = jnp.log1p(-jnp.exp2(-5.0 - jnp.arange(RET_HEADS, dtype=F32)))
    i = jnp.arange(c, dtype=F32)
    diff = i[:, None] - i[None, :]
    dmat = jnp.where(diff >= 0, jnp.exp(jnp.maximum(diff, 0.0)[None] * log_g[:, None, None]), 0.0)
    xi = jnp.exp((i[:, None] + 1.0) * log_g[None, :])
    zeta = jnp.exp((c - 1.0 - i)[:, None] * log_g[None, :])
    dec = jnp.broadcast_to(jnp.exp(c * log_g)[:, None, None], (RET_HEADS, 1, RET_DV))
    return dmat, xi, zeta, dec


def _retention(rq, rk, rv, rg, r0, gain):
    b, l, _ = rq.shape
    c = RET_CHUNK if l % RET_CHUNK == 0 else l
    n = l // c
    dmat, xi, zeta, dec = _retention_consts(c)
    tok = pl.BlockSpec((1, c, RET_WIDTH), lambda i, j: (i, j, 0))
    st = pl.BlockSpec((1, RET_HEADS, RET_DK, RET_DV), lambda i, j: (i, 0, 0, 0))
    cst = lambda a: pl.BlockSpec(a.shape, lambda i, j: (0,) * a.ndim)
    return pl.pallas_call(
        _retention_kernel,
        grid=(b, n),
        in_specs=[tok, tok, tok, tok, st, cst(dmat), cst(xi), cst(zeta), cst(dec), cst(gain)],
        out_specs=[tok, st],
        out_shape=[jax.ShapeDtypeStruct((b, l, RET_WIDTH), F32),
                   jax.ShapeDtypeStruct((b, RET_HEADS, RET_DK, RET_DV), F32)],
        scratch_shapes=[pltpu.VMEM((RET_HEADS, RET_DK, RET_DV), F32)],
        compiler_params=_cparams(("arbitrary", "arbitrary")),
        name="retention",
    )(rq, rk, rv, rg, r0, dmat, xi, zeta, dec, gain)


def _rglru_kernel(x_ref, gate_ref, buf_ref, h0_ref, cw_ref, cb_ref, wai_ref, bai_ref, lam_ref, gain_ref,
                  o_ref, conv_ref, hl_ref, xin_sc, a_sc, b_sc, h_sc, hc_sc):
    ci = pl.program_id(1)
    tc = x_ref.shape[1]
    pad = SUBLANES - (CONV_W - 1)

    @pl.when(ci == 0)
    def _():
        xin_sc[pad:SUBLANES, :] = buf_ref[0]
        hc_sc[...] = h0_ref[0]

    xin_sc[SUBLANES:SUBLANES + tc, :] = x_ref[0]
    xc = cb_ref[...]
    for kk in range(CONV_W):
        xc = xc + xin_sc[pad + kk:pad + kk + tc, :] * cw_ref[kk:kk + 1, :]
    tail = xin_sc[SUBLANES + tc - (CONV_W - 1):SUBLANES + tc, :]
    xin_sc[pad:SUBLANES, :] = tail

    z = jnp.dot(xc.astype(BF16), wai_ref[...], preferred_element_type=F32) + bai_ref[...]
    r = jax.nn.sigmoid(z[:, :LRU_WIDTH])
    ig = jax.nn.sigmoid(z[:, LRU_WIDTH:])
    log_a = -LRU_C * r * jax.nn.softplus(-lam_ref[...])
    a = jnp.exp(log_a)
    a_sc[...] = a
    b_sc[...] = jnp.sqrt(-jnp.tanh(log_a) * (a * a + 1.0)) * ig * xc

    def body(t, h):
        h = a_sc[pl.ds(t, 1), :] * h + b_sc[pl.ds(t, 1), :]
        h_sc[pl.ds(t, 1), :] = h
        return h

    h_last = lax.fori_loop(0, tc, body, hc_sc[...], unroll=8)
    hc_sc[...] = h_last
    y = h_sc[...] * jax.nn.gelu(gate_ref[0])
    o_ref[0] = _rms(y, gain_ref[...])

    @pl.when(ci == pl.num_programs(1) - 1)
    def _():
        conv_ref[0] = tail
        hl_ref[0] = h_last


def _rglru(lx, lg, buf0, h0, conv_w, conv_b, wai, bai, lam, gain, tc):
    b, l, w = lx.shape
    tok = pl.BlockSpec((1, tc, w), lambda i, j: (i, j, 0))
    cst = lambda a: pl.BlockSpec(a.shape, lambda i, j: (0,) * a.ndim)
    bufs = pl.BlockSpec((1, CONV_W - 1, w), lambda i, j: (i, 0, 0))
    hs = pl.BlockSpec((1, 1, w), lambda i, j: (i, 0, 0))
    return pl.pallas_call(
        _rglru_kernel,
        grid=(b, l // tc),
        in_specs=[tok, tok, bufs, hs, cst(conv_w), cst(conv_b), cst(wai), cst(bai), cst(lam), cst(gain)],
        out_specs=[tok, bufs, hs],
        out_shape=[jax.ShapeDtypeStruct((b, l, w), F32),
                   jax.ShapeDtypeStruct((b, CONV_W - 1, w), F32),
                   jax.ShapeDtypeStruct((b, 1, w), F32)],
        scratch_shapes=[pltpu.VMEM((SUBLANES + tc, w), F32), pltpu.VMEM((tc, w), F32), pltpu.VMEM((tc, w), F32),
                        pltpu.VMEM((tc, w), F32), pltpu.VMEM((1, w), F32)],
        compiler_params=_cparams(("arbitrary", "arbitrary")),
        name="rglru",
    )(lx, lg, buf0, h0, conv_w, conv_b, wai, bai, lam, gain)


def _softmax_update(s, v, m_sc, l_sc, acc_sc):
    n = s.shape[1]
    m_prev = m_sc[...]
    m_new = jnp.maximum(m_prev, jnp.max(s, axis=-1, keepdims=True))
    corr = jnp.exp(m_prev - m_new)
    if n % LANES == 0:
        p = jnp.exp(s - jnp.concatenate([m_new] * (n // LANES), axis=1))
    else:
        p = jnp.exp(s - m_new[:, :1])
    l_sc[...] = l_sc[...] * corr + jnp.sum(p, axis=-1, keepdims=True)
    acc_sc[...] = acc_sc[...] * corr + jnp.dot(p.astype(BF16), v, preferred_element_type=F32)
    m_sc[...] = m_new


def _attn_finish(wuv_ref, gain_ref, l_sc, acc_sc, rows_per_head):
    o = (acc_sc[...] / l_sc[...]).astype(BF16)
    om = None
    for hd in range(MLA_HEADS):
        part = jnp.dot(o[hd * rows_per_head:(hd + 1) * rows_per_head], wuv_ref[hd], preferred_element_type=F32)
        om = part if om is None else om + part
    return _rms(om, gain_ref[...])


def _flash_kernel(q_ref, k_ref, wuv_ref, gain_ref, o_ref, m_sc, l_sc, acc_sc, *, qb, kb):
    qi = pl.program_id(1)
    r = MLA_HEADS * qb
    q = q_ref[0].reshape(r, QCAT)
    m_sc[...] = jnp.full((r, LANES), -jnp.inf, F32)
    l_sc[...] = jnp.zeros((r, LANES), F32)
    acc_sc[...] = jnp.zeros((r, LANES), F32)
    nfull = (qi * qb) // kb

    def step(j, masked):
        start = pl.multiple_of(j * kb, kb)
        kblk = k_ref[0, pl.ds(start, kb), :]
        s = lax.dot_general(q, kblk, (((1,), (1,)), ((), ())), preferred_element_type=F32)
        if masked:
            row = lax.broadcasted_iota(jnp.int32, (r, kb), 0)
            col = lax.broadcasted_iota(jnp.int32, (r, kb), 1)
            s = jnp.where(j * kb + col <= qi * qb + row % qb, s, -jnp.inf)
        _softmax_update(s, kblk[:, :KV_RANK], m_sc, l_sc, acc_sc)

    def body(j, carry):
        step(j, False)
        return carry

    lax.fori_loop(0, nfull, body, 0)
    step(nfull, True)
    o_ref[0] = _attn_finish(wuv_ref, gain_ref, l_sc, acc_sc, qb)


def _mla_prompt(qcat, kcat, wuv_exp, gain, qb, kb):
    b, _, s, _ = qcat.shape
    r = MLA_HEADS * qb
    return pl.pallas_call(
        functools.partial(_flash_kernel, qb=qb, kb=kb),
        grid=(b, s // qb),
        in_specs=[pl.BlockSpec((1, MLA_HEADS, qb, QCAT), lambda i, j: (i, 0, j, 0)),
                  pl.BlockSpec((1, s, QCAT), lambda i, j: (i, 0, 0)),
                  pl.BlockSpec(wuv_exp.shape, lambda i, j: (0, 0, 0)),
                  pl.BlockSpec(gain.shape, lambda i, j: (0, 0))],
        out_specs=pl.BlockSpec((1, qb, MLA_WIDTH), lambda i, j: (i, j, 0)),
        out_shape=jax.ShapeDtypeStruct((b, s, MLA_WIDTH), F32),
        scratch_shapes=[pltpu.VMEM((r, LANES), F32)] * 3,
        compiler_params=_cparams(("arbitrary", "arbitrary")),
        name="mla_prompt",
    )(qcat, kcat, wuv_exp, gain)


def _mla_sample_kernel(pt_ref, q_ref, knew_ref, *rest, n_group, t_new):
    ck_refs = rest[:n_group]
    kp_refs = rest[n_group:2 * n_group]
    wuv_ref, gain_ref, o_ref, m_sc, l_sc, acc_sc = rest[2 * n_group:]
    gi = pl.program_id(1)
    r = MLA_HEADS * t_new
    q = q_ref[0].reshape(r, QCAT)
    qb = q.astype(BF16)
    q_rot = q[:, KV_RANK:].astype(F32)
    q_r = q_rot[:, :MLA_ROPE]
    for i in range(1, LANES // MLA_ROPE):
        q_r = q_r + q_rot[:, i * MLA_ROPE:(i + 1) * MLA_ROPE]
    q_r = q_r.astype(BF16)

    @pl.when(gi == 0)
    def _():
        m_sc[...] = jnp.full((r, LANES), -jnp.inf, F32)
        l_sc[...] = jnp.zeros((r, LANES), F32)
        acc_sc[...] = jnp.zeros((r, LANES), F32)
        kn = knew_ref[0].astype(BF16)
        s = lax.dot_general(qb, kn, (((1,), (1,)), ((), ())), preferred_element_type=F32)
        row = lax.broadcasted_iota(jnp.int32, s.shape, 0)
        col = lax.broadcasted_iota(jnp.int32, s.shape, 1)
        s = jnp.where(row % t_new >= col, s, -jnp.inf)
        _softmax_update(s, kn[:, :KV_RANK], m_sc, l_sc, acc_sc)

    for i in range(n_group):
        ck = ck_refs[i][0, 0].astype(BF16)
        kp = kp_refs[i][0, 0].astype(BF16)
        s = (lax.dot_general(qb[:, :KV_RANK], ck, (((1,), (1,)), ((), ())), preferred_element_type=F32)
             + lax.dot_general(q_r, kp, (((1,), (1,)), ((), ())), preferred_element_type=F32))
        _softmax_update(s, ck, m_sc, l_sc, acc_sc)

    @pl.when(gi == pl.num_programs(1) - 1)
    def _():
        o_ref[0] = _attn_finish(wuv_ref, gain_ref, l_sc, acc_sc, t_new)


def _mla_sample(qcat, kcat, cache_ckv, cache_kr, page_table, li, wuv_exp, gain, n_group):
    b, _, t, _ = qcat.shape
    n_pages = page_table.shape[1]
    page, rank = cache_ckv.shape[2], cache_ckv.shape[3]
    rope_w = cache_kr.shape[3]
    r = MLA_HEADS * t

    def ck_spec(i):
        return pl.BlockSpec((1, 1, page, rank), lambda bi, gi, pt: (li, pt[bi, gi * n_group + i], 0, 0))

    def kp_spec(i):
        return pl.BlockSpec((1, 1, page, rope_w), lambda bi, gi, pt: (li, pt[bi, gi * n_group + i], 0, 0))

    grid_spec = pltpu.PrefetchScalarGridSpec(
        num_scalar_prefetch=1,
        grid=(b, n_pages // n_group),
        in_specs=[pl.BlockSpec((1, MLA_HEADS, t, QCAT), lambda bi, gi, pt: (bi, 0, 0, 0)),
                  pl.BlockSpec((1, t, QCAT), lambda bi, gi, pt: (bi, 0, 0))]
        + [ck_spec(i) for i in range(n_group)] + [kp_spec(i) for i in range(n_group)]
        + [pl.BlockSpec(wuv_exp.shape, lambda bi, gi, pt: (0, 0, 0)),
           pl.BlockSpec(gain.shape, lambda bi, gi, pt: (0, 0))],
        out_specs=pl.BlockSpec((1, t, MLA_WIDTH), lambda bi, gi, pt: (bi, 0, 0)),
        scratch_shapes=[pltpu.VMEM((r, LANES), F32)] * 3,
    )
    return pl.pallas_call(
        functools.partial(_mla_sample_kernel, n_group=n_group, t_new=t),
        grid_spec=grid_spec,
        out_shape=jax.ShapeDtypeStruct((b, t, MLA_WIDTH), F32),
        compiler_params=_cparams(("arbitrary", "arbitrary")),
        name="mla_sample",
    )(page_table, qcat, kcat, *([cache_ckv] * n_group), *([cache_kr] * n_group), wuv_exp, gain)


def _outproj_kernel(ret_ref, lru_ref, mla_ref, x_ref, w_ref, g1_ref, gt1_ref, g2_ref, sc2_ref, sh2_ref,
                    x1_ref, h2_ref):
    bt, lt, d = x_ref.shape
    rows = bt * lt
    mix = jnp.concatenate([ret_ref[...].reshape(rows, RET_WIDTH), lru_ref[...].reshape(rows, LRU_WIDTH),
                           mla_ref[...].reshape(rows, MLA_WIDTH)], axis=1).astype(BF16)
    y = jnp.dot(mix, w_ref[...], preferred_element_type=F32).reshape(bt, lt, d)
    x1 = x_ref[...] + gt1_ref[...] * _rms(y, g1_ref[...])
    x1_ref[...] = x1
    h2_ref[...] = _rms(x1, g2_ref[...]) * (1.0 + sc2_ref[...]) + sh2_ref[...]


def _outproj(ret, lru, mla, x, w_out, g1, gt1, g2, sc2, sh2, bt, lt):
    b, l, d = x.shape
    tok = lambda w: pl.BlockSpec((bt, lt, w), lambda i, j: (i, j, 0))
    mod = pl.BlockSpec((bt, 1, d), lambda i, j: (i, 0, 0))
    gsp = pl.BlockSpec((1, 1, d), lambda i, j: (0, 0, 0))
    return pl.pallas_call(
        _outproj_kernel,
        grid=(b // bt, l // lt),
        in_specs=[tok(RET_WIDTH), tok(LRU_WIDTH), tok(MLA_WIDTH), tok(d),
                  pl.BlockSpec(w_out.shape, lambda i, j: (0, 0)), gsp, mod, gsp, mod, mod],
        out_specs=[tok(d), tok(d)],
        out_shape=[jax.ShapeDtypeStruct((b, l, d), F32)] * 2,
        compiler_params=_cparams(("arbitrary", "arbitrary")),
        name="outproj",
    )(ret, lru, mla, x, w_out, g1, gt1, g2, sc2, sh2)


def _split_dot_t(w, x):
    dn = (((1,), (1,)), ((), ()))
    wh = w.astype(BF16)
    wl = (w - wh.astype(F32)).astype(BF16)
    xh = x.astype(BF16)
    xl = (x - xh.astype(F32)).astype(BF16)
    return (lax.dot_general(wh, xh, dn, preferred_element_type=F32)
            + lax.dot_general(wh, xl, dn, preferred_element_type=F32)
            + lax.dot_general(wl, xh, dn, preferred_element_type=F32))


def _router_kernel(x_ref, w_ref, b_ref, upper_ref, lower_ref, eid_ref, pos_ref, gate_ref, cnt_ref, cnt_sc):
    ti = pl.program_id(0)
    tm = x_ref.shape[0]

    @pl.when(ti == 0)
    def _():
        cnt_sc[...] = jnp.zeros(cnt_sc.shape, F32)

    logits = _split_dot_t(w_ref[...], x_ref[...])
    scores = jax.nn.sigmoid(logits)
    biased = scores + b_ref[...]
    groups = [biased[g * GROUP_SIZE:(g + 1) * GROUP_SIZE, :] for g in range(N_GROUPS)]
    sub = lax.broadcasted_iota(jnp.int32, (GROUP_SIZE, tm), 0)
    gs = []
    for bg in groups:
        m1 = jnp.max(bg, axis=0, keepdims=True)
        i1 = jnp.min(jnp.where(bg == m1, sub, GROUP_SIZE), axis=0, keepdims=True)
        m2 = jnp.max(jnp.where(sub == i1, -jnp.inf, bg), axis=0, keepdims=True)
        gs.append(m1 + m2)
    masked = []
    for g in range(N_GROUPS):
        rank = jnp.zeros((1, tm), F32)
        for g2 in range(N_GROUPS):
            if g2 != g:
                ahead = (gs[g2] >= gs[g]) if g2 < g else (gs[g2] > gs[g])
                rank = rank + jnp.where(ahead, 1.0, 0.0)
        masked.append(jnp.where(rank < TOPK_GROUPS, groups[g], -jnp.inf))
    sels = []
    for g in range(N_GROUPS):
        mine = masked[g]
        rank = jnp.zeros((GROUP_SIZE, tm), F32)
        for g2 in range(N_GROUPS):
            for s2 in range(GROUP_SIZE):
                other = jnp.broadcast_to(masked[g2][s2:s2 + 1, :], (GROUP_SIZE, tm))
                if g2 < g:
                    rank = rank + jnp.where(other >= mine, 1.0, 0.0)
                elif g2 > g:
                    rank = rank + jnp.where(other > mine, 1.0, 0.0)
                else:
                    tie = jnp.where(sub > s2, 1.0, 0.0)
                    rank = rank + jnp.where(other > mine, 1.0, jnp.where(other == mine, tie, 0.0))
        sels.append(jnp.where(rank < TOP_K, 1.0, 0.0))
    self32 = jnp.concatenate(sels, axis=0)
    sel = self32 > 0.5
    picked = jnp.where(sel, scores, 0.0)
    gate = picked / jnp.sum(picked, axis=0, keepdims=True) * ROUTED_SCALE
    selb = self32.astype(BF16)
    pos = jnp.dot(selb, upper_ref[...], preferred_element_type=F32) + cnt_sc[:, :1]
    cnt_sc[...] = cnt_sc[...] + jnp.sum(self32, axis=1, keepdims=True)
    slot = jnp.dot(lower_ref[...], selb, preferred_element_type=F32)
    efl = lax.broadcasted_iota(jnp.int32, (N_EXPERTS, tm), 0).astype(F32)
    for kk in range(TOP_K):
        hit = sel & (slot == float(kk))
        eid_ref[kk:kk + 1, :] = jnp.sum(jnp.where(hit, efl, 0.0), axis=0, keepdims=True).astype(jnp.int32)
        pos_ref[kk:kk + 1, :] = jnp.sum(jnp.where(hit, pos, 0.0), axis=0, keepdims=True).astype(jnp.int32)
        gate_ref[kk:kk + 1, :] = jnp.sum(jnp.where(hit, gate, 0.0), axis=0, keepdims=True)

    @pl.when(ti == pl.num_programs(0) - 1)
    def _():
        cnt_ref[...] = cnt_sc[...]


def _router(h2, w_rt, b_r, tm):
    t, d = h2.shape
    upper = jnp.triu(jnp.ones((tm, tm), F32), 1).astype(BF16)
    lower = jnp.tril(jnp.ones((N_EXPERTS, N_EXPERTS), F32), -1).astype(BF16)
    kt = pl.BlockSpec((TOP_K, tm), lambda i: (0, i))
    cst = lambda a: pl.BlockSpec(a.shape, lambda i: (0,) * a.ndim)
    return pl.pallas_call(
        _router_kernel,
        grid=(t // tm,),
        in_specs=[pl.BlockSpec((tm, d), lambda i: (i, 0)), cst(w_rt), cst(b_r), cst(upper), cst(lower)],
        out_specs=[kt, kt, kt, pl.BlockSpec((N_EXPERTS, LANES), lambda i: (0, 0))],
        out_shape=[jax.ShapeDtypeStruct((TOP_K, t), jnp.int32), jax.ShapeDtypeStruct((TOP_K, t), jnp.int32),
                   jax.ShapeDtypeStruct((TOP_K, t), F32), jax.ShapeDtypeStruct((N_EXPERTS, LANES), F32)],
        scratch_shapes=[pltpu.VMEM((N_EXPERTS, LANES), F32)],
        compiler_params=_cparams(("arbitrary",)),
        name="router",
    )(h2, w_rt, b_r, upper, lower)


def _dispatch_kernel(dest_ref, x_ref, rows_in_ref, rows_ref, sem):
    del rows_in_ref
    tm = x_ref.shape[0]
    n = tm * TOP_K

    def row_copy(i, d):
        return pltpu.make_async_copy(x_ref.at[pl.ds(i // TOP_K, 1), :], rows_ref.at[pl.ds(d, 1), :], sem)

    def issue(i, c):
        row_copy(i, dest_ref[i]).start()
        return c

    def drain(i, c):
        row_copy(i, dest_ref[i]).wait()
        return c

    lax.fori_loop(0, n, issue, 0)
    lax.fori_loop(0, n, drain, 0)


def _dispatch(dest_flat, h2, n_rows, tm):
    t, d = h2.shape
    rows0 = jnp.zeros((n_rows, d), F32)
    return pl.pallas_call(
        _dispatch_kernel,
        grid=(t // tm,),
        in_specs=[pl.BlockSpec((tm * TOP_K,), lambda i: (i,), memory_space=pltpu.SMEM),
                  pl.BlockSpec((tm, d), lambda i: (i, 0)),
                  pl.BlockSpec(memory_space=pl.ANY)],
        out_specs=pl.BlockSpec(memory_space=pl.ANY),
        out_shape=jax.ShapeDtypeStruct((n_rows, d), F32),
        scratch_shapes=[pltpu.SemaphoreType.DMA(())],
        input_output_aliases={2: 0},
        compiler_params=_cparams(("arbitrary",)),
        name="moe_dispatch",
    )(dest_flat, h2, rows0)


def _expert_kernel(be_ref, nu_ref, rows_ref, wgu_ref, wd_ref, o_ref, wgu_sc, wd_sc):
    i = pl.program_id(0)
    prev = be_ref[jnp.maximum(i - 1, 0)]
    fresh = (i == 0) | (be_ref[i] != prev)

    @pl.when(fresh)
    def _():
        wgu_sc[...] = wgu_ref[...].astype(BF16)
        wd_sc[...] = wd_ref[...].astype(BF16)

    @pl.when(i < nu_ref[0])
    def _():
        gu = jnp.dot(rows_ref[...].astype(BF16), wgu_sc[...], preferred_element_type=F32)
        act = _silu(gu[:, :D_EXPERT]) * gu[:, D_EXPERT:]
        o_ref[...] = jnp.dot(act.astype(BF16), wd_sc[...], preferred_element_type=F32)


def _experts(block_e, n_used, rows, w_gu, w_down, li, blk):
    n_rows, d = rows.shape
    n_blocks = n_rows // blk
    de2 = w_gu.shape[-1]
    de = w_down.shape[-2]

    def rmap(i, be, nu):
        return (jnp.minimum(i, nu[0] - 1), 0)

    grid_spec = pltpu.PrefetchScalarGridSpec(
        num_scalar_prefetch=2,
        grid=(n_blocks,),
        in_specs=[pl.BlockSpec((blk, d), rmap),
                  pl.BlockSpec((None, None, d, de2), lambda i, be, nu: (li, be[i], 0, 0)),
                  pl.BlockSpec((None, None, de, d), lambda i, be, nu: (li, be[i], 0, 0))],
        out_specs=pl.BlockSpec((blk, d), rmap),
        scratch_shapes=[pltpu.VMEM((d, de2), BF16), pltpu.VMEM((de, d), BF16)],
    )
    return pl.pallas_call(
        _expert_kernel,
        grid_spec=grid_spec,
        out_shape=jax.ShapeDtypeStruct((n_rows, d), F32),
        compiler_params=_cparams(("arbitrary",)),
        name="moe_experts",
    )(block_e, n_used, rows, w_gu, w_down)


def _combine_kernel(dest_ref, gate_ref, orow_ref, h2_ref, x1_ref, wsgu_ref, wsd_ref, g3_ref, gt2_ref,
                    y_ref, buf, sem):
    bt, lt, d = x1_ref.shape
    tm = bt * lt
    n = tm * TOP_K

    def row_copy(i, dst):
        return pltpu.make_async_copy(orow_ref.at[pl.ds(dst, 1), :],
                                     buf.at[i % TOP_K, pl.ds(i // TOP_K, 1), :], sem)

    def issue(i, c):
        row_copy(i, dest_ref[i]).start()
        return c

    def drain(i, c):
        row_copy(i, dest_ref[i]).wait()
        return c

    lax.fori_loop(0, n, issue, 0)
    hb = h2_ref[...].astype(BF16)
    gu = jnp.dot(hb, wsgu_ref[...], preferred_element_type=F32)
    ds = wsd_ref.shape[0]
    f = jnp.dot((_silu(gu[:, :ds]) * gu[:, ds:]).astype(BF16), wsd_ref[...], preferred_element_type=F32)
    lax.fori_loop(0, n, drain, 0)
    gate = gate_ref[...]
    for kk in range(TOP_K):
        f = f + buf[kk] * gate[:, kk:kk + 1]
    y_ref[...] = x1_ref[...] + gt2_ref[...] * _rms(f, g3_ref[...]).reshape(bt, lt, d)


def _combine(dest_flat, gate_tk, out_rows, h2, x1, ws_gu, ws_down, g3, gt2, bt, lt):
    b, l, d = x1.shape
    tm = bt * lt
    nl = l // lt
    tok = pl.BlockSpec((bt, lt, d), lambda i, j: (i, j, 0))
    flat = lambda w: pl.BlockSpec((tm, w), lambda i, j: (i * nl + j, 0))
    return pl.pallas_call(
        _combine_kernel,
        grid=(b // bt, nl),
        in_specs=[pl.BlockSpec((tm * TOP_K,), lambda i, j: (i * nl + j,), memory_space=pltpu.SMEM),
                  flat(TOP_K), pl.BlockSpec(memory_space=pl.ANY), flat(d), tok,
                  pl.BlockSpec(ws_gu.shape, lambda i, j: (0, 0)), pl.BlockSpec(ws_down.shape, lambda i, j: (0, 0)),
                  pl.BlockSpec((1, 1, d), lambda i, j: (0, 0, 0)),
                  pl.BlockSpec((bt, 1, d), lambda i, j: (i, 0, 0))],
        out_specs=tok,
        out_shape=jax.ShapeDtypeStruct((b, l, d), F32),
        scratch_shapes=[pltpu.VMEM((TOP_K, tm, d), F32), pltpu.SemaphoreType.DMA(())],
        compiler_params=_cparams(("arbitrary", "arbitrary")),
        name="moe_combine",
    )(dest_flat, gate_tk, out_rows, h2, x1, ws_gu, ws_down, g3, gt2)


def _moe(h2, x1, gt2, g3, w_rt, b_r, w_exp_gu, w_exp_down, ws_gu, ws_down, li, bt, lt, blk):
    b, l, d = x1.shape
    t = b * l
    h2f = h2.reshape(t, d)
    tm_r = min(512, t)
    eid, pos, gate, cnt = _router(h2f, w_rt, b_r, tm_r)
    counts = cnt[:, 0].astype(jnp.int32)
    padded = (counts + blk - 1) // blk * blk
    pends = jnp.cumsum(padded)
    pstart = pends - padded
    n_blocks = -(-(t * TOP_K) // blk) + N_EXPERTS
    block_e = jnp.minimum(jnp.searchsorted(pends, jnp.arange(n_blocks, dtype=jnp.int32) * blk, side='right'),
                          N_EXPERTS - 1).astype(jnp.int32)
    n_used = (pends[-1:] // blk).astype(jnp.int32)
    dest = (pstart[eid] + pos).T.reshape(t * TOP_K)
    gate_tk = gate.T
    tm_d = min(256, t)
    rows = _dispatch(dest, h2f, n_blocks * blk, tm_d)
    out_rows = _experts(block_e, n_used, rows, w_exp_gu, w_exp_down, li, blk)
    return _combine(dest, gate_tk, out_rows, h2f, x1, ws_gu, ws_down, g3, gt2, bt, lt)


def _proj_perm():
    base = np.arange(C_QN)
    mq0 = C_QN
    qn = np.concatenate([mq0 + h * (MLA_NOPE + MLA_ROPE) + np.arange(MLA_NOPE) for h in range(MLA_HEADS)])
    qr = np.concatenate([mq0 + h * (MLA_NOPE + MLA_ROPE) + MLA_NOPE + np.arange(MLA_ROPE) for h in range(MLA_HEADS)])
    mc0 = mq0 + MLA_HEADS * (MLA_NOPE + MLA_ROPE)
    mc = mc0 + np.arange(KV_RANK)
    mk = np.tile(mc0 + KV_RANK + np.arange(MLA_ROPE), LANES // MLA_ROPE)
    return np.concatenate([base, qn, qr, mc, mk])


def _rope_tables(pos, dim):
    half = dim // 2
    inv = ROPE_BASE ** (-jnp.arange(0, dim, 2, dtype=F32) / dim)
    ang = pos.astype(F32)[:, None] * inv[None, :]
    cos, sin = jnp.cos(ang), jnp.sin(ang)
    reps = LANES // dim
    cos_t = jnp.tile(jnp.concatenate([cos, cos], axis=1), (1, reps))
    sin_t = jnp.tile(jnp.concatenate([-sin, sin], axis=1), (1, reps))
    return cos_t, sin_t


def _block_diag(blocks):
    n, a, b = blocks.shape
    eye = jnp.eye(n, dtype=blocks.dtype)
    return (eye[:, None, :, None] * blocks[:, :, None, :]).reshape(n * a, n * b)


def _layer(x, mod, pos_tabs, li, p, past, tiles):
    b, l, d = x.shape
    bt, lt = tiles['bt'], tiles['lt']
    sh1, sc1, gt1, sh2, sc2, gt2 = [m[:, None, :] for m in jnp.split(mod, 6, axis=-1)]
    g = p['norm_gains']
    gains = [g[i][None, None, :] for i in range(4)]
    gain = p['grp_gain']
    rq, rk, rv, rg, lx, lg, qcat, kcat, ckv, kr = _inproj(
        x, sc1, sh1, gains[0], p['w_in_p'], p['wuk_bd'], p['kv_norm'][None, :], pos_tabs, bt, lt, tiles['qdtype'])
    if past is None:
        r0 = jnp.zeros((b, RET_HEADS, RET_DK, RET_DV), F32)
        buf0 = jnp.zeros((b, CONV_W - 1, LRU_WIDTH), F32)
        h0 = jnp.zeros((b, 1, LRU_WIDTH), F32)
    else:
        r0, buf0, h0 = past['ret'], past['conv'], past['lru'][:, None, :]
    ret_out, r_new = _retention(rq, rk, rv, rg, r0, gain[None, :RET_WIDTH])
    lru_out, conv_new, h_new = _rglru(
        lx, lg, buf0, h0, p['conv_w'], p['conv_b'][None, :], p['wai'], p['bai'], p['lru_lambda'][None, :],
        gain[None, RET_WIDTH:RET_WIDTH + LRU_WIDTH], tiles['tc'])
    mla_gain = gain[None, RET_WIDTH + LRU_WIDTH:]
    if past is None:
        mla_out = _mla_prompt(qcat, kcat, p['wuv_exp'], mla_gain, tiles['qb'], tiles['kb'])
    else:
        mla_out = _mla_sample(qcat, kcat, past['cache_ckv'], past['cache_kr'], past['page_table'], li,
                              p['wuv_exp'], mla_gain, tiles['n_group'])
    x1, h2 = _outproj(ret_out, lru_out, mla_out, x, p['w_out'], gains[1], gt1, gains[2], sc2, sh2, bt, lt)
    y = _moe(h2, x1, gt2, gains[3], p['w_rt'], p['b_router'][:, None], p['w_exp_gu'], p['w_exp_down'],
             p['ws_gu'], p['ws_down'], li, tiles['cbt'], tiles['clt'], tiles['blk'])
    return y, (ckv, kr, r_new, conv_new, h_new[:, 0])


def _largest_tile(n, cap):
    t = min(n, cap)
    while n % t:
        t //= 2
    return t


def kernel(x_prompt, x_sample, cache_ckv, cache_krope, state_ret, state_conv, state_lru, page_table, c_prompt, c_sample, w_ada, b_ada, norm_gains, w_in, w_out, grp_gain, conv_w, conv_b, w_lru_a, b_lru_a, w_lru_i, b_lru_i, lru_lambda, kv_norm, w_ukv, w_router, b_router, w_exp_gu, w_exp_down, w_sh_gu, w_sh_down):
    depth = w_in.shape[0]
    bp, lp, d = x_prompt.shape
    bs, ls, _ = x_sample.shape
    past_len = page_table.shape[1] * cache_ckv.shape[2]

    n_c = bp + bs
    n_cp = -(-n_c // SUBLANES) * SUBLANES
    c_all = jnp.concatenate([c_prompt, c_sample, jnp.zeros((n_cp - n_c, d), F32)], axis=0)
    mod_all = _adaln(c_all, w_ada, b_ada[:, None, :])

    w_in_p = w_in[:, :, _proj_perm()].astype(BF16)
    w_uk = w_ukv[..., :MLA_NOPE]
    w_uv = w_ukv[..., MLA_NOPE:]
    eye_h = jnp.eye(MLA_HEADS, dtype=F32)

    pos_p = jnp.arange(lp)
    pos_s = past_len + jnp.arange(ls)
    tabs_p = _rope_tables(pos_p, RET_DK) + _rope_tables(pos_p, MLA_ROPE)
    tabs_s = _rope_tables(pos_s, RET_DK) + _rope_tables(pos_s, MLA_ROPE)

    lt_p = _largest_tile(lp, 256)
    tiles_p = dict(bt=1, lt=lt_p, qdtype=BF16, tc=_largest_tile(lp, 256), qb=_largest_tile(lp, 64),
                   kb=_largest_tile(lp, 512), cbt=1, clt=_largest_tile(lp, 128), blk=256 if bp * lp >= 4096 else 128)
    bt_s = _largest_tile(bs, 32)
    tiles_s = dict(bt=bt_s, lt=ls, qdtype=F32, tc=ls, n_group=_largest_tile(page_table.shape[1], 8),
                   cbt=_largest_tile(bs, 16), clt=ls, blk=128)

    y_p, y_s = x_prompt, x_sample
    st_p, st_s = [], []
    for li in range(depth):
        wuk_bd = _block_diag(jnp.transpose(w_uk[li], (1, 2, 0))).astype(BF16)
        wuv_h = jnp.transpose(w_uv[li], (1, 0, 2))
        wuv_exp = (wuv_h[:, :, None, :] * eye_h[:, None, :, None]).reshape(
            MLA_HEADS, KV_RANK, MLA_WIDTH).astype(BF16)
        wai = jnp.concatenate([_block_diag(w_lru_a[li]), _block_diag(w_lru_i[li])], axis=1).astype(BF16)
        bai = jnp.concatenate([b_lru_a[li], b_lru_i[li]])[None, :]
        p = dict(norm_gains=norm_gains[li], w_in_p=w_in_p[li], wuk_bd=wuk_bd, wuv_exp=wuv_exp,
                 w_out=w_out[li].astype(BF16), grp_gain=grp_gain[li], conv_w=conv_w[li], conv_b=conv_b[li],
                 wai=wai, bai=bai, lru_lambda=lru_lambda[li], kv_norm=kv_norm[li],
                 w_rt=w_router[li].T, b_router=b_router[li], w_exp_gu=w_exp_gu, w_exp_down=w_exp_down,
                 ws_gu=w_sh_gu[li].astype(BF16), ws_down=w_sh_down[li].astype(BF16))
        y_p, sp = _layer(y_p, mod_all[li, :bp], tabs_p, li, p, None, tiles_p)
        past = dict(ret=state_ret[li], conv=state_conv[li], lru=state_lru[li], cache_ckv=cache_ckv,
                    cache_kr=cache_krope, page_table=page_table)
        y_s, ss = _layer(y_s, mod_all[li, bp:bp + bs], tabs_s, li, p, past, tiles_s)
        st_p.append(sp)
        st_s.append(ss)
    stack = lambda sts, i: jnp.stack([s[i] for s in sts])
    return (y_p, y_s) + tuple(stack(st_p, i) for i in range(5)) + tuple(stack(st_s, i) for i in range(5))
```

```python
import functools

import numpy as np
import jax
import jax.numpy as jnp
from jax import lax
from jax.experimental import pallas as pl
from jax.experimental.pallas import tpu as pltpu

F32 = jnp.float32
BF16 = jnp.bfloat16
EPS = 1e-6
ROPE_BASE = 10000.0

RET_HEADS = 4
RET_DK = 64
RET_DV = 64
RET_WIDTH = RET_HEADS * RET_DV
RET_CHUNK = 128
LRU_WIDTH = 256
LRU_BLOCKS = 4
LRU_BLOCK = LRU_WIDTH // LRU_BLOCKS
CONV_W = 4
LRU_C = 8.0
MLA_HEADS = 8
MLA_NOPE = 64
MLA_ROPE = 32
MLA_V = 64
KV_RANK = 128
MLA_WIDTH = MLA_HEADS * MLA_V
N_EXPERTS = 64
TOP_K = 8
N_GROUPS = 8
GROUP_SIZE = N_EXPERTS // N_GROUPS
TOPK_GROUPS = 4
D_EXPERT = 256
ROUTED_SCALE = 2.5

LANES = 128
SUBLANES = 8
QCAT = 2 * LANES
C_RQ, C_RK, C_RV, C_RG, C_LX, C_LG = 0, 256, 512, 768, 1024, 1280
C_QN = 1536
C_QR = C_QN + MLA_HEADS * MLA_NOPE
C_MC = C_QR + MLA_HEADS * MLA_ROPE
C_MK = C_MC + KV_RANK
N_PROJ = C_MK + LANES


def _cparams(sem, vmem_mb=48):
    return pltpu.CompilerParams(dimension_semantics=sem, vmem_limit_bytes=vmem_mb * 2 ** 20)


def _rms(x, g):
    return x * lax.rsqrt(jnp.mean(x * x, axis=-1, keepdims=True) + EPS) * g


def _silu(x):
    return x * jax.nn.sigmoid(x)


def _adaln_kernel(c_ref, w_ref, b_ref, o_ref):
    c = c_ref[...]
    s = _silu(c).astype(BF16)
    o_ref[...] = jnp.dot(s, w_ref[...].astype(BF16), preferred_element_type=F32) + b_ref[...]


def _adaln(c, w_ada, b_ada):
    depth, d, n = w_ada.shape
    bp = c.shape[0]
    tn = 768
    return pl.pallas_call(
        _adaln_kernel,
        grid=(depth, n // tn),
        in_specs=[pl.BlockSpec((bp, d), lambda l, j: (0, 0)),
                  pl.BlockSpec((None, d, tn), lambda l, j: (l, 0, j)),
                  pl.BlockSpec((None, 1, tn), lambda l, j: (l, 0, j))],
        out_specs=pl.BlockSpec((None, bp, tn), lambda l, j: (l, 0, j)),
        out_shape=jax.ShapeDtypeStruct((depth, bp, n), F32),
        compiler_params=_cparams(("arbitrary", "arbitrary")),
        name="adaln",
    )(c, w_ada, b_ada)


def _rope_slab(v, cos, sin_signed, half):
    lane = lax.broadcasted_iota(jnp.int32, v.shape, 1)
    first = (lane % (2 * half)) < half
    partner = jnp.where(first, pltpu.roll(v, LANES - half, 1), pltpu.roll(v, half, 1))
    return v * cos + partner * sin_signed


def _rope(v, cos, sin_signed, half):
    n = v.shape[1] // LANES
    outs = [_rope_slab(v[:, i * LANES:(i + 1) * LANES], cos, sin_signed, half) for i in range(n)]
    return outs[0] if n == 1 else jnp.concatenate(outs, axis=1)


def _inproj_kernel(x_ref, sc_ref, sh_ref, g_ref, w_ref, wuk_ref, kvg_ref, c64_ref, s64_ref, c32_ref, s32_ref,
                   rq_ref, rk_ref, rv_ref, rg_ref, lx_ref, lg_ref, qcat_ref, kcat_ref, ckv_ref, kr_ref):
    bt, lt, d = x_ref.shape
    rows = bt * lt
    h = _rms(x_ref[...], g_ref[...]) * (1.0 + sc_ref[...]) + sh_ref[...]
    hb = h.reshape(rows, d).astype(BF16)

    def proj(a, b):
        return jnp.dot(hb, w_ref[:, a:b], preferred_element_type=F32)

    def table(ref):
        t = ref[...]
        if bt > 1:
            t = jnp.broadcast_to(t[None], (bt, lt, LANES)).reshape(rows, LANES)
        return t

    c64, s64, c32, s32 = table(c64_ref), table(s64_ref), table(c32_ref), table(s32_ref)
    rq_ref[...] = _rope(proj(C_RQ, C_RK), c64, s64, RET_DK // 2).reshape(bt, lt, RET_WIDTH)
    rk_ref[...] = _rope(proj(C_RK, C_RV), c64, s64, RET_DK // 2).reshape(bt, lt, RET_WIDTH)
    rv_ref[...] = proj(C_RV, C_RG).reshape(bt, lt, RET_WIDTH)
    rg_ref[...] = proj(C_RG, C_LX).reshape(bt, lt, RET_WIDTH)
    lx_ref[...] = proj(C_LX, C_LG).reshape(bt, lt, LRU_WIDTH)
    lg_ref[...] = proj(C_LG, C_QN).reshape(bt, lt, LRU_WIDTH)

    scale = (MLA_NOPE + MLA_ROPE) ** -0.5
    q_lat = jnp.dot(proj(C_QN, C_QR).astype(BF16), wuk_ref[...], preferred_element_type=F32) * scale
    q_rope = _rope(proj(C_QR, C_MC), c32, s32, MLA_ROPE // 2) * scale
    lane = lax.broadcasted_iota(jnp.int32, (rows, LANES), 1)
    heads_per_slab = LANES // MLA_ROPE
    for hd in range(MLA_HEADS):
        slab = q_rope[:, (hd // heads_per_slab) * LANES:(hd // heads_per_slab + 1) * LANES]
        own = jnp.where(lane // MLA_ROPE == hd % heads_per_slab, slab, 0.0)
        qh = jnp.concatenate([q_lat[:, hd * KV_RANK:(hd + 1) * KV_RANK], own], axis=1)
        qcat_ref[:, hd, :, :] = qh.reshape(bt, lt, QCAT).astype(qcat_ref.dtype)

    ckv = _rms(proj(C_MC, C_MK), kvg_ref[...])
    kr4 = _rope(proj(C_MK, N_PROJ), c32, s32, MLA_ROPE // 2)
    ckv_ref[...] = ckv.reshape(bt, lt, KV_RANK)
    kr_ref[...] = kr4[:, :MLA_ROPE].reshape(bt, lt, MLA_ROPE)
    kcat_ref[...] = jnp.concatenate([ckv, kr4], axis=1).reshape(bt, lt, QCAT).astype(kcat_ref.dtype)


def _inproj(x, sc, sh, g, w_in_p, wuk_bd, kv_gain, tabs, bt, lt, qdtype):
    b, l, d = x.shape
    grid = (b // bt, l // lt)
    c64, s64, c32, s32 = tabs
    tok = lambda w: pl.BlockSpec((bt, lt, w), lambda i, j: (i, j, 0))
    mod = pl.BlockSpec((bt, 1, d), lambda i, j: (i, 0, 0))
    const2 = lambda a: pl.BlockSpec(a.shape, lambda i, j: (0, 0))
    tab = pl.BlockSpec((lt, LANES), lambda i, j: (j, 0))
    out_shape = [jax.ShapeDtypeStruct((b, l, 256), F32)] * 6 + [
        jax.ShapeDtypeStruct((b, MLA_HEADS, l, QCAT), qdtype),
        jax.ShapeDtypeStruct((b, l, QCAT), qdtype),
        jax.ShapeDtypeStruct((b, l, KV_RANK), F32),
        jax.ShapeDtypeStruct((b, l, MLA_ROPE), F32)]
    out_specs = [tok(256)] * 6 + [
        pl.BlockSpec((bt, MLA_HEADS, lt, QCAT), lambda i, j: (i, 0, j, 0)),
        tok(QCAT), tok(KV_RANK), tok(MLA_ROPE)]
    return pl.pallas_call(
        _inproj_kernel,
        grid=grid,
        in_specs=[tok(d), mod, mod, pl.BlockSpec((1, 1, d), lambda i, j: (0, 0, 0)),
                  const2(w_in_p), const2(wuk_bd), const2(kv_gain), tab, tab, tab, tab],
        out_specs=out_specs,
        out_shape=out_shape,
        compiler_params=_cparams(("arbitrary", "arbitrary")),
        name="inproj",
    )(x, sc, sh, g, w_in_p, wuk_bd, kv_gain, c64, s64, c32, s32)


def _retention_kernel(q_ref, k_ref, v_ref, g_ref, r0_ref, dmat_ref, xi_ref, zeta_ref, dec_ref, gain_ref,
                      o_ref, rfin_ref, r_sc):
    ci = pl.program_id(1)

    @pl.when(ci == 0)
    def _():
        r_sc[...] = r0_ref[0]

    q = q_ref[0]
    k = k_ref[0] * (RET_DK ** -0.5)
    v = v_ref[0]
    outs = []
    for hd in range(RET_HEADS):
        sl = slice(hd * RET_DK, (hd + 1) * RET_DK)
        qh = q[:, sl].astype(BF16)
        kh = k[:, sl]
        vh = v[:, sl].astype(BF16)
        s = lax.dot_general(qh, kh.astype(BF16), (((1,), (1,)), ((), ())), preferred_element_type=F32)
        inner = jnp.dot((s * dmat_ref[hd]).astype(BF16), vh, preferred_element_type=F32)
        r = r_sc[hd]
        cross = jnp.dot(qh, r.astype(BF16), preferred_element_type=F32) * xi_ref[:, hd:hd + 1]
        o = inner + cross
        mu = jnp.mean(o, axis=-1, keepdims=True)
        oc = o - mu
        var = jnp.mean(oc * oc, axis=-1, keepdims=True)
        outs.append(oc * lax.rsqrt(var + EPS))
        kz = (kh * zeta_ref[:, hd:hd + 1]).astype(BF16)
        kv = lax.dot_general(kz, vh, (((0,), (0,)), ((), ())), preferred_element_type=F32)
        r_sc[hd] = dec_ref[hd] * r + kv
    o_ref[0] = jnp.concatenate(outs, axis=1) * gain_ref[...] * _silu(g_ref[0])

    @pl.when(ci == pl.num_programs(1) - 1)
    def _():
        rfin_ref[0] = r_sc[...]


def _retention_consts(c):
    log_g = jnp.log1p(-jnp.exp2(-5.0 - jnp.arange(RET_HEADS, dtype=F32)))
    i = jnp.arange(c, dtype=F32)
    diff = i[:, None] - i[None, :]
    dmat = jnp.where(diff >= 0, jnp.exp(jnp.maximum(diff, 0.0)[None] * log_g[:, None, None]), 0.0)
    xi = jnp.exp((i[:, None] + 1.0) * log_g[None, :])
    zeta = jnp.exp((c - 1.0 - i)[:, None] * log_g[None, :])
    dec = jnp.broadcast_to(jnp.exp(c * log_g)[:, None, None], (RET_HEADS, 1, RET_DV))
    return dmat, xi, zeta, dec


def _retention(rq, rk, rv, rg, r0, gain):
    b, l, _ = rq.shape
    c = RET_CHUNK if l % RET_CHUNK == 0 else l
    n = l // c
    dmat, xi, zeta, dec = _retention_consts(c)
    tok = pl.BlockSpec((1, c, RET_WIDTH), lambda i, j: (i, j, 0))
    st = pl.BlockSpec((1, RET_HEADS, RET_DK, RET_DV), lambda i, j: (i, 0, 0, 0))
    cst = lambda a: pl.BlockSpec(a.shape, lambda i, j: (0,) * a.ndim)
    return pl.pallas_call(
        _retention_kernel,
        grid=(b, n),
        in_specs=[tok, tok, tok, tok, st, cst(dmat), cst(xi), cst(zeta), cst(dec), cst(gain)],
        out_specs=[tok, st],
        out_shape=[jax.ShapeDtypeStruct((b, l, RET_WIDTH), F32),
                   jax.ShapeDtypeStruct((b, RET_HEADS, RET_DK, RET_DV), F32)],
        scratch_shapes=[pltpu.VMEM((RET_HEADS, RET_DK, RET_DV), F32)],
        compiler_params=_cparams(("arbitrary", "arbitrary")),
        name="retention",
    )(rq, rk, rv, rg, r0, dmat, xi, zeta, dec, gain)


def _rglru_kernel(x_ref, gate_ref, buf_ref, h0_ref, cw_ref, cb_ref, wai_ref, bai_ref, lam_ref, gain_ref,
                  o_ref, conv_ref, hl_ref, xin_sc, a_sc, b_sc, h_sc, hc_sc):
    ci = pl.program_id(1)
    tc = x_ref.shape[1]
    pad = SUBLANES - (CONV_W - 1)

    @pl.when(ci == 0)
    def _():
        xin_sc[pad:SUBLANES, :] = buf_ref[0]
        hc_sc[...] = h0_ref[0]

    xin_sc[SUBLANES:SUBLANES + tc, :] = x_ref[0]
    xc = cb_ref[...]
    for kk in range(CONV_W):
        xc = xc + xin_sc[pad + kk:pad + kk + tc, :] * cw_ref[kk:kk + 1, :]
    tail = xin_sc[SUBLANES + tc - (CONV_W - 1):SUBLANES + tc, :]
    xin_sc[pad:SUBLANES, :] = tail

    z = jnp.dot(xc.astype(BF16), wai_ref[...], preferred_element_type=F32) + bai_ref[...]
    r = jax.nn.sigmoid(z[:, :LRU_WIDTH])
    ig = jax.nn.sigmoid(z[:, LRU_WIDTH:])
    log_a = -LRU_C * r * jax.nn.softplus(-lam_ref[...])
    a = jnp.exp(log_a)
    a_sc[...] = a
    b_sc[...] = jnp.sqrt(-jnp.tanh(log_a) * (a * a + 1.0)) * ig * xc

    def body(t, h):
        h = a_sc[pl.ds(t, 1), :] * h + b_sc[pl.ds(t, 1), :]
        h_sc[pl.ds(t, 1), :] = h
        return h

    h_last = lax.fori_loop(0, tc, body, hc_sc[...], unroll=8)
    hc_sc[...] = h_last
    y = h_sc[...] * jax.nn.gelu(gate_ref[0])
    o_ref[0] = _rms(y, gain_ref[...])

    @pl.when(ci == pl.num_programs(1) - 1)
    def _():
        conv_ref[0] = tail
        hl_ref[0] = h_last


def _rglru(lx, lg, buf0, h0, conv_w, conv_b, wai, bai, lam, gain, tc):
    b, l, w = lx.shape
    tok = pl.BlockSpec((1, tc, w), lambda i, j: (i, j, 0))
    cst = lambda a: pl.BlockSpec(a.shape, lambda i, j: (0,) * a.ndim)
    bufs = pl.BlockSpec((1, CONV_W - 1, w), lambda i, j: (i, 0, 0))
    hs = pl.BlockSpec((1, 1, w), lambda i, j: (i, 0, 0))
    return pl.pallas_call(
        _rglru_kernel,
        grid=(b, l // tc),
        in_specs=[tok, tok, bufs, hs, cst(conv_w), cst(conv_b), cst(wai), cst(bai), cst(lam), cst(gain)],
        out_specs=[tok, bufs, hs],
        out_shape=[jax.ShapeDtypeStruct((b, l, w), F32),
                   jax.ShapeDtypeStruct((b, CONV_W - 1, w), F32),
                   jax.ShapeDtypeStruct((b, 1, w), F32)],
        scratch_shapes=[pltpu.VMEM((SUBLANES + tc, w), F32), pltpu.VMEM((tc, w), F32), pltpu.VMEM((tc, w), F32),
                        pltpu.VMEM((tc, w), F32), pltpu.VMEM((1, w), F32)],
        compiler_params=_cparams(("arbitrary", "arbitrary")),
        name="rglru",
    )(lx, lg, buf0, h0, conv_w, conv_b, wai, bai, lam, gain)


def _softmax_update(s, v, m_sc, l_sc, acc_sc, rows=slice(None)):
    n = s.shape[1]
    m_prev = m_sc[rows, :]
    m_new = jnp.maximum(m_prev, jnp.max(s, axis=-1, keepdims=True))
    corr = jnp.exp(m_prev - m_new)
    p = jnp.exp(s - jnp.concatenate([m_new] * (n // LANES), axis=1))
    l_sc[rows, :] = l_sc[rows, :] * corr + jnp.sum(p, axis=-1, keepdims=True)
    acc_sc[rows, :] = acc_sc[rows, :] * corr + jnp.dot(p.astype(BF16), v, preferred_element_type=F32)
    m_sc[rows, :] = m_new


def _attn_finish(o, wuv_ref, gain_ref, rows_per_head):
    o = o.astype(BF16)
    om = None
    for hd in range(MLA_HEADS):
        part = jnp.dot(o[hd * rows_per_head:(hd + 1) * rows_per_head], wuv_ref[hd], preferred_element_type=F32)
        om = part if om is None else om + part
    return _rms(om, gain_ref[...])


def _flash_kernel(q_ref, k_ref, wuv_ref, gain_ref, o_ref, m_sc, l_sc, acc_sc, *, qb, kb):
    qi = pl.program_id(1)
    r = MLA_HEADS * qb
    m_sc[...] = jnp.full((r, LANES), -jnp.inf, F32)
    l_sc[...] = jnp.zeros((r, LANES), F32)
    acc_sc[...] = jnp.zeros((r, LANES), F32)
    nfull = (qi * qb) // kb
    halves = [slice(i * (r // 2), (i + 1) * (r // 2)) for i in range(2)]
    q = q_ref[0].reshape(r, QCAT)
    qs = [q[h] for h in halves]

    def step(j, masked):
        start = pl.multiple_of(j * kb, kb)
        kblk = k_ref[0, pl.ds(start, kb), :]
        v = kblk[:, :KV_RANK]
        for h, qh in zip(halves, qs):
            s = lax.dot_general(qh, kblk, (((1,), (1,)), ((), ())), preferred_element_type=F32)
            if masked:
                row = lax.broadcasted_iota(jnp.int32, s.shape, 0)
                col = lax.broadcasted_iota(jnp.int32, s.shape, 1)
                s = jnp.where(j * kb + col <= qi * qb + row % qb, s, -jnp.inf)
            _softmax_update(s, v, m_sc, l_sc, acc_sc, h)

    def body(j, carry):
        step(j, False)
        return carry

    lax.fori_loop(0, nfull, body, 0)
    step(nfull, True)
    o_ref[0] = _attn_finish(acc_sc[...] / l_sc[...], wuv_ref, gain_ref, qb)


def _mla_prompt(qcat, kcat, wuv_exp, gain, qb, kb):
    b, _, s, _ = qcat.shape
    r = MLA_HEADS * qb
    return pl.pallas_call(
        functools.partial(_flash_kernel, qb=qb, kb=kb),
        grid=(b, s // qb),
        in_specs=[pl.BlockSpec((1, MLA_HEADS, qb, QCAT), lambda i, j: (i, 0, j, 0)),
                  pl.BlockSpec((1, s, QCAT), lambda i, j: (i, 0, 0)),
                  pl.BlockSpec(wuv_exp.shape, lambda i, j: (0, 0, 0)),
                  pl.BlockSpec(gain.shape, lambda i, j: (0, 0))],
        out_specs=pl.BlockSpec((1, qb, MLA_WIDTH), lambda i, j: (i, j, 0)),
        out_shape=jax.ShapeDtypeStruct((b, s, MLA_WIDTH), F32),
        scratch_shapes=[pltpu.VMEM((r, LANES), F32)] * 3,
        compiler_params=_cparams(("arbitrary", "arbitrary")),
        name="mla_prompt",
    )(qcat, kcat, wuv_exp, gain)


def _mla_sample_kernel(pt_ref, q_ref, knew_ref, ck_hbm, kp_hbm, wuv_ref, gain_ref, o_ref, ckbuf, kpbuf, sems,
                       *, li, n_pages, page, t_new, chunk):
    bi = pl.program_id(0)
    nb = pl.num_programs(0)
    r = MLA_HEADS * t_new

    def page_copies(pg_of, slot):
        out = []
        for j in range(n_pages):
            pg = pg_of(j)
            out.append(pltpu.make_async_copy(ck_hbm.at[li, pg], ckbuf.at[slot, pl.ds(j * page, page), :],
                                             sems.at[0, slot]))
            out.append(pltpu.make_async_copy(kp_hbm.at[li, pg], kpbuf.at[slot, :, pl.ds(j * page, page)],
                                             sems.at[1, slot]))
        return out

    @pl.when(bi == 0)
    def _():
        for c in page_copies(lambda j: pt_ref[0, j], 0):
            c.start()

    @pl.when(bi + 1 < nb)
    def _():
        for c in page_copies(lambda j: pt_ref[bi + 1, j], (bi + 1) % 2):
            c.start()

    slot = bi % 2
    q = q_ref[0].reshape(r, QCAT)
    qb = q.astype(BF16)
    q_lat = qb[:, :KV_RANK]
    q_rot = q[:, KV_RANK:].astype(F32)
    q_r = q_rot[:, :MLA_ROPE]
    for i in range(1, LANES // MLA_ROPE):
        q_r = q_r + q_rot[:, i * MLA_ROPE:(i + 1) * MLA_ROPE]
    q_r = q_r.astype(BF16)

    kn = knew_ref[0].astype(BF16)
    s = lax.dot_general(qb, kn, (((1,), (1,)), ((), ())), preferred_element_type=F32)
    row = lax.broadcasted_iota(jnp.int32, s.shape, 0)
    col = lax.broadcasted_iota(jnp.int32, s.shape, 1)
    s = jnp.where(row % t_new >= col, s, -jnp.inf)
    m = jnp.max(s, axis=-1, keepdims=True)
    p = jnp.exp(s - m)
    l = jnp.sum(p, axis=-1, keepdims=True)
    acc = jnp.dot(p.astype(BF16), kn[:, :KV_RANK], preferred_element_type=F32)

    for c in page_copies(lambda j: 0, slot):
        c.wait()

    for c in range(n_pages * page // chunk):
        ck = ckbuf[slot, pl.ds(c * chunk, chunk), :].astype(BF16)
        kp = kpbuf[slot, :, pl.ds(c * chunk, chunk)].astype(BF16)
        s = (lax.dot_general(q_lat, ck, (((1,), (1,)), ((), ())), preferred_element_type=F32)
             + jnp.dot(q_r, kp, preferred_element_type=F32))
        m_new = jnp.maximum(m, jnp.max(s, axis=-1, keepdims=True))
        corr = jnp.exp(m - m_new)
        p = jnp.exp(s - m_new)
        l = l * corr + jnp.sum(p, axis=-1, keepdims=True)
        acc = acc * corr + jnp.dot(p.astype(BF16), ck, preferred_element_type=F32)
        m = m_new
    o_ref[0] = _attn_finish(acc / l, wuv_ref, gain_ref, t_new)


def _mla_sample(qcat, kcat, cache_ckv, cache_kr_t, page_table, li, wuv_exp, gain):
    b, _, t, _ = qcat.shape
    n_pages = page_table.shape[1]
    page, rank = cache_ckv.shape[2], cache_ckv.shape[3]
    rope_w = cache_kr_t.shape[2]
    chunk = min(2048, n_pages * page)
    grid_spec = pltpu.PrefetchScalarGridSpec(
        num_scalar_prefetch=1,
        grid=(b,),
        in_specs=[pl.BlockSpec((1, MLA_HEADS, t, QCAT), lambda bi, pt: (bi, 0, 0, 0)),
                  pl.BlockSpec((1, t, QCAT), lambda bi, pt: (bi, 0, 0)),
                  pl.BlockSpec(memory_space=pl.ANY),
                  pl.BlockSpec(memory_space=pl.ANY),
                  pl.BlockSpec(wuv_exp.shape, lambda bi, pt: (0, 0, 0)),
                  pl.BlockSpec(gain.shape, lambda bi, pt: (0, 0))],
        out_specs=pl.BlockSpec((1, t, MLA_WIDTH), lambda bi, pt: (bi, 0, 0)),
        scratch_shapes=[pltpu.VMEM((2, n_pages * page, rank), F32),
                        pltpu.VMEM((2, rope_w, n_pages * page), F32),
                        pltpu.SemaphoreType.DMA((2, 2))],
    )
    return pl.pallas_call(
        functools.partial(_mla_sample_kernel, li=li, n_pages=n_pages, page=page, t_new=t, chunk=chunk),
        grid_spec=grid_spec,
        out_shape=jax.ShapeDtypeStruct((b, t, MLA_WIDTH), F32),
        compiler_params=_cparams(("arbitrary",)),
        name="mla_sample",
    )(page_table, qcat, kcat, cache_ckv, cache_kr_t, wuv_exp, gain)


def _outproj_kernel(ret_ref, lru_ref, mla_ref, x_ref, w_ref, g1_ref, gt1_ref, g2_ref, sc2_ref, sh2_ref,
                    x1_ref, h2_ref):
    bt, lt, d = x_ref.shape
    rows = bt * lt
    mix = jnp.concatenate([ret_ref[...].reshape(rows, RET_WIDTH), lru_ref[...].reshape(rows, LRU_WIDTH),
                           mla_ref[...].reshape(rows, MLA_WIDTH)], axis=1).astype(BF16)
    y = jnp.dot(mix, w_ref[...], preferred_element_type=F32).reshape(bt, lt, d)
    x1 = x_ref[...] + gt1_ref[...] * _rms(y, g1_ref[...])
    x1_ref[...] = x1
    h2_ref[...] = _rms(x1, g2_ref[...]) * (1.0 + sc2_ref[...]) + sh2_ref[...]


def _outproj(ret, lru, mla, x, w_out, g1, gt1, g2, sc2, sh2, bt, lt):
    b, l, d = x.shape
    tok = lambda w: pl.BlockSpec((bt, lt, w), lambda i, j: (i, j, 0))
    mod = pl.BlockSpec((bt, 1, d), lambda i, j: (i, 0, 0))
    gsp = pl.BlockSpec((1, 1, d), lambda i, j: (0, 0, 0))
    return pl.pallas_call(
        _outproj_kernel,
        grid=(b // bt, l // lt),
        in_specs=[tok(RET_WIDTH), tok(LRU_WIDTH), tok(MLA_WIDTH), tok(d),
                  pl.BlockSpec(w_out.shape, lambda i, j: (0, 0)), gsp, mod, gsp, mod, mod],
        out_specs=[tok(d), tok(d)],
        out_shape=[jax.ShapeDtypeStruct((b, l, d), F32)] * 2,
        compiler_params=_cparams(("arbitrary", "arbitrary")),
        name="outproj",
    )(ret, lru, mla, x, w_out, g1, gt1, g2, sc2, sh2)


def _split_dot_t(w, x):
    dn = (((1,), (1,)), ((), ()))
    wh = w.astype(BF16)
    wl = (w - wh.astype(F32)).astype(BF16)
    xh = x.astype(BF16)
    xl = (x - xh.astype(F32)).astype(BF16)
    return (lax.dot_general(wh, xh, dn, preferred_element_type=F32)
            + lax.dot_general(wh, xl, dn, preferred_element_type=F32)
            + lax.dot_general(wl, xh, dn, preferred_element_type=F32))


def _router_kernel(x_ref, w_ref, b_ref, upper_ref, lower_ref, eid_ref, pos_ref, gate_ref, cnt_ref, cnt_sc):
    ti = pl.program_id(0)
    tm = x_ref.shape[0]

    @pl.when(ti == 0)
    def _():
        cnt_sc[...] = jnp.zeros(cnt_sc.shape, F32)

    logits = _split_dot_t(w_ref[...], x_ref[...])
    scores = jax.nn.sigmoid(logits)
    biased = scores + b_ref[...]
    groups = [biased[g * GROUP_SIZE:(g + 1) * GROUP_SIZE, :] for g in range(N_GROUPS)]
    sub = lax.broadcasted_iota(jnp.int32, (GROUP_SIZE, tm), 0)
    gs = []
    for bg in groups:
        m1 = jnp.max(bg, axis=0, keepdims=True)
        i1 = jnp.min(jnp.where(bg == m1, sub, GROUP_SIZE), axis=0, keepdims=True)
        m2 = jnp.max(jnp.where(sub == i1, -jnp.inf, bg), axis=0, keepdims=True)
        gs.append(m1 + m2)
    masked = []
    for g in range(N_GROUPS):
        rank = jnp.zeros((1, tm), F32)
        for g2 in range(N_GROUPS):
            if g2 != g:
                ahead = (gs[g2] >= gs[g]) if g2 < g else (gs[g2] > gs[g])
                rank = rank + jnp.where(ahead, 1.0, 0.0)
        masked.append(jnp.where(rank < TOPK_GROUPS, groups[g], -jnp.inf))
    sels = []
    for g in range(N_GROUPS):
        mine = masked[g]
        rank = jnp.zeros((GROUP_SIZE, tm), F32)
        for g2 in range(N_GROUPS):
            for s2 in range(GROUP_SIZE):
                other = jnp.broadcast_to(masked[g2][s2:s2 + 1, :], (GROUP_SIZE, tm))
                if g2 < g:
                    rank = rank + jnp.where(other >= mine, 1.0, 0.0)
                elif g2 > g:
                    rank = rank + jnp.where(other > mine, 1.0, 0.0)
                else:
                    tie = jnp.where(sub > s2, 1.0, 0.0)
                    rank = rank + jnp.where(other > mine, 1.0, jnp.where(other == mine, tie, 0.0))
        sels.append(jnp.where(rank < TOP_K, 1.0, 0.0))
    self32 = jnp.concatenate(sels, axis=0)
    sel = self32 > 0.5
    picked = jnp.where(sel, scores, 0.0)
    gate = picked / jnp.sum(picked, axis=0, keepdims=True) * ROUTED_SCALE
    selb = self32.astype(BF16)
    pos = jnp.dot(selb, upper_ref[...], preferred_element_type=F32) + cnt_sc[:, :1]
    cnt_sc[...] = cnt_sc[...] + jnp.sum(self32, axis=1, keepdims=True)
    slot = jnp.dot(lower_ref[...], selb, preferred_element_type=F32)
    efl = lax.broadcasted_iota(jnp.int32, (N_EXPERTS, tm), 0).astype(F32)
    for kk in range(TOP_K):
        hit = sel & (slot == float(kk))
        eid_ref[kk:kk + 1, :] = jnp.sum(jnp.where(hit, efl, 0.0), axis=0, keepdims=True).astype(jnp.int32)
        pos_ref[kk:kk + 1, :] = jnp.sum(jnp.where(hit, pos, 0.0), axis=0, keepdims=True).astype(jnp.int32)
        gate_ref[kk:kk + 1, :] = jnp.sum(jnp.where(hit, gate, 0.0), axis=0, keepdims=True)

    @pl.when(ti == pl.num_programs(0) - 1)
    def _():
        cnt_ref[...] = cnt_sc[...]


def _router(h2, w_rt, b_r, tm):
    t, d = h2.shape
    upper = jnp.triu(jnp.ones((tm, tm), F32), 1).astype(BF16)
    lower = jnp.tril(jnp.ones((N_EXPERTS, N_EXPERTS), F32), -1).astype(BF16)
    kt = pl.BlockSpec((TOP_K, tm), lambda i: (0, i))
    cst = lambda a: pl.BlockSpec(a.shape, lambda i: (0,) * a.ndim)
    return pl.pallas_call(
        _router_kernel,
        grid=(t // tm,),
        in_specs=[pl.BlockSpec((tm, d), lambda i: (i, 0)), cst(w_rt), cst(b_r), cst(upper), cst(lower)],
        out_specs=[kt, kt, kt, pl.BlockSpec((N_EXPERTS, LANES), lambda i: (0, 0))],
        out_shape=[jax.ShapeDtypeStruct((TOP_K, t), jnp.int32), jax.ShapeDtypeStruct((TOP_K, t), jnp.int32),
                   jax.ShapeDtypeStruct((TOP_K, t), F32), jax.ShapeDtypeStruct((N_EXPERTS, LANES), F32)],
        scratch_shapes=[pltpu.VMEM((N_EXPERTS, LANES), F32)],
        compiler_params=_cparams(("arbitrary",)),
        name="router",
    )(h2, w_rt, b_r, upper, lower)


def _dispatch_kernel(dest_ref, x_ref, rows_in_ref, rows_ref, sem):
    del rows_in_ref
    tm = x_ref.shape[0]

    def row_copy(t, d):
        return pltpu.make_async_copy(x_ref.at[pl.ds(t, 1), :], rows_ref.at[pl.ds(d, 1), :], sem)

    def issue(t, c):
        for kk in range(TOP_K):
            row_copy(t, dest_ref[t * TOP_K + kk]).start()
        return c

    def drain(t, c):
        for kk in range(TOP_K):
            row_copy(0, 0).wait()
        return c

    lax.fori_loop(0, tm, issue, 0)
    lax.fori_loop(0, tm, drain, 0)


def _dispatch(dest_flat, h2, n_rows, tm):
    t, d = h2.shape
    rows0 = jnp.zeros((n_rows, d), F32)
    return pl.pallas_call(
        _dispatch_kernel,
        grid=(t // tm,),
        in_specs=[pl.BlockSpec((tm * TOP_K,), lambda i: (i,), memory_space=pltpu.SMEM),
                  pl.BlockSpec((tm, d), lambda i: (i, 0)),
                  pl.BlockSpec(memory_space=pl.ANY)],
        out_specs=pl.BlockSpec(memory_space=pl.ANY),
        out_shape=jax.ShapeDtypeStruct((n_rows, d), F32),
        scratch_shapes=[pltpu.SemaphoreType.DMA(())],
        input_output_aliases={2: 0},
        compiler_params=_cparams(("arbitrary",)),
        name="moe_dispatch",
    )(dest_flat, h2, rows0)


def _expert_kernel(be_ref, nu_ref, rows_ref, wgu_ref, wd_ref, o_ref, wgu_sc, wd_sc):
    i = pl.program_id(0)
    prev = be_ref[jnp.maximum(i - 1, 0)]
    fresh = (i == 0) | (be_ref[i] != prev)

    @pl.when(fresh)
    def _():
        wgu_sc[...] = wgu_ref[...].astype(BF16)
        wd_sc[...] = wd_ref[...].astype(BF16)

    @pl.when(i < nu_ref[0])
    def _():
        gu = jnp.dot(rows_ref[...].astype(BF16), wgu_sc[...], preferred_element_type=F32)
        act = _silu(gu[:, :D_EXPERT]) * gu[:, D_EXPERT:]
        o_ref[...] = jnp.dot(act.astype(BF16), wd_sc[...], preferred_element_type=F32)


def _experts(block_e, n_used, rows, w_gu, w_down, li, blk):
    n_rows, d = rows.shape
    n_blocks = n_rows // blk
    de2 = w_gu.shape[-1]
    de = w_down.shape[-2]

    def rmap(i, be, nu):
        return (jnp.minimum(i, nu[0] - 1), 0)

    grid_spec = pltpu.PrefetchScalarGridSpec(
        num_scalar_prefetch=2,
        grid=(n_blocks,),
        in_specs=[pl.BlockSpec((blk, d), rmap),
                  pl.BlockSpec((None, None, d, de2), lambda i, be, nu: (li, be[i], 0, 0)),
                  pl.BlockSpec((None, None, de, d), lambda i, be, nu: (li, be[i], 0, 0))],
        out_specs=pl.BlockSpec((blk, d), rmap),
        scratch_shapes=[pltpu.VMEM((d, de2), BF16), pltpu.VMEM((de, d), BF16)],
    )
    return pl.pallas_call(
        _expert_kernel,
        grid_spec=grid_spec,
        out_shape=jax.ShapeDtypeStruct((n_rows, d), F32),
        compiler_params=_cparams(("arbitrary",)),
        name="moe_experts",
    )(block_e, n_used, rows, w_gu, w_down)


def _combine_kernel(dest_ref, dnext_ref, gate_ref, orow_ref, h2_ref, x1_ref, wsgu_ref, wsd_ref, g3_ref, gt2_ref,
                    y_ref, buf, sems):
    bt, lt, d = x1_ref.shape
    tm = bt * lt
    step = pl.program_id(0) * pl.num_programs(1) + pl.program_id(1)
    nsteps = pl.num_programs(0) * pl.num_programs(1)
    slot = step % 2

    def row_copy(src_row, t, kk, sl):
        return pltpu.make_async_copy(orow_ref.at[pl.ds(src_row, 1), :], buf.at[sl, kk, pl.ds(t, 1), :], sems.at[sl])

    def issue_tile(dref, sl):
        def issue(t, c):
            for kk in range(TOP_K):
                row_copy(dref[t * TOP_K + kk], t, kk, sl).start()
            return c
        lax.fori_loop(0, tm, issue, 0)

    @pl.when(step == 0)
    def _():
        issue_tile(dest_ref, 0)

    @pl.when(step + 1 < nsteps)
    def _():
        issue_tile(dnext_ref, 1 - slot)

    hb = h2_ref[...].astype(BF16)
    gu = jnp.dot(hb, wsgu_ref[...], preferred_element_type=F32)
    ds = wsd_ref.shape[0]
    f = jnp.dot((_silu(gu[:, :ds]) * gu[:, ds:]).astype(BF16), wsd_ref[...], preferred_element_type=F32)

    def drain(t, c):
        for kk in range(TOP_K):
            row_copy(0, 0, 0, slot).wait()
        return c

    lax.fori_loop(0, tm, drain, 0)
    gate = gate_ref[...]
    for kk in range(TOP_K):
        f = f + buf[slot, kk] * gate[:, kk:kk + 1]
    y_ref[...] = x1_ref[...] + gt2_ref[...] * _rms(f, g3_ref[...]).reshape(bt, lt, d)


def _combine(dest_flat, gate_tk, out_rows, h2, x1, ws_gu, ws_down, g3, gt2, bt, lt):
    b, l, d = x1.shape
    tm = bt * lt
    nl = l // lt
    nsteps = (b // bt) * nl
    tok = pl.BlockSpec((bt, lt, d), lambda i, j: (i, j, 0))
    flat = lambda w: pl.BlockSpec((tm, w), lambda i, j: (i * nl + j, 0))
    return pl.pallas_call(
        _combine_kernel,
        grid=(b // bt, nl),
        in_specs=[pl.BlockSpec((tm * TOP_K,), lambda i, j: (i * nl + j,), memory_space=pltpu.SMEM),
                  pl.BlockSpec((tm * TOP_K,), lambda i, j: (jnp.minimum(i * nl + j + 1, nsteps - 1),),
                               memory_space=pltpu.SMEM),
                  flat(TOP_K), pl.BlockSpec(memory_space=pl.ANY), flat(d), tok,
                  pl.BlockSpec(ws_gu.shape, lambda i, j: (0, 0)), pl.BlockSpec(ws_down.shape, lambda i, j: (0, 0)),
                  pl.BlockSpec((1, 1, d), lambda i, j: (0, 0, 0)),
                  pl.BlockSpec((bt, 1, d), lambda i, j: (i, 0, 0))],
        out_specs=tok,
        out_shape=jax.ShapeDtypeStruct((b, l, d), F32),
        scratch_shapes=[pltpu.VMEM((2, TOP_K, tm, d), F32), pltpu.SemaphoreType.DMA((2,))],
        compiler_params=_cparams(("arbitrary", "arbitrary")),
        name="moe_combine",
    )(dest_flat, dest_flat, gate_tk, out_rows, h2, x1, ws_gu, ws_down, g3, gt2)


def _moe(h2, x1, gt2, g3, w_rt, b_r, w_exp_gu, w_exp_down, ws_gu, ws_down, li, bt, lt, blk):
    b, l, d = x1.shape
    t = b * l
    h2f = h2.reshape(t, d)
    tm_r = min(512, t)
    eid, pos, gate, cnt = _router(h2f, w_rt, b_r, tm_r)
    counts = cnt[:, 0].astype(jnp.int32)
    padded = (counts + blk - 1) // blk * blk
    pends = jnp.cumsum(padded)
    pstart = pends - padded
    n_blocks = -(-(t * TOP_K) // blk) + N_EXPERTS
    starts = jnp.arange(n_blocks, dtype=jnp.int32) * blk
    block_e = jnp.minimum(jnp.sum((pends[None, :] <= starts[:, None]).astype(jnp.int32), axis=1), N_EXPERTS - 1)
    n_used = (pends[-1:] // blk).astype(jnp.int32)
    first_row = jnp.sum(jnp.where(eid[:, :, None] == jnp.arange(N_EXPERTS, dtype=jnp.int32), pstart, 0), axis=-1)
    dest = (first_row + pos).T.reshape(t * TOP_K)
    gate_tk = gate.T
    tm_d = min(512, t)
    rows = _dispatch(dest, h2f, n_blocks * blk, tm_d)
    out_rows = _experts(block_e, n_used, rows, w_exp_gu, w_exp_down, li, blk)
    return _combine(dest, gate_tk, out_rows, h2f, x1, ws_gu, ws_down, g3, gt2, bt, lt)


def _proj_perm():
    base = np.arange(C_QN)
    mq0 = C_QN
    qn = np.concatenate([mq0 + h * (MLA_NOPE + MLA_ROPE) + np.arange(MLA_NOPE) for h in range(MLA_HEADS)])
    qr = np.concatenate([mq0 + h * (MLA_NOPE + MLA_ROPE) + MLA_NOPE + np.arange(MLA_ROPE) for h in range(MLA_HEADS)])
    mc0 = mq0 + MLA_HEADS * (MLA_NOPE + MLA_ROPE)
    mc = mc0 + np.arange(KV_RANK)
    mk = np.tile(mc0 + KV_RANK + np.arange(MLA_ROPE), LANES // MLA_ROPE)
    return np.concatenate([base, qn, qr, mc, mk])


def _rope_tables(pos, dim):
    half = dim // 2
    inv = ROPE_BASE ** (-jnp.arange(0, dim, 2, dtype=F32) / dim)
    ang = pos.astype(F32)[:, None] * inv[None, :]
    cos, sin = jnp.cos(ang), jnp.sin(ang)
    reps = LANES // dim
    cos_t = jnp.tile(jnp.concatenate([cos, cos], axis=1), (1, reps))
    sin_t = jnp.tile(jnp.concatenate([-sin, sin], axis=1), (1, reps))
    return cos_t, sin_t


def _block_diag(blocks):
    n, a, b = blocks.shape
    eye = jnp.eye(n, dtype=blocks.dtype)
    return (eye[:, None, :, None] * blocks[:, :, None, :]).reshape(n * a, n * b)


def _layer(x, mod, pos_tabs, li, p, past, tiles):
    b, l, d = x.shape
    bt, lt = tiles['bt'], tiles['lt']
    sh1, sc1, gt1, sh2, sc2, gt2 = [m[:, None, :] for m in jnp.split(mod, 6, axis=-1)]
    g = p['norm_gains']
    gains = [g[i][None, None, :] for i in range(4)]
    gain = p['grp_gain']
    rq, rk, rv, rg, lx, lg, qcat, kcat, ckv, kr = _inproj(
        x, sc1, sh1, gains[0], p['w_in_p'], p['wuk_bd'], p['kv_norm'][None, :], pos_tabs, bt, lt, tiles['qdtype'])
    if past is None:
        r0 = jnp.zeros((b, RET_HEADS, RET_DK, RET_DV), F32)
        buf0 = jnp.zeros((b, CONV_W - 1, LRU_WIDTH), F32)
        h0 = jnp.zeros((b, 1, LRU_WIDTH), F32)
    else:
        r0, buf0, h0 = past['ret'], past['conv'], past['lru'][:, None, :]
    ret_out, r_new = _retention(rq, rk, rv, rg, r0, gain[None, :RET_WIDTH])
    lru_out, conv_new, h_new = _rglru(
        lx, lg, buf0, h0, p['conv_w'], p['conv_b'][None, :], p['wai'], p['bai'], p['lru_lambda'][None, :],
        gain[None, RET_WIDTH:RET_WIDTH + LRU_WIDTH], tiles['tc'])
    mla_gain = gain[None, RET_WIDTH + LRU_WIDTH:]
    if past is None:
        mla_out = _mla_prompt(qcat, kcat, p['wuv_exp'], mla_gain, tiles['qb'], tiles['kb'])
    else:
        mla_out = _mla_sample(qcat, kcat, past['cache_ckv'], past['cache_kr_t'], past['page_table'], li,
                              p['wuv_exp'], mla_gain)
    x1, h2 = _outproj(ret_out, lru_out, mla_out, x, p['w_out'], gains[1], gt1, gains[2], sc2, sh2, bt, lt)
    y = _moe(h2, x1, gt2, gains[3], p['w_rt'], p['b_router'][:, None], p['w_exp_gu'], p['w_exp_down'],
             p['ws_gu'], p['ws_down'], li, tiles['cbt'], tiles['clt'], tiles['blk'])
    return y, (ckv, kr, r_new, conv_new, h_new[:, 0])


def _largest_tile(n, cap):
    t = min(n, cap)
    while n % t:
        t //= 2
    return t


def kernel(x_prompt, x_sample, cache_ckv, cache_krope, state_ret, state_conv, state_lru, page_table, c_prompt, c_sample, w_ada, b_ada, norm_gains, w_in, w_out, grp_gain, conv_w, conv_b, w_lru_a, b_lru_a, w_lru_i, b_lru_i, lru_lambda, kv_norm, w_ukv, w_router, b_router, w_exp_gu, w_exp_down, w_sh_gu, w_sh_down):
    depth = w_in.shape[0]
    bp, lp, d = x_prompt.shape
    bs, ls, _ = x_sample.shape
    past_len = page_table.shape[1] * cache_ckv.shape[2]

    n_c = bp + bs
    n_cp = -(-n_c // SUBLANES) * SUBLANES
    c_all = jnp.concatenate([c_prompt, c_sample, jnp.zeros((n_cp - n_c, d), F32)], axis=0)
    mod_all = _adaln(c_all, w_ada, b_ada[:, None, :])

    w_in_p = w_in[:, :, _proj_perm()].astype(BF16)
    w_uk = w_ukv[..., :MLA_NOPE]
    w_uv = w_ukv[..., MLA_NOPE:]
    eye_h = jnp.eye(MLA_HEADS, dtype=F32)

    pos_p = jnp.arange(lp)
    pos_s = past_len + jnp.arange(ls)
    tabs_p = _rope_tables(pos_p, RET_DK) + _rope_tables(pos_p, MLA_ROPE)
    tabs_s = _rope_tables(pos_s, RET_DK) + _rope_tables(pos_s, MLA_ROPE)

    lt_p = _largest_tile(lp, 256)
    tiles_p = dict(bt=1, lt=lt_p, qdtype=BF16, tc=_largest_tile(lp, 256), qb=_largest_tile(lp, 128),
                   kb=_largest_tile(lp, 512), cbt=1, clt=_largest_tile(lp, 128), blk=256 if bp * lp >= 4096 else 128)
    bt_s = _largest_tile(bs, 32)
    tiles_s = dict(bt=bt_s, lt=ls, qdtype=F32, tc=ls,
                   cbt=_largest_tile(bs, 16), clt=ls, blk=128)

    cache_kr_t = jnp.swapaxes(cache_krope, 2, 3)
    y_p, y_s = x_prompt, x_sample
    st_p, st_s = [], []
    for li in range(depth):
        wuk_bd = _block_diag(jnp.transpose(w_uk[li], (1, 2, 0))).astype(BF16)
        wuv_h = jnp.transpose(w_uv[li], (1, 0, 2))
        wuv_exp = (wuv_h[:, :, None, :] * eye_h[:, None, :, None]).reshape(
            MLA_HEADS, KV_RANK, MLA_WIDTH).astype(BF16)
        wai = jnp.concatenate([_block_diag(w_lru_a[li]), _block_diag(w_lru_i[li])], axis=1).astype(BF16)
        bai = jnp.concatenate([b_lru_a[li], b_lru_i[li]])[None, :]
        p = dict(norm_gains=norm_gains[li], w_in_p=w_in_p[li], wuk_bd=wuk_bd, wuv_exp=wuv_exp,
                 w_out=w_out[li].astype(BF16), grp_gain=grp_gain[li], conv_w=conv_w[li], conv_b=conv_b[li],
                 wai=wai, bai=bai, lru_lambda=lru_lambda[li], kv_norm=kv_norm[li],
                 w_rt=w_router[li].T, b_router=b_router[li], w_exp_gu=w_exp_gu, w_exp_down=w_exp_down,
                 ws_gu=w_sh_gu[li].astype(BF16), ws_down=w_sh_down[li].astype(BF16))
        y_p, sp = _layer(y_p, mod_all[li, :bp], tabs_p, li, p, None, tiles_p)
        past = dict(ret=state_ret[li], conv=state_conv[li], lru=state_lru[li], cache_ckv=cache_ckv,
                    cache_kr_t=cache_kr_t, page_table=page_table)
        y_s, ss = _layer(y_s, mod_all[li, bp:bp + bs], tabs_s, li, p, past, tiles_s)
        st_p.append(sp)
        st_s.append(ss)
    stack = lambda sts, i: jnp.stack([s[i] for s in sts])
    return (y_p, y_s) + tuple(stack(st_p, i) for i in range(5)) + tuple(stack(st_s, i) for i in range(5))
```

```python
import functools

import numpy as np
import jax
import jax.numpy as jnp
from jax import lax
from jax.experimental import pallas as pl
from jax.experimental.pallas import tpu as pltpu

F32 = jnp.float32
BF16 = jnp.bfloat16
EPS = 1e-6
ROPE_BASE = 10000.0

RET_HEADS = 4
RET_DK = 64
RET_DV = 64
RET_WIDTH = RET_HEADS * RET_DV
RET_CHUNK = 128
LRU_WIDTH = 256
LRU_BLOCKS = 4
LRU_BLOCK = LRU_WIDTH // LRU_BLOCKS
CONV_W = 4
LRU_C = 8.0
MLA_HEADS = 8
MLA_NOPE = 64
MLA_ROPE = 32
MLA_V = 64
KV_RANK = 128
MLA_WIDTH = MLA_HEADS * MLA_V
N_EXPERTS = 64
TOP_K = 8
N_GROUPS = 8
GROUP_SIZE = N_EXPERTS // N_GROUPS
TOPK_GROUPS = 4
D_EXPERT = 256
ROUTED_SCALE = 2.5

LANES = 128
SUBLANES = 8
QCAT = 2 * LANES
C_RQ, C_RK, C_RV, C_RG, C_LX, C_LG = 0, 256, 512, 768, 1024, 1280
C_QN = 1536
C_QR = C_QN + MLA_HEADS * MLA_NOPE
C_MC = C_QR + MLA_HEADS * MLA_ROPE
C_MK = C_MC + KV_RANK
N_PROJ = C_MK + LANES


def _cparams(sem, vmem_mb=48):
    return pltpu.CompilerParams(dimension_semantics=sem, vmem_limit_bytes=vmem_mb * 2 ** 20)


def _rms(x, g):
    return x * lax.rsqrt(jnp.mean(x * x, axis=-1, keepdims=True) + EPS) * g


def _silu(x):
    return x * jax.nn.sigmoid(x)


def _adaln_kernel(c_ref, w_ref, b_ref, o_ref):
    c = c_ref[...]
    s = _silu(c).astype(BF16)
    o_ref[...] = jnp.dot(s, w_ref[...].astype(BF16), preferred_element_type=F32) + b_ref[...]


def _adaln(c, w_ada, b_ada):
    depth, d, n = w_ada.shape
    bp = c.shape[0]
    tn = 768
    return pl.pallas_call(
        _adaln_kernel,
        grid=(depth, n // tn),
        in_specs=[pl.BlockSpec((bp, d), lambda l, j: (0, 0)),
                  pl.BlockSpec((None, d, tn), lambda l, j: (l, 0, j)),
                  pl.BlockSpec((None, 1, tn), lambda l, j: (l, 0, j))],
        out_specs=pl.BlockSpec((None, bp, tn), lambda l, j: (l, 0, j)),
        out_shape=jax.ShapeDtypeStruct((depth, bp, n), F32),
        compiler_params=_cparams(("arbitrary", "arbitrary")),
        name="adaln",
    )(c, w_ada, b_ada)


def _rope_slab(v, cos, sin_signed, half):
    lane = lax.broadcasted_iota(jnp.int32, v.shape, 1)
    first = (lane % (2 * half)) < half
    partner = jnp.where(first, pltpu.roll(v, LANES - half, 1), pltpu.roll(v, half, 1))
    return v * cos + partner * sin_signed


def _rope(v, cos, sin_signed, half):
    n = v.shape[1] // LANES
    outs = [_rope_slab(v[:, i * LANES:(i + 1) * LANES], cos, sin_signed, half) for i in range(n)]
    return outs[0] if n == 1 else jnp.concatenate(outs, axis=1)


def _inproj_kernel(x_ref, sc_ref, sh_ref, g_ref, w_ref, wuk_ref, kvg_ref, c64_ref, s64_ref, c32_ref, s32_ref,
                   rq_ref, rk_ref, rv_ref, rg_ref, lx_ref, lg_ref, qcat_ref, kcat_ref, ckv_ref, kr_ref):
    bt, lt, d = x_ref.shape
    rows = bt * lt
    h = _rms(x_ref[...], g_ref[...]) * (1.0 + sc_ref[...]) + sh_ref[...]
    hb = h.reshape(rows, d).astype(BF16)

    def proj(a, b):
        return jnp.dot(hb, w_ref[:, a:b], preferred_element_type=F32)

    def table(ref):
        t = ref[...]
        if bt > 1:
            t = jnp.broadcast_to(t[None], (bt, lt, LANES)).reshape(rows, LANES)
        return t

    c64, s64, c32, s32 = table(c64_ref), table(s64_ref), table(c32_ref), table(s32_ref)
    rq_ref[...] = _rope(proj(C_RQ, C_RK), c64, s64, RET_DK // 2).reshape(bt, lt, RET_WIDTH)
    rk_ref[...] = _rope(proj(C_RK, C_RV), c64, s64, RET_DK // 2).reshape(bt, lt, RET_WIDTH)
    rv_ref[...] = proj(C_RV, C_RG).reshape(bt, lt, RET_WIDTH)
    rg_ref[...] = proj(C_RG, C_LX).reshape(bt, lt, RET_WIDTH)
    lx_ref[...] = proj(C_LX, C_LG).reshape(bt, lt, LRU_WIDTH)
    lg_ref[...] = proj(C_LG, C_QN).reshape(bt, lt, LRU_WIDTH)

    scale = (MLA_NOPE + MLA_ROPE) ** -0.5
    q_lat = jnp.dot(proj(C_QN, C_QR).astype(BF16), wuk_ref[...], preferred_element_type=F32) * scale
    q_rope = _rope(proj(C_QR, C_MC), c32, s32, MLA_ROPE // 2) * scale
    lane = lax.broadcasted_iota(jnp.int32, (rows, LANES), 1)
    heads_per_slab = LANES // MLA_ROPE
    for hd in range(MLA_HEADS):
        slab = q_rope[:, (hd // heads_per_slab) * LANES:(hd // heads_per_slab + 1) * LANES]
        own = jnp.where(lane // MLA_ROPE == hd % heads_per_slab, slab, 0.0)
        qh = jnp.concatenate([q_lat[:, hd * KV_RANK:(hd + 1) * KV_RANK], own], axis=1)
        qcat_ref[:, hd, :, :] = qh.reshape(bt, lt, QCAT).astype(qcat_ref.dtype)

    ckv = _rms(proj(C_MC, C_MK), kvg_ref[...])
    kr4 = _rope(proj(C_MK, N_PROJ), c32, s32, MLA_ROPE // 2)
    ckv_ref[...] = ckv.reshape(bt, lt, KV_RANK)
    kr_ref[...] = kr4[:, :MLA_ROPE].reshape(bt, lt, MLA_ROPE)
    kcat_ref[...] = jnp.concatenate([ckv, kr4], axis=1).reshape(bt, lt, QCAT).astype(kcat_ref.dtype)


def _inproj(x, sc, sh, g, w_in_p, wuk_bd, kv_gain, tabs, bt, lt, qdtype):
    b, l, d = x.shape
    grid = (b // bt, l // lt)
    c64, s64, c32, s32 = tabs
    tok = lambda w: pl.BlockSpec((bt, lt, w), lambda i, j: (i, j, 0))
    mod = pl.BlockSpec((bt, 1, d), lambda i, j: (i, 0, 0))
    const2 = lambda a: pl.BlockSpec(a.shape, lambda i, j: (0, 0))
    tab = pl.BlockSpec((lt, LANES), lambda i, j: (j, 0))
    out_shape = [jax.ShapeDtypeStruct((b, l, 256), F32)] * 6 + [
        jax.ShapeDtypeStruct((b, MLA_HEADS, l, QCAT), qdtype),
        jax.ShapeDtypeStruct((b, l, QCAT), qdtype),
        jax.ShapeDtypeStruct((b, l, KV_RANK), F32),
        jax.ShapeDtypeStruct((b, l, MLA_ROPE), F32)]
    out_specs = [tok(256)] * 6 + [
        pl.BlockSpec((bt, MLA_HEADS, lt, QCAT), lambda i, j: (i, 0, j, 0)),
        tok(QCAT), tok(KV_RANK), tok(MLA_ROPE)]
    return pl.pallas_call(
        _inproj_kernel,
        grid=grid,
        in_specs=[tok(d), mod, mod, pl.BlockSpec((1, 1, d), lambda i, j: (0, 0, 0)),
                  const2(w_in_p), const2(wuk_bd), const2(kv_gain), tab, tab, tab, tab],
        out_specs=out_specs,
        out_shape=out_shape,
        compiler_params=_cparams(("arbitrary", "arbitrary")),
        name="inproj",
    )(x, sc, sh, g, w_in_p, wuk_bd, kv_gain, c64, s64, c32, s32)


def _retention_kernel(q_ref, k_ref, v_ref, g_ref, r0_ref, dmat_ref, xi_ref, zeta_ref, dec_ref, gain_ref,
                      o_ref, rfin_ref, r_sc):
    ci = pl.program_id(1)

    @pl.when(ci == 0)
    def _():
        r_sc[...] = r0_ref[0]

    q = q_ref[0]
    k = k_ref[0] * (RET_DK ** -0.5)
    v = v_ref[0]
    outs = []
    for hd in range(RET_HEADS):
        sl = slice(hd * RET_DK, (hd + 1) * RET_DK)
        qh = q[:, sl].astype(BF16)
        kh = k[:, sl]
        vh = v[:, sl].astype(BF16)
        s = lax.dot_general(qh, kh.astype(BF16), (((1,), (1,)), ((), ())), preferred_element_type=F32)
        inner = jnp.dot((s * dmat_ref[hd]).astype(BF16), vh, preferred_element_type=F32)
        r = r_sc[hd]
        cross = jnp.dot(qh, r.astype(BF16), preferred_element_type=F32) * xi_ref[:, hd:hd + 1]
        o = inner + cross
        mu = jnp.mean(o, axis=-1, keepdims=True)
        oc = o - mu
        var = jnp.mean(oc * oc, axis=-1, keepdims=True)
        outs.append(oc * lax.rsqrt(var + EPS))
        kz = (kh * zeta_ref[:, hd:hd + 1]).astype(BF16)
        kv = lax.dot_general(kz, vh, (((0,), (0,)), ((), ())), preferred_element_type=F32)
        r_sc[hd] = dec_ref[hd] * r + kv
    o_ref[0] = jnp.concatenate(outs, axis=1) * gain_ref[...] * _silu(g_ref[0])

    @pl.when(ci == pl.num_programs(1) - 1)
    def _():
        rfin_ref[0] = r_sc[...]


def _retention_consts(c):
    log_g = jnp.log1p(-jnp.exp2(-5.0 - jnp.arange(RET_HEADS, dtype=F32)))
    i = jnp.arange(c, dtype=F32)
    diff = i[:, None] - i[None, :]
    dmat = jnp.where(diff >= 0, jnp.exp(jnp.maximum(diff, 0.0)[None] * log_g[:, None, None]), 0.0)
    xi = jnp.exp((i[:, None] + 1.0) * log_g[None, :])
    zeta = jnp.exp((c - 1.0 - i)[:, None] * log_g[None, :])
    dec = jnp.broadcast_to(jnp.exp(c * log_g)[:, None, None], (RET_HEADS, 1, RET_DV))
    return dmat, xi, zeta, dec


def _retention(rq, rk, rv, rg, r0, gain):
    b, l, _ = rq.shape
    c = RET_CHUNK if l % RET_CHUNK == 0 else l
    n = l // c
    dmat, xi, zeta, dec = _retention_consts(c)
    tok = pl.BlockSpec((1, c, RET_WIDTH), lambda i, j: (i, j, 0))
    st = pl.BlockSpec((1, RET_HEADS, RET_DK, RET_DV), lambda i, j: (i, 0, 0, 0))
    cst = lambda a: pl.BlockSpec(a.shape, lambda i, j: (0,) * a.ndim)
    return pl.pallas_call(
        _retention_kernel,
        grid=(b, n),
        in_specs=[tok, tok, tok, tok, st, cst(dmat), cst(xi), cst(zeta), cst(dec), cst(gain)],
        out_specs=[tok, st],
        out_shape=[jax.ShapeDtypeStruct((b, l, RET_WIDTH), F32),
                   jax.ShapeDtypeStruct((b, RET_HEADS, RET_DK, RET_DV), F32)],
        scratch_shapes=[pltpu.VMEM((RET_HEADS, RET_DK, RET_DV), F32)],
        compiler_params=_cparams(("arbitrary", "arbitrary")),
        name="retention",
    )(rq, rk, rv, rg, r0, dmat, xi, zeta, dec, gain)


def _rglru_kernel(x_ref, gate_ref, buf_ref, h0_ref, cw_ref, cb_ref, wai_ref, bai_ref, lam_ref, gain_ref,
                  o_ref, conv_ref, hl_ref, xin_sc, a_sc, b_sc, h_sc, hc_sc):
    ci = pl.program_id(1)
    tc = x_ref.shape[1]
    pad = SUBLANES - (CONV_W - 1)

    @pl.when(ci == 0)
    def _():
        xin_sc[pad:SUBLANES, :] = buf_ref[0]
        hc_sc[...] = h0_ref[0]

    xin_sc[SUBLANES:SUBLANES + tc, :] = x_ref[0]
    xc = cb_ref[...]
    for kk in range(CONV_W):
        xc = xc + xin_sc[pad + kk:pad + kk + tc, :] * cw_ref[kk:kk + 1, :]
    tail = xin_sc[SUBLANES + tc - (CONV_W - 1):SUBLANES + tc, :]
    xin_sc[pad:SUBLANES, :] = tail

    z = jnp.dot(xc.astype(BF16), wai_ref[...], preferred_element_type=F32) + bai_ref[...]
    r = jax.nn.sigmoid(z[:, :LRU_WIDTH])
    ig = jax.nn.sigmoid(z[:, LRU_WIDTH:])
    log_a = -LRU_C * r * jax.nn.softplus(-lam_ref[...])
    a = jnp.exp(log_a)
    a_sc[...] = a
    b_sc[...] = jnp.sqrt(-jnp.tanh(log_a) * (a * a + 1.0)) * ig * xc

    def body(t, h):
        h = a_sc[pl.ds(t, 1), :] * h + b_sc[pl.ds(t, 1), :]
        h_sc[pl.ds(t, 1), :] = h
        return h

    h_last = lax.fori_loop(0, tc, body, hc_sc[...], unroll=8)
    hc_sc[...] = h_last
    y = h_sc[...] * jax.nn.gelu(gate_ref[0])
    o_ref[0] = _rms(y, gain_ref[...])

    @pl.when(ci == pl.num_programs(1) - 1)
    def _():
        conv_ref[0] = tail
        hl_ref[0] = h_last


def _rglru(lx, lg, buf0, h0, conv_w, conv_b, wai, bai, lam, gain, tc):
    b, l, w = lx.shape
    tok = pl.BlockSpec((1, tc, w), lambda i, j: (i, j, 0))
    cst = lambda a: pl.BlockSpec(a.shape, lambda i, j: (0,) * a.ndim)
    bufs = pl.BlockSpec((1, CONV_W - 1, w), lambda i, j: (i, 0, 0))
    hs = pl.BlockSpec((1, 1, w), lambda i, j: (i, 0, 0))
    return pl.pallas_call(
        _rglru_kernel,
        grid=(b, l // tc),
        in_specs=[tok, tok, bufs, hs, cst(conv_w), cst(conv_b), cst(wai), cst(bai), cst(lam), cst(gain)],
        out_specs=[tok, bufs, hs],
        out_shape=[jax.ShapeDtypeStruct((b, l, w), F32),
                   jax.ShapeDtypeStruct((b, CONV_W - 1, w), F32),
                   jax.ShapeDtypeStruct((b, 1, w), F32)],
        scratch_shapes=[pltpu.VMEM((SUBLANES + tc, w), F32), pltpu.VMEM((tc, w), F32), pltpu.VMEM((tc, w), F32),
                        pltpu.VMEM((tc, w), F32), pltpu.VMEM((1, w), F32)],
        compiler_params=_cparams(("arbitrary", "arbitrary")),
        name="rglru",
    )(lx, lg, buf0, h0, conv_w, conv_b, wai, bai, lam, gain)


def _softmax_update(s, v, m_sc, l_sc, acc_sc, rows=slice(None)):
    n = s.shape[1]
    m_prev = m_sc[rows, :]
    m_new = jnp.maximum(m_prev, jnp.max(s, axis=-1, keepdims=True))
    corr = jnp.exp(m_prev - m_new)
    p = jnp.exp(s - jnp.concatenate([m_new] * (n // LANES), axis=1))
    l_sc[rows, :] = l_sc[rows, :] * corr + jnp.sum(p, axis=-1, keepdims=True)
    acc_sc[rows, :] = acc_sc[rows, :] * corr + jnp.dot(p.astype(BF16), v, preferred_element_type=F32)
    m_sc[rows, :] = m_new


def _attn_finish(o, wuv_ref, gain_ref, rows_per_head):
    o = o.astype(BF16)
    om = None
    for hd in range(MLA_HEADS):
        part = jnp.dot(o[hd * rows_per_head:(hd + 1) * rows_per_head], wuv_ref[hd], preferred_element_type=F32)
        om = part if om is None else om + part
    return _rms(om, gain_ref[...])


def _flash_kernel(q_ref, k_ref, wuv_ref, gain_ref, o_ref, m_sc, l_sc, acc_sc, *, qb, kb):
    qi = pl.program_id(1)
    r = MLA_HEADS * qb
    m_sc[...] = jnp.full((r, LANES), -jnp.inf, F32)
    l_sc[...] = jnp.zeros((r, LANES), F32)
    acc_sc[...] = jnp.zeros((r, LANES), F32)
    q0 = qi * qb
    nfull = q0 // kb
    halves = [slice(i * (r // 2), (i + 1) * (r // 2)) for i in range(2)]
    q = q_ref[0].reshape(r, QCAT)
    qs = [q[h] for h in halves]

    def step(start, width, masked):
        kblk = k_ref[0, pl.ds(start, width), :]
        v = kblk[:, :KV_RANK]
        for h, qh in zip(halves, qs):
            s = lax.dot_general(qh, kblk, (((1,), (1,)), ((), ())), preferred_element_type=F32)
            if masked:
                row = lax.broadcasted_iota(jnp.int32, s.shape, 0)
                col = lax.broadcasted_iota(jnp.int32, s.shape, 1)
                s = jnp.where(start + col <= q0 + row % qb, s, -jnp.inf)
            _softmax_update(s, v, m_sc, l_sc, acc_sc, h)

    def body(j, carry):
        step(pl.multiple_of(j * kb, kb), kb, False)
        return carry

    lax.fori_loop(0, nfull, body, 0)
    half = kb // 2
    base = pl.multiple_of(nfull * kb, kb)
    upper = (q0 - base) >= half

    @pl.when(upper)
    def _():
        step(base, half, False)

    step(pl.multiple_of(base + jnp.where(upper, half, 0), half), half, True)
    o_ref[0] = _attn_finish(acc_sc[...] / l_sc[...], wuv_ref, gain_ref, qb)


def _mla_prompt(qcat, kcat, wuv_exp, gain, qb, kb):
    b, _, s, _ = qcat.shape
    r = MLA_HEADS * qb
    return pl.pallas_call(
        functools.partial(_flash_kernel, qb=qb, kb=kb),
        grid=(b, s // qb),
        in_specs=[pl.BlockSpec((1, MLA_HEADS, qb, QCAT), lambda i, j: (i, 0, j, 0)),
                  pl.BlockSpec((1, s, QCAT), lambda i, j: (i, 0, 0)),
                  pl.BlockSpec(wuv_exp.shape, lambda i, j: (0, 0, 0)),
                  pl.BlockSpec(gain.shape, lambda i, j: (0, 0))],
        out_specs=pl.BlockSpec((1, qb, MLA_WIDTH), lambda i, j: (i, j, 0)),
        out_shape=jax.ShapeDtypeStruct((b, s, MLA_WIDTH), F32),
        scratch_shapes=[pltpu.VMEM((r, LANES), F32)] * 3,
        compiler_params=_cparams(("arbitrary", "arbitrary")),
        name="mla_prompt",
    )(qcat, kcat, wuv_exp, gain)


def _mla_sample_kernel(pt_ref, q_ref, knew_ref, ck_hbm, kp_hbm, wuv_ref, gain_ref, o_ref, ckbuf, kpbuf, sems,
                       *, li, n_pages, page, t_new, chunk):
    bi = pl.program_id(0)
    nb = pl.num_programs(0)
    r = MLA_HEADS * t_new

    def page_copies(pg_of, slot):
        out = []
        for j in range(n_pages):
            pg = pg_of(j)
            out.append(pltpu.make_async_copy(ck_hbm.at[li, pg], ckbuf.at[slot, pl.ds(j * page, page), :],
                                             sems.at[0, slot]))
            out.append(pltpu.make_async_copy(kp_hbm.at[li, pg], kpbuf.at[slot, :, pl.ds(j * page, page)],
                                             sems.at[1, slot]))
        return out

    @pl.when(bi == 0)
    def _():
        for c in page_copies(lambda j: pt_ref[0, j], 0):
            c.start()

    @pl.when(bi + 1 < nb)
    def _():
        for c in page_copies(lambda j: pt_ref[bi + 1, j], (bi + 1) % 2):
            c.start()

    slot = bi % 2
    q = q_ref[0].reshape(r, QCAT)
    qb = q.astype(BF16)
    q_lat = qb[:, :KV_RANK]
    q_rot = q[:, KV_RANK:].astype(F32)
    q_r = q_rot[:, :MLA_ROPE]
    for i in range(1, LANES // MLA_ROPE):
        q_r = q_r + q_rot[:, i * MLA_ROPE:(i + 1) * MLA_ROPE]
    q_r = q_r.astype(BF16)

    kn = knew_ref[0].astype(BF16)
    s = lax.dot_general(qb, kn, (((1,), (1,)), ((), ())), preferred_element_type=F32)
    row = lax.broadcasted_iota(jnp.int32, s.shape, 0)
    col = lax.broadcasted_iota(jnp.int32, s.shape, 1)
    s = jnp.where(row % t_new >= col, s, -jnp.inf)
    m = jnp.max(s, axis=-1, keepdims=True)
    p = jnp.exp(s - m)
    l = jnp.sum(p, axis=-1, keepdims=True)
    acc = jnp.dot(p.astype(BF16), kn[:, :KV_RANK], preferred_element_type=F32)

    for c in page_copies(lambda j: 0, slot):
        c.wait()

    for c in range(n_pages * page // chunk):
        ck = ckbuf[slot, pl.ds(c * chunk, chunk), :].astype(BF16)
        kp = kpbuf[slot, :, pl.ds(c * chunk, chunk)].astype(BF16)
        s = (lax.dot_general(q_lat, ck, (((1,), (1,)), ((), ())), preferred_element_type=F32)
             + jnp.dot(q_r, kp, preferred_element_type=F32))
        m_new = jnp.maximum(m, jnp.max(s, axis=-1, keepdims=True))
        corr = jnp.exp(m - m_new)
        p = jnp.exp(s - m_new)
        l = l * corr + jnp.sum(p, axis=-1, keepdims=True)
        acc = acc * corr + jnp.dot(p.astype(BF16), ck, preferred_element_type=F32)
        m = m_new
    o_ref[0] = _attn_finish(acc / l, wuv_ref, gain_ref, t_new)


def _mla_sample(qcat, kcat, cache_ckv, cache_kr_t, page_table, li, wuv_exp, gain):
    b, _, t, _ = qcat.shape
    n_pages = page_table.shape[1]
    page, rank = cache_ckv.shape[2], cache_ckv.shape[3]
    rope_w = cache_kr_t.shape[2]
    chunk = min(8192, n_pages * page)
    grid_spec = pltpu.PrefetchScalarGridSpec(
        num_scalar_prefetch=1,
        grid=(b,),
        in_specs=[pl.BlockSpec((1, MLA_HEADS, t, QCAT), lambda bi, pt: (bi, 0, 0, 0)),
                  pl.BlockSpec((1, t, QCAT), lambda bi, pt: (bi, 0, 0)),
                  pl.BlockSpec(memory_space=pl.ANY),
                  pl.BlockSpec(memory_space=pl.ANY),
                  pl.BlockSpec(wuv_exp.shape, lambda bi, pt: (0, 0, 0)),
                  pl.BlockSpec(gain.shape, lambda bi, pt: (0, 0))],
        out_specs=pl.BlockSpec((1, t, MLA_WIDTH), lambda bi, pt: (bi, 0, 0)),
        scratch_shapes=[pltpu.VMEM((2, n_pages * page, rank), F32),
                        pltpu.VMEM((2, rope_w, n_pages * page), F32),
                        pltpu.SemaphoreType.DMA((2, 2))],
    )
    return pl.pallas_call(
        functools.partial(_mla_sample_kernel, li=li, n_pages=n_pages, page=page, t_new=t, chunk=chunk),
        grid_spec=grid_spec,
        out_shape=jax.ShapeDtypeStruct((b, t, MLA_WIDTH), F32),
        compiler_params=_cparams(("arbitrary",)),
        name="mla_sample",
    )(page_table, qcat, kcat, cache_ckv, cache_kr_t, wuv_exp, gain)


def _outproj_kernel(ret_ref, lru_ref, mla_ref, x_ref, w_ref, g1_ref, gt1_ref, g2_ref, sc2_ref, sh2_ref,
                    x1_ref, h2_ref):
    bt, lt, d = x_ref.shape
    rows = bt * lt
    mix = jnp.concatenate([ret_ref[...].reshape(rows, RET_WIDTH), lru_ref[...].reshape(rows, LRU_WIDTH),
                           mla_ref[...].reshape(rows, MLA_WIDTH)], axis=1).astype(BF16)
    y = jnp.dot(mix, w_ref[...], preferred_element_type=F32).reshape(bt, lt, d)
    x1 = x_ref[...] + gt1_ref[...] * _rms(y, g1_ref[...])
    x1_ref[...] = x1
    h2_ref[...] = _rms(x1, g2_ref[...]) * (1.0 + sc2_ref[...]) + sh2_ref[...]


def _outproj(ret, lru, mla, x, w_out, g1, gt1, g2, sc2, sh2, bt, lt):
    b, l, d = x.shape
    tok = lambda w: pl.BlockSpec((bt, lt, w), lambda i, j: (i, j, 0))
    mod = pl.BlockSpec((bt, 1, d), lambda i, j: (i, 0, 0))
    gsp = pl.BlockSpec((1, 1, d), lambda i, j: (0, 0, 0))
    return pl.pallas_call(
        _outproj_kernel,
        grid=(b // bt, l // lt),
        in_specs=[tok(RET_WIDTH), tok(LRU_WIDTH), tok(MLA_WIDTH), tok(d),
                  pl.BlockSpec(w_out.shape, lambda i, j: (0, 0)), gsp, mod, gsp, mod, mod],
        out_specs=[tok(d), tok(d)],
        out_shape=[jax.ShapeDtypeStruct((b, l, d), F32)] * 2,
        compiler_params=_cparams(("arbitrary", "arbitrary")),
        name="outproj",
    )(ret, lru, mla, x, w_out, g1, gt1, g2, sc2, sh2)


def _split_dot_t(w, x):
    dn = (((1,), (1,)), ((), ()))
    wh = w.astype(BF16)
    wl = (w - wh.astype(F32)).astype(BF16)
    xh = x.astype(BF16)
    xl = (x - xh.astype(F32)).astype(BF16)
    return (lax.dot_general(wh, xh, dn, preferred_element_type=F32)
            + lax.dot_general(wh, xl, dn, preferred_element_type=F32)
            + lax.dot_general(wl, xh, dn, preferred_element_type=F32))


def _router_kernel(x_ref, w_ref, b_ref, upper_ref, lower_ref, eid_ref, pos_ref, gate_ref, cnt_ref, cnt_sc):
    ti = pl.program_id(0)
    tm = x_ref.shape[0]

    @pl.when(ti == 0)
    def _():
        cnt_sc[...] = jnp.zeros(cnt_sc.shape, F32)

    logits = _split_dot_t(w_ref[...], x_ref[...])
    scores = jax.nn.sigmoid(logits)
    biased = scores + b_ref[...]
    groups = [biased[g * GROUP_SIZE:(g + 1) * GROUP_SIZE, :] for g in range(N_GROUPS)]
    sub = lax.broadcasted_iota(jnp.int32, (GROUP_SIZE, tm), 0)
    gs = []
    for bg in groups:
        m1 = jnp.max(bg, axis=0, keepdims=True)
        i1 = jnp.min(jnp.where(bg == m1, sub, GROUP_SIZE), axis=0, keepdims=True)
        m2 = jnp.max(jnp.where(sub == i1, -jnp.inf, bg), axis=0, keepdims=True)
        gs.append(m1 + m2)
    masked = []
    for g in range(N_GROUPS):
        rank = jnp.zeros((1, tm), F32)
        for g2 in range(N_GROUPS):
            if g2 != g:
                ahead = (gs[g2] >= gs[g]) if g2 < g else (gs[g2] > gs[g])
                rank = rank + jnp.where(ahead, 1.0, 0.0)
        masked.append(jnp.where(rank < TOPK_GROUPS, groups[g], -jnp.inf))
    sels = []
    for g in range(N_GROUPS):
        mine = masked[g]
        rank = jnp.zeros((GROUP_SIZE, tm), F32)
        for g2 in range(N_GROUPS):
            for s2 in range(GROUP_SIZE):
                other = jnp.broadcast_to(masked[g2][s2:s2 + 1, :], (GROUP_SIZE, tm))
                if g2 < g:
                    rank = rank + jnp.where(other >= mine, 1.0, 0.0)
                elif g2 > g:
                    rank = rank + jnp.where(other > mine, 1.0, 0.0)
                else:
                    tie = jnp.where(sub > s2, 1.0, 0.0)
                    rank = rank + jnp.where(other > mine, 1.0, jnp.where(other == mine, tie, 0.0))
        sels.append(jnp.where(rank < TOP_K, 1.0, 0.0))
    self32 = jnp.concatenate(sels, axis=0)
    sel = self32 > 0.5
    picked = jnp.where(sel, scores, 0.0)
    gate = picked / jnp.sum(picked, axis=0, keepdims=True) * ROUTED_SCALE
    selb = self32.astype(BF16)
    pos = jnp.dot(selb, upper_ref[...], preferred_element_type=F32) + cnt_sc[:, :1]
    cnt_sc[...] = cnt_sc[...] + jnp.sum(self32, axis=1, keepdims=True)
    slot = jnp.dot(lower_ref[...], selb, preferred_element_type=F32)
    efl = lax.broadcasted_iota(jnp.int32, (N_EXPERTS, tm), 0).astype(F32)
    for kk in range(TOP_K):
        hit = sel & (slot == float(kk))
        eid_ref[kk:kk + 1, :] = jnp.sum(jnp.where(hit, efl, 0.0), axis=0, keepdims=True).astype(jnp.int32)
        pos_ref[kk:kk + 1, :] = jnp.sum(jnp.where(hit, pos, 0.0), axis=0, keepdims=True).astype(jnp.int32)
        gate_ref[kk:kk + 1, :] = jnp.sum(jnp.where(hit, gate, 0.0), axis=0, keepdims=True)

    @pl.when(ti == pl.num_programs(0) - 1)
    def _():
        cnt_ref[...] = cnt_sc[...]


def _router(h2, w_rt, b_r, tm):
    t, d = h2.shape
    upper = jnp.triu(jnp.ones((tm, tm), F32), 1).astype(BF16)
    lower = jnp.tril(jnp.ones((N_EXPERTS, N_EXPERTS), F32), -1).astype(BF16)
    kt = pl.BlockSpec((TOP_K, tm), lambda i: (0, i))
    cst = lambda a: pl.BlockSpec(a.shape, lambda i: (0,) * a.ndim)
    return pl.pallas_call(
        _router_kernel,
        grid=(t // tm,),
        in_specs=[pl.BlockSpec((tm, d), lambda i: (i, 0)), cst(w_rt), cst(b_r), cst(upper), cst(lower)],
        out_specs=[kt, kt, kt, pl.BlockSpec((N_EXPERTS, LANES), lambda i: (0, 0))],
        out_shape=[jax.ShapeDtypeStruct((TOP_K, t), jnp.int32), jax.ShapeDtypeStruct((TOP_K, t), jnp.int32),
                   jax.ShapeDtypeStruct((TOP_K, t), F32), jax.ShapeDtypeStruct((N_EXPERTS, LANES), F32)],
        scratch_shapes=[pltpu.VMEM((N_EXPERTS, LANES), F32)],
        compiler_params=_cparams(("arbitrary",)),
        name="router",
    )(h2, w_rt, b_r, upper, lower)


def _dispatch_kernel(dest_ref, x_ref, rows_in_ref, rows_ref, sem):
    del rows_in_ref
    tm = x_ref.shape[0]

    def row_copy(t, d):
        return pltpu.make_async_copy(x_ref.at[t], rows_ref.at[d], sem)

    def issue(t, c):
        for kk in range(TOP_K):
            row_copy(t, dest_ref[t * TOP_K + kk]).start()
        return c

    def drain(t, c):
        for kk in range(TOP_K):
            row_copy(0, 0).wait()
        return c

    lax.fori_loop(0, tm, issue, 0)
    lax.fori_loop(0, tm, drain, 0)


def _dispatch(dest_flat, h2r, n_rows, tm):
    t, ns, nl = h2r.shape
    rows0 = jnp.zeros((n_rows, ns, nl), F32)
    return pl.pallas_call(
        _dispatch_kernel,
        grid=(t // tm,),
        in_specs=[pl.BlockSpec((tm * TOP_K,), lambda i: (i,), memory_space=pltpu.SMEM),
                  pl.BlockSpec((tm, ns, nl), lambda i: (i, 0, 0)),
                  pl.BlockSpec(memory_space=pl.ANY)],
        out_specs=pl.BlockSpec(memory_space=pl.ANY),
        out_shape=jax.ShapeDtypeStruct((n_rows, ns, nl), F32),
        scratch_shapes=[pltpu.SemaphoreType.DMA(())],
        input_output_aliases={2: 0},
        compiler_params=_cparams(("arbitrary",)),
        name="moe_dispatch",
    )(dest_flat, h2r, rows0)


def _expert_kernel(be_ref, nu_ref, rows_ref, wgu_ref, wd_ref, o_ref, wgu_sc, wd_sc, *, ns):
    i = pl.program_id(0)
    prev = be_ref[jnp.maximum(i - 1, 0)]
    fresh = (i == 0) | (be_ref[i] != prev)

    @pl.when(fresh)
    def _():
        wgu_sc[...] = wgu_ref[...].astype(BF16)
        wd_sc[...] = wd_ref[...].astype(BF16)

    @pl.when(i < nu_ref[0])
    def _():
        blk = rows_ref.shape[0] // ns
        x = jnp.concatenate([rows_ref[pl.ds(j, blk, stride=ns), :] for j in range(ns)], axis=1).astype(BF16)
        gu = jnp.dot(x, wgu_sc[...], preferred_element_type=F32)
        act = _silu(gu[:, :D_EXPERT]) * gu[:, D_EXPERT:]
        y = jnp.dot(act.astype(BF16), wd_sc[...], preferred_element_type=F32)
        for j in range(ns):
            o_ref[pl.ds(j, blk, stride=ns), :] = y[:, j * LANES:(j + 1) * LANES]


def _experts(block_e, n_used, rows, w_gu, w_down, li, blk):
    n_rows, ns, nl = rows.shape
    d = ns * nl
    n_blocks = n_rows // blk
    de2 = w_gu.shape[-1]
    de = w_down.shape[-2]

    def rmap(i, be, nu):
        return (jnp.minimum(i, nu[0] - 1), 0)

    grid_spec = pltpu.PrefetchScalarGridSpec(
        num_scalar_prefetch=2,
        grid=(n_blocks,),
        in_specs=[pl.BlockSpec((blk * ns, nl), rmap),
                  pl.BlockSpec((None, None, d, de2), lambda i, be, nu: (li, be[i], 0, 0)),
                  pl.BlockSpec((None, None, de, d), lambda i, be, nu: (li, be[i], 0, 0))],
        out_specs=pl.BlockSpec((blk * ns, nl), rmap),
        scratch_shapes=[pltpu.VMEM((d, de2), BF16), pltpu.VMEM((de, d), BF16)],
    )
    out = pl.pallas_call(
        functools.partial(_expert_kernel, ns=ns),
        grid_spec=grid_spec,
        out_shape=jax.ShapeDtypeStruct((n_rows * ns, nl), F32),
        compiler_params=_cparams(("arbitrary",)),
        name="moe_experts",
    )(block_e, n_used, rows.reshape(n_rows * ns, nl), w_gu, w_down)
    return out.reshape(n_rows, ns, nl)


def _combine_kernel(dest_ref, dnext_ref, gate_ref, orow_ref, h2_ref, x1_ref, wsgu_ref, wsd_ref, g3_ref, gt2_ref,
                    y_ref, buf, sems):
    bt, lt, d = x1_ref.shape
    tm = bt * lt
    ns = orow_ref.shape[1]
    step = pl.program_id(0) * pl.num_programs(1) + pl.program_id(1)
    nsteps = pl.num_programs(0) * pl.num_programs(1)
    slot = step % 2

    def row_copy(src_row, t, kk, sl):
        return pltpu.make_async_copy(orow_ref.at[src_row], buf.at[sl, kk, pl.ds(pl.multiple_of(t * ns, ns), ns), :],
                                     sems.at[sl])

    def issue_tile(dref, sl):
        def issue(t, c):
            for kk in range(TOP_K):
                row_copy(dref[t * TOP_K + kk], t, kk, sl).start()
            return c
        lax.fori_loop(0, tm, issue, 0)

    @pl.when(step == 0)
    def _():
        issue_tile(dest_ref, 0)

    @pl.when(step + 1 < nsteps)
    def _():
        issue_tile(dnext_ref, 1 - slot)

    hb = h2_ref[...].astype(BF16)
    gu = jnp.dot(hb, wsgu_ref[...], preferred_element_type=F32)
    ds = wsd_ref.shape[0]
    f = jnp.dot((_silu(gu[:, :ds]) * gu[:, ds:]).astype(BF16), wsd_ref[...], preferred_element_type=F32)

    def drain(t, c):
        for kk in range(TOP_K):
            row_copy(0, 0, 0, slot).wait()
        return c

    lax.fori_loop(0, tm, drain, 0)
    gate = gate_ref[...]
    fs =[f[:, j * LANES:(j + 1) * LANES] for j in range(ns)]
    for kk in range(TOP_K):
        gk = jnp.broadcast_to(gate[:, kk:kk + 1], (tm, LANES))
        for j in range(ns):
            fs[j] = fs[j] + buf[slot, kk, pl.ds(j, tm, stride=ns), :] * gk
    f = jnp.concatenate(fs, axis=1)
    y_ref[...] = x1_ref[...] + gt2_ref[...] * _rms(f, g3_ref[...]).reshape(bt, lt, d)


def _combine(dest_flat, gate_tk, out_rows, h2, x1, ws_gu, ws_down, g3, gt2, bt, lt):
    b, l, d = x1.shape
    tm = bt * lt
    nl = l // lt
    nsteps = (b // bt) * nl
    tok = pl.BlockSpec((bt, lt, d), lambda i, j: (i, j, 0))
    flat = lambda w: pl.BlockSpec((tm, w), lambda i, j: (i * nl + j, 0))
    return pl.pallas_call(
        _combine_kernel,
        grid=(b // bt, nl),
        in_specs=[pl.BlockSpec((tm * TOP_K,), lambda i, j: (i * nl + j,), memory_space=pltpu.SMEM),
                  pl.BlockSpec((tm * TOP_K,), lambda i, j: (jnp.minimum(i * nl + j + 1, nsteps - 1),),
                               memory_space=pltpu.SMEM),
                  flat(TOP_K), pl.BlockSpec(memory_space=pl.ANY), flat(d), tok,
                  pl.BlockSpec(ws_gu.shape, lambda i, j: (0, 0)), pl.BlockSpec(ws_down.shape, lambda i, j: (0, 0)),
                  pl.BlockSpec((1, 1, d), lambda i, j: (0, 0, 0)),
                  pl.BlockSpec((bt, 1, d), lambda i, j: (i, 0, 0))],
        out_specs=tok,
        out_shape=jax.ShapeDtypeStruct((b, l, d), F32),
        scratch_shapes=[pltpu.VMEM((2, TOP_K, tm * out_rows.shape[1], out_rows.shape[2]), F32), pltpu.SemaphoreType.DMA((2,))],
        compiler_params=_cparams(("arbitrary", "arbitrary")),
        name="moe_combine",
    )(dest_flat, dest_flat, gate_tk, out_rows, h2, x1, ws_gu, ws_down, g3, gt2)


def _moe(h2, x1, gt2, g3, w_rt, b_r, w_exp_gu, w_exp_down, ws_gu, ws_down, li, bt, lt, blk):
    b, l, d = x1.shape
    t = b * l
    h2f = h2.reshape(t, d)
    tm_r = min(512, t)
    eid, pos, gate, cnt = _router(h2f, w_rt, b_r, tm_r)
    counts = cnt[:, 0].astype(jnp.int32)
    padded = (counts + blk - 1) // blk * blk
    pends = jnp.cumsum(padded)
    pstart = pends - padded
    n_blocks = -(-(t * TOP_K) // blk) + N_EXPERTS
    starts = jnp.arange(n_blocks, dtype=jnp.int32) * blk
    block_e = jnp.minimum(jnp.sum((pends[None, :] <= starts[:, None]).astype(jnp.int32), axis=1), N_EXPERTS - 1)
    n_used = (pends[-1:] // blk).astype(jnp.int32)
    first_row = jnp.sum(jnp.where(eid[:, :, None] == jnp.arange(N_EXPERTS, dtype=jnp.int32), pstart, 0), axis=-1)
    dest = (first_row + pos).T.reshape(t * TOP_K)
    gate_tk = gate.T
    tm_d = min(512, t)
    rows = _dispatch(dest, h2f.reshape(t, d // LANES, LANES), n_blocks * blk, tm_d)
    out_rows = _experts(block_e, n_used, rows, w_exp_gu, w_exp_down, li, blk)
    return _combine(dest, gate_tk, out_rows, h2f, x1, ws_gu, ws_down, g3, gt2, bt, lt)


def _proj_perm():
    base = np.arange(C_QN)
    mq0 = C_QN
    qn = np.concatenate([mq0 + h * (MLA_NOPE + MLA_ROPE) + np.arange(MLA_NOPE) for h in range(MLA_HEADS)])
    qr = np.concatenate([mq0 + h * (MLA_NOPE + MLA_ROPE) + MLA_NOPE + np.arange(MLA_ROPE) for h in range(MLA_HEADS)])
    mc0 = mq0 + MLA_HEADS * (MLA_NOPE + MLA_ROPE)
    mc = mc0 + np.arange(KV_RANK)
    mk = np.tile(mc0 + KV_RANK + np.arange(MLA_ROPE), LANES // MLA_ROPE)
    return np.concatenate([base, qn, qr, mc, mk])


def _rope_tables(pos, dim):
    half = dim // 2
    inv = ROPE_BASE ** (-jnp.arange(0, dim, 2, dtype=F32) / dim)
    ang = pos.astype(F32)[:, None] * inv[None, :]
    cos, sin = jnp.cos(ang), jnp.sin(ang)
    reps = LANES // dim
    cos_t = jnp.tile(jnp.concatenate([cos, cos], axis=1), (1, reps))
    sin_t = jnp.tile(jnp.concatenate([-sin, sin], axis=1), (1, reps))
    return cos_t, sin_t


def _block_diag(blocks):
    n, a, b = blocks.shape
    eye = jnp.eye(n, dtype=blocks.dtype)
    return (eye[:, None, :, None] * blocks[:, :, None, :]).reshape(n * a, n * b)


def _layer(x, mod, pos_tabs, li, p, past, tiles):
    b, l, d = x.shape
    bt, lt = tiles['bt'], tiles['lt']
    sh1, sc1, gt1, sh2, sc2, gt2 = [m[:, None, :] for m in jnp.split(mod, 6, axis=-1)]
    g = p['norm_gains']
    gains = [g[i][None, None, :] for i in range(4)]
    gain = p['grp_gain']
    rq, rk, rv, rg, lx, lg, qcat, kcat, ckv, kr = _inproj(
        x, sc1, sh1, gains[0], p['w_in_p'], p['wuk_bd'], p['kv_norm'][None, :], pos_tabs, bt, lt, tiles['qdtype'])
    if past is None:
        r0 = jnp.zeros((b, RET_HEADS, RET_DK, RET_DV), F32)
        buf0 = jnp.zeros((b, CONV_W - 1, LRU_WIDTH), F32)
        h0 = jnp.zeros((b, 1, LRU_WIDTH), F32)
    else:
        r0, buf0, h0 = past['ret'], past['conv'], past['lru'][:, None, :]
    ret_out, r_new = _retention(rq, rk, rv, rg, r0, gain[None, :RET_WIDTH])
    lru_out, conv_new, h_new = _rglru(
        lx, lg, buf0, h0, p['conv_w'], p['conv_b'][None, :], p['wai'], p['bai'], p['lru_lambda'][None, :],
        gain[None, RET_WIDTH:RET_WIDTH + LRU_WIDTH], tiles['tc'])
    mla_gain = gain[None, RET_WIDTH + LRU_WIDTH:]
    if past is None:
        mla_out = _mla_prompt(qcat, kcat, p['wuv_exp'], mla_gain, tiles['qb'], tiles['kb'])
    else:
        mla_out = _mla_sample(qcat, kcat, past['cache_ckv'], past['cache_kr_t'], past['page_table'], li,
                              p['wuv_exp'], mla_gain)
    x1, h2 = _outproj(ret_out, lru_out, mla_out, x, p['w_out'], gains[1], gt1, gains[2], sc2, sh2, bt, lt)
    y = _moe(h2, x1, gt2, gains[3], p['w_rt'], p['b_router'][:, None], p['w_exp_gu'], p['w_exp_down'],
             p['ws_gu'], p['ws_down'], li, tiles['cbt'], tiles['clt'], tiles['blk'])
    return y, (ckv, kr, r_new, conv_new, h_new[:, 0])


def _largest_tile(n, cap):
    t = min(n, cap)
    while n % t:
        t //= 2
    return t


def kernel(x_prompt, x_sample, cache_ckv, cache_krope, state_ret, state_conv, state_lru, page_table, c_prompt, c_sample, w_ada, b_ada, norm_gains, w_in, w_out, grp_gain, conv_w, conv_b, w_lru_a, b_lru_a, w_lru_i, b_lru_i, lru_lambda, kv_norm, w_ukv, w_router, b_router, w_exp_gu, w_exp_down, w_sh_gu, w_sh_down):
    depth = w_in.shape[0]
    bp, lp, d = x_prompt.shape
    bs, ls, _ = x_sample.shape
    past_len = page_table.shape[1] * cache_ckv.shape[2]

    n_c = bp + bs
    n_cp = -(-n_c // SUBLANES) * SUBLANES
    c_all = jnp.concatenate([c_prompt, c_sample, jnp.zeros((n_cp - n_c, d), F32)], axis=0)
    mod_all = _adaln(c_all, w_ada, b_ada[:, None, :])

    w_in_p = w_in[:, :, _proj_perm()].astype(BF16)
    w_uk = w_ukv[..., :MLA_NOPE]
    w_uv = w_ukv[..., MLA_NOPE:]
    eye_h = jnp.eye(MLA_HEADS, dtype=F32)

    pos_p = jnp.arange(lp)
    pos_s = past_len + jnp.arange(ls)
    tabs_p = _rope_tables(pos_p, RET_DK) + _rope_tables(pos_p, MLA_ROPE)
    tabs_s = _rope_tables(pos_s, RET_DK) + _rope_tables(pos_s, MLA_ROPE)

    lt_p = _largest_tile(lp, 256)
    kb_p = _largest_tile(lp, 1024)
    tiles_p = dict(bt=1, lt=lt_p, qdtype=BF16, tc=_largest_tile(lp, 256), qb=_largest_tile(lp, min(128, kb_p // 2)),
                   kb=kb_p, cbt=1, clt=_largest_tile(lp, 128), blk=256 if bp * lp >= 4096 else 128)
    bt_s = _largest_tile(bs, 32)
    tiles_s = dict(bt=bt_s, lt=ls, qdtype=F32, tc=ls,
                   cbt=_largest_tile(bs, 16), clt=ls, blk=128)

    cache_kr_t = jnp.swapaxes(cache_krope, 2, 3)
    y_p, y_s = x_prompt, x_sample
    st_p, st_s = [], []
    for li in range(depth):
        wuk_bd = _block_diag(jnp.transpose(w_uk[li], (1, 2, 0))).astype(BF16)
        wuv_h = jnp.transpose(w_uv[li], (1, 0, 2))
        wuv_exp = (wuv_h[:, :, None, :] * eye_h[:, None, :, None]).reshape(
            MLA_HEADS, KV_RANK, MLA_WIDTH).astype(BF16)
        wai = jnp.concatenate([_block_diag(w_lru_a[li]), _block_diag(w_lru_i[li])], axis=1).astype(BF16)
        bai = jnp.concatenate([b_lru_a[li], b_lru_i[li]])[None, :]
        p = dict(norm_gains=norm_gains[li], w_in_p=w_in_p[li], wuk_bd=wuk_bd, wuv_exp=wuv_exp,
                 w_out=w_out[li].astype(BF16), grp_gain=grp_gain[li], conv_w=conv_w[li], conv_b=conv_b[li],
                 wai=wai, bai=bai, lru_lambda=lru_lambda[li], kv_norm=kv_norm[li],
                 w_rt=w_router[li].T, b_router=b_router[li], w_exp_gu=w_exp_gu, w_exp_down=w_exp_down,
                 ws_gu=w_sh_gu[li].astype(BF16), ws_down=w_sh_down[li].astype(BF16))
        y_p, sp = _layer(y_p, mod_all[li, :bp], tabs_p, li, p, None, tiles_p)
        past = dict(ret=state_ret[li], conv=state_conv[li], lru=state_lru[li], cache_ckv=cache_ckv,
                    cache_kr_t=cache_kr_t, page_table=page_table)
        y_s, ss = _layer(y_s, mod_all[li, bp:bp + bs], tabs_s, li, p, past, tiles_s)
        st_p.append(sp)
        st_s.append(ss)
    stack = lambda sts, i: jnp.stack([s[i] for s in sts])
    return (y_p, y_s) + tuple(stack(st_p, i) for i in range(5)) + tuple(stack(st_s, i) for i in range(5))
```

```python
import functools

import numpy as np
import jax
import jax.numpy as jnp
from jax import lax
from jax.experimental import pallas as pl
from jax.experimental.pallas import tpu as pltpu

F32 = jnp.float32
BF16 = jnp.bfloat16
EPS = 1e-6
ROPE_BASE = 10000.0

RET_HEADS = 4
RET_DK = 64
RET_DV = 64
RET_WIDTH = RET_HEADS * RET_DV
RET_CHUNK = 128
LRU_WIDTH = 256
LRU_BLOCKS = 4
LRU_BLOCK = LRU_WIDTH // LRU_BLOCKS
CONV_W = 4
LRU_C = 8.0
MLA_HEADS = 8
MLA_NOPE = 64
MLA_ROPE = 32
MLA_V = 64
KV_RANK = 128
MLA_WIDTH = MLA_HEADS * MLA_V
N_EXPERTS = 64
TOP_K = 8
N_GROUPS = 8
GROUP_SIZE = N_EXPERTS // N_GROUPS
TOPK_GROUPS = 4
D_EXPERT = 256
ROUTED_SCALE = 2.5

LANES = 128
SUBLANES = 8
QCAT = 2 * LANES
C_RQ, C_RK, C_RV, C_RG, C_LX, C_LG = 0, 256, 512, 768, 1024, 1280
C_QN = 1536
C_QR = C_QN + MLA_HEADS * MLA_NOPE
C_MC = C_QR + MLA_HEADS * MLA_ROPE
C_MK = C_MC + KV_RANK
N_PROJ = C_MK + LANES


def _cparams(sem, vmem_mb=48):
    return pltpu.CompilerParams(dimension_semantics=sem, vmem_limit_bytes=vmem_mb * 2 ** 20)


def _rms(x, g):
    return x * lax.rsqrt(jnp.mean(x * x, axis=-1, keepdims=True) + EPS) * g


def _silu(x):
    return x * jax.nn.sigmoid(x)


def _adaln_kernel(c_ref, w_ref, b_ref, o_ref):
    c = c_ref[...]
    s = _silu(c).astype(BF16)
    o_ref[...] = jnp.dot(s, w_ref[...].astype(BF16), preferred_element_type=F32) + b_ref[...]


def _adaln(c, w_ada, b_ada):
    depth, d, n = w_ada.shape
    bp = c.shape[0]
    tn = 768
    return pl.pallas_call(
        _adaln_kernel,
        grid=(depth, n // tn),
        in_specs=[pl.BlockSpec((bp, d), lambda l, j: (0, 0)),
                  pl.BlockSpec((None, d, tn), lambda l, j: (l, 0, j)),
                  pl.BlockSpec((None, 1, tn), lambda l, j: (l, 0, j))],
        out_specs=pl.BlockSpec((None, bp, tn), lambda l, j: (l, 0, j)),
        out_shape=jax.ShapeDtypeStruct((depth, bp, n), F32),
        compiler_params=_cparams(("arbitrary", "arbitrary")),
        name="adaln",
    )(c, w_ada, b_ada)


def _rope_slab(v, cos, sin_signed, half):
    lane = lax.broadcasted_iota(jnp.int32, v.shape, 1)
    first = (lane % (2 * half)) < half
    partner = jnp.where(first, pltpu.roll(v, LANES - half, 1), pltpu.roll(v, half, 1))
    return v * cos + partner * sin_signed


def _rope(v, cos, sin_signed, half):
    n = v.shape[1] // LANES
    outs = [_rope_slab(v[:, i * LANES:(i + 1) * LANES], cos, sin_signed, half) for i in range(n)]
    return outs[0] if n == 1 else jnp.concatenate(outs, axis=1)


def _inproj_kernel(x_ref, sc_ref, sh_ref, g_ref, w_ref, wuk_ref, kvg_ref, c64_ref, s64_ref, c32_ref, s32_ref,
                   rq_ref, rk_ref, rv_ref, rg_ref, lx_ref, lg_ref, qcat_ref, kcat_ref, ckv_ref, kr_ref):
    bt, lt, d = x_ref.shape
    rows = bt * lt
    h = _rms(x_ref[...], g_ref[...]) * (1.0 + sc_ref[...]) + sh_ref[...]
    hb = h.reshape(rows, d).astype(BF16)

    def proj(a, b):
        return jnp.dot(hb, w_ref[:, a:b], preferred_element_type=F32)

    def table(ref):
        t = ref[...]
        if bt > 1:
            t = jnp.broadcast_to(t[None], (bt, lt, LANES)).reshape(rows, LANES)
        return t

    c64, s64, c32, s32 = table(c64_ref), table(s64_ref), table(c32_ref), table(s32_ref)
    rq_ref[...] = _rope(proj(C_RQ, C_RK), c64, s64, RET_DK // 2).reshape(bt, lt, RET_WIDTH)
    rk_ref[...] = _rope(proj(C_RK, C_RV), c64, s64, RET_DK // 2).reshape(bt, lt, RET_WIDTH)
    rv_ref[...] = proj(C_RV, C_RG).reshape(bt, lt, RET_WIDTH)
    rg_ref[...] = proj(C_RG, C_LX).reshape(bt, lt, RET_WIDTH)
    lx_ref[...] = proj(C_LX, C_LG).reshape(bt, lt, LRU_WIDTH)
    lg_ref[...] = proj(C_LG, C_QN).reshape(bt, lt, LRU_WIDTH)

    scale = (MLA_NOPE + MLA_ROPE) ** -0.5
    q_lat = jnp.dot(proj(C_QN, C_QR).astype(BF16), wuk_ref[...], preferred_element_type=F32) * scale
    q_rope = _rope(proj(C_QR, C_MC), c32, s32, MLA_ROPE // 2) * scale
    lane = lax.broadcasted_iota(jnp.int32, (rows, LANES), 1)
    heads_per_slab = LANES // MLA_ROPE
    for hd in range(MLA_HEADS):
        slab = q_rope[:, (hd // heads_per_slab) * LANES:(hd // heads_per_slab + 1) * LANES]
        own = jnp.where(lane // MLA_ROPE == hd % heads_per_slab, slab, 0.0)
        qh = jnp.concatenate([q_lat[:, hd * KV_RANK:(hd + 1) * KV_RANK], own], axis=1)
        qcat_ref[:, hd, :, :] = qh.reshape(bt, lt, QCAT).astype(qcat_ref.dtype)

    ckv = _rms(proj(C_MC, C_MK), kvg_ref[...])
    kr4 = _rope(proj(C_MK, N_PROJ), c32, s32, MLA_ROPE // 2)
    ckv_ref[...] = ckv.reshape(bt, lt, KV_RANK)
    kr_ref[...] = kr4[:, :MLA_ROPE].reshape(bt, lt, MLA_ROPE)
    kcat_ref[...] = jnp.concatenate([ckv, kr4], axis=1).reshape(bt, lt, QCAT).astype(kcat_ref.dtype)


def _inproj(x, sc, sh, g, w_in_p, wuk_bd, kv_gain, tabs, bt, lt, qdtype):
    b, l, d = x.shape
    grid = (b // bt, l // lt)
    c64, s64, c32, s32 = tabs
    tok = lambda w: pl.BlockSpec((bt, lt, w), lambda i, j: (i, j, 0))
    mod = pl.BlockSpec((bt, 1, d), lambda i, j: (i, 0, 0))
    const2 = lambda a: pl.BlockSpec(a.shape, lambda i, j: (0, 0))
    tab = pl.BlockSpec((lt, LANES), lambda i, j: (j, 0))
    out_shape = [jax.ShapeDtypeStruct((b, l, 256), F32)] * 6 + [
        jax.ShapeDtypeStruct((b, MLA_HEADS, l, QCAT), qdtype),
        jax.ShapeDtypeStruct((b, l, QCAT), qdtype),
        jax.ShapeDtypeStruct((b, l, KV_RANK), F32),
        jax.ShapeDtypeStruct((b, l, MLA_ROPE), F32)]
    out_specs = [tok(256)] * 6 + [
        pl.BlockSpec((bt, MLA_HEADS, lt, QCAT), lambda i, j: (i, 0, j, 0)),
        tok(QCAT), tok(KV_RANK), tok(MLA_ROPE)]
    return pl.pallas_call(
        _inproj_kernel,
        grid=grid,
        in_specs=[tok(d), mod, mod, pl.BlockSpec((1, 1, d), lambda i, j: (0, 0, 0)),
                  const2(w_in_p), const2(wuk_bd), const2(kv_gain), tab, tab, tab, tab],
        out_specs=out_specs,
        out_shape=out_shape,
        compiler_params=_cparams(("arbitrary", "arbitrary")),
        name="inproj",
    )(x, sc, sh, g, w_in_p, wuk_bd, kv_gain, c64, s64, c32, s32)


def _retention_kernel(q_ref, k_ref, v_ref, g_ref, r0_ref, dmat_ref, xi_ref, zeta_ref, dec_ref, gain_ref,
                      o_ref, rfin_ref, r_sc):
    ci = pl.program_id(1)

    @pl.when(ci == 0)
    def _():
        r_sc[...] = r0_ref[0]

    q = q_ref[0]
    k = k_ref[0] * (RET_DK ** -0.5)
    v = v_ref[0]
    outs = []
    for hd in range(RET_HEADS):
        sl = slice(hd * RET_DK, (hd + 1) * RET_DK)
        qh = q[:, sl].astype(BF16)
        kh = k[:, sl]
        vh = v[:, sl].astype(BF16)
        s = lax.dot_general(qh, kh.astype(BF16), (((1,), (1,)), ((), ())), preferred_element_type=F32)
        inner = jnp.dot((s * dmat_ref[hd]).astype(BF16), vh, preferred_element_type=F32)
        r = r_sc[hd]
        cross = jnp.dot(qh, r.astype(BF16), preferred_element_type=F32) * xi_ref[:, hd:hd + 1]
        o = inner + cross
        mu = jnp.mean(o, axis=-1, keepdims=True)
        oc = o - mu
        var = jnp.mean(oc * oc, axis=-1, keepdims=True)
        outs.append(oc * lax.rsqrt(var + EPS))
        kz = (kh * zeta_ref[:, hd:hd + 1]).astype(BF16)
        kv = lax.dot_general(kz, vh, (((0,), (0,)), ((), ())), preferred_element_type=F32)
        r_sc[hd] = dec_ref[hd] * r + kv
    o_ref[0] = jnp.concatenate(outs, axis=1) * gain_ref[...] * _silu(g_ref[0])

    @pl.when(ci == pl.num_programs(1) - 1)
    def _():
        rfin_ref[0] = r_sc[...]


def _retention_consts(c):
    log_g = jnp.log1p(-jnp.exp2(-5.0 - jnp.arange(RET_HEADS, dtype=F32)))
    i = jnp.arange(c, dtype=F32)
    diff = i[:, None] - i[None, :]
    dmat = jnp.where(diff >= 0, jnp.exp(jnp.maximum(diff, 0.0)[None] * log_g[:, None, None]), 0.0)
    xi = jnp.exp((i[:, None] + 1.0) * log_g[None, :])
    zeta = jnp.exp((c - 1.0 - i)[:, None] * log_g[None, :])
    dec = jnp.broadcast_to(jnp.exp(c * log_g)[:, None, None], (RET_HEADS, 1, RET_DV))
    return dmat, xi, zeta, dec


def _retention(rq, rk, rv, rg, r0, gain):
    b, l, _ = rq.shape
    c = RET_CHUNK if l % RET_CHUNK == 0 else l
    n = l // c
    dmat, xi, zeta, dec = _retention_consts(c)
    tok = pl.BlockSpec((1, c, RET_WIDTH), lambda i, j: (i, j, 0))
    st = pl.BlockSpec((1, RET_HEADS, RET_DK, RET_DV), lambda i, j: (i, 0, 0, 0))
    cst = lambda a: pl.BlockSpec(a.shape, lambda i, j: (0,) * a.ndim)
    return pl.pallas_call(
        _retention_kernel,
        grid=(b, n),
        in_specs=[tok, tok, tok, tok, st, cst(dmat), cst(xi), cst(zeta), cst(dec), cst(gain)],
        out_specs=[tok, st],
        out_shape=[jax.ShapeDtypeStruct((b, l, RET_WIDTH), F32),
                   jax.ShapeDtypeStruct((b, RET_HEADS, RET_DK, RET_DV), F32)],
        scratch_shapes=[pltpu.VMEM((RET_HEADS, RET_DK, RET_DV), F32)],
        compiler_params=_cparams(("arbitrary", "arbitrary")),
        name="retention",
    )(rq, rk, rv, rg, r0, dmat, xi, zeta, dec, gain)


def _rglru_kernel(x_ref, gate_ref, buf_ref, h0_ref, cw_ref, cb_ref, wai_ref, bai_ref, lam_ref, gain_ref,
                  o_ref, conv_ref, hl_ref, xin_sc, a_sc, b_sc, h_sc, hc_sc):
    ci = pl.program_id(1)
    tc = x_ref.shape[1]
    pad = SUBLANES - (CONV_W - 1)

    @pl.when(ci == 0)
    def _():
        xin_sc[pad:SUBLANES, :] = buf_ref[0]
        hc_sc[...] = h0_ref[0]

    xin_sc[SUBLANES:SUBLANES + tc, :] = x_ref[0]
    xc = cb_ref[...]
    for kk in range(CONV_W):
        xc = xc + xin_sc[pad + kk:pad + kk + tc, :] * cw_ref[kk:kk + 1, :]
    tail = xin_sc[SUBLANES + tc - (CONV_W - 1):SUBLANES + tc, :]
    xin_sc[pad:SUBLANES, :] = tail

    z = jnp.dot(xc.astype(BF16), wai_ref[...], preferred_element_type=F32) + bai_ref[...]
    r = jax.nn.sigmoid(z[:, :LRU_WIDTH])
    ig = jax.nn.sigmoid(z[:, LRU_WIDTH:])
    log_a = -LRU_C * r * jax.nn.softplus(-lam_ref[...])
    a = jnp.exp(log_a)
    a_sc[...] = a
    b_sc[...] = jnp.sqrt(-jnp.tanh(log_a) * (a * a + 1.0)) * ig * xc

    def body(t, h):
        h = a_sc[pl.ds(t, 1), :] * h + b_sc[pl.ds(t, 1), :]
        h_sc[pl.ds(t, 1), :] = h
        return h

    h_last = lax.fori_loop(0, tc, body, hc_sc[...], unroll=8)
    hc_sc[...] = h_last
    y = h_sc[...] * jax.nn.gelu(gate_ref[0])
    o_ref[0] = _rms(y, gain_ref[...])

    @pl.when(ci == pl.num_programs(1) - 1)
    def _():
        conv_ref[0] = tail
        hl_ref[0] = h_last


def _rglru(lx, lg, buf0, h0, conv_w, conv_b, wai, bai, lam, gain, tc):
    b, l, w = lx.shape
    tok = pl.BlockSpec((1, tc, w), lambda i, j: (i, j, 0))
    cst = lambda a: pl.BlockSpec(a.shape, lambda i, j: (0,) * a.ndim)
    bufs = pl.BlockSpec((1, CONV_W - 1, w), lambda i, j: (i, 0, 0))
    hs = pl.BlockSpec((1, 1, w), lambda i, j: (i, 0, 0))
    return pl.pallas_call(
        _rglru_kernel,
        grid=(b, l // tc),
        in_specs=[tok, tok, bufs, hs, cst(conv_w), cst(conv_b), cst(wai), cst(bai), cst(lam), cst(gain)],
        out_specs=[tok, bufs, hs],
        out_shape=[jax.ShapeDtypeStruct((b, l, w), F32),
                   jax.ShapeDtypeStruct((b, CONV_W - 1, w), F32),
                   jax.ShapeDtypeStruct((b, 1, w), F32)],
        scratch_shapes=[pltpu.VMEM((SUBLANES + tc, w), F32), pltpu.VMEM((tc, w), F32), pltpu.VMEM((tc, w), F32),
                        pltpu.VMEM((tc, w), F32), pltpu.VMEM((1, w), F32)],
        compiler_params=_cparams(("arbitrary", "arbitrary")),
        name="rglru",
    )(lx, lg, buf0, h0, conv_w, conv_b, wai, bai, lam, gain)


def _softmax_update(s, v, m_sc, l_sc, acc_sc, rows=slice(None)):
    n = s.shape[1]
    m_prev = m_sc[rows, :]
    m_new = jnp.maximum(m_prev, jnp.max(s, axis=-1, keepdims=True))
    corr = jnp.exp(m_prev - m_new)
    p = jnp.exp(s - jnp.concatenate([m_new] * (n // LANES), axis=1))
    l_sc[rows, :] = l_sc[rows, :] * corr + jnp.sum(p, axis=-1, keepdims=True)
    acc_sc[rows, :] = acc_sc[rows, :] * corr + jnp.dot(p.astype(BF16), v, preferred_element_type=F32)
    m_sc[rows, :] = m_new


def _attn_finish(o, wuv_ref, gain_ref, rows_per_head):
    o = o.astype(BF16)
    om = None
    for hd in range(MLA_HEADS):
        part = jnp.dot(o[hd * rows_per_head:(hd + 1) * rows_per_head], wuv_ref[hd], preferred_element_type=F32)
        om = part if om is None else om + part
    return _rms(om, gain_ref[...])


def _flash_kernel(q_ref, k_ref, wuv_ref, gain_ref, o_ref, m_sc, l_sc, acc_sc, *, qb, kb):
    qi = pl.program_id(1)
    r = MLA_HEADS * qb
    m_sc[...] = jnp.full((r, LANES), -jnp.inf, F32)
    l_sc[...] = jnp.zeros((r, LANES), F32)
    acc_sc[...] = jnp.zeros((r, LANES), F32)
    q0 = qi * qb
    nfull = q0 // kb
    halves = [slice(i * (r // 2), (i + 1) * (r // 2)) for i in range(2)]
    q = q_ref[0].reshape(r, QCAT)
    qs = [q[h] for h in halves]

    def step(start, width, masked):
        kblk = k_ref[0, pl.ds(start, width), :]
        v = kblk[:, :KV_RANK]
        for h, qh in zip(halves, qs):
            s = lax.dot_general(qh, kblk, (((1,), (1,)), ((), ())), preferred_element_type=F32)
            if masked:
                row = lax.broadcasted_iota(jnp.int32, s.shape, 0)
                col = lax.broadcasted_iota(jnp.int32, s.shape, 1)
                s = jnp.where(start + col <= q0 + row % qb, s, -jnp.inf)
            _softmax_update(s, v, m_sc, l_sc, acc_sc, h)

    def body(j, carry):
        step(pl.multiple_of(j * kb, kb), kb, False)
        return carry

    lax.fori_loop(0, nfull, body, 0)
    half = kb // 2
    base = pl.multiple_of(nfull * kb, kb)
    upper = (q0 - base) >= half

    @pl.when(upper)
    def _():
        step(base, half, False)

    step(pl.multiple_of(base + jnp.where(upper, half, 0), half), half, True)
    o_ref[0] = _attn_finish(acc_sc[...] / l_sc[...], wuv_ref, gain_ref, qb)


def _mla_prompt(qcat, kcat, wuv_exp, gain, qb, kb):
    b, _, s, _ = qcat.shape
    r = MLA_HEADS * qb
    return pl.pallas_call(
        functools.partial(_flash_kernel, qb=qb, kb=kb),
        grid=(b, s // qb),
        in_specs=[pl.BlockSpec((1, MLA_HEADS, qb, QCAT), lambda i, j: (i, 0, j, 0)),
                  pl.BlockSpec((1, s, QCAT), lambda i, j: (i, 0, 0)),
                  pl.BlockSpec(wuv_exp.shape, lambda i, j: (0, 0, 0)),
                  pl.BlockSpec(gain.shape, lambda i, j: (0, 0))],
        out_specs=pl.BlockSpec((1, qb, MLA_WIDTH), lambda i, j: (i, j, 0)),
        out_shape=jax.ShapeDtypeStruct((b, s, MLA_WIDTH), F32),
        scratch_shapes=[pltpu.VMEM((r, LANES), F32)] * 3,
        compiler_params=_cparams(("arbitrary", "arbitrary")),
        name="mla_prompt",
    )(qcat, kcat, wuv_exp, gain)


def _mla_sample_kernel(pt_ref, q_ref, knew_ref, ck_hbm, kp_hbm, wuv_ref, gain_ref, o_ref, ckbuf, kpbuf, sems,
                       *, li, n_pages, page, t_new, chunk):
    bi = pl.program_id(0)
    nb = pl.num_programs(0)
    r = MLA_HEADS * t_new

    def page_copies(pg_of, slot):
        out = []
        for j in range(n_pages):
            pg = pg_of(j)
            out.append(pltpu.make_async_copy(ck_hbm.at[li, pg], ckbuf.at[slot, pl.ds(j * page, page), :],
                                             sems.at[0, slot]))
            out.append(pltpu.make_async_copy(kp_hbm.at[li, pg], kpbuf.at[slot, :, pl.ds(j * page, page)],
                                             sems.at[1, slot]))
        return out

    @pl.when(bi == 0)
    def _():
        for c in page_copies(lambda j: pt_ref[0, j], 0):
            c.start()

    @pl.when(bi + 1 < nb)
    def _():
        for c in page_copies(lambda j: pt_ref[bi + 1, j], (bi + 1) % 2):
            c.start()

    slot = bi % 2
    q = q_ref[0].reshape(r, QCAT)
    qb = q.astype(BF16)
    q_lat = qb[:, :KV_RANK]
    q_rot = q[:, KV_RANK:].astype(F32)
    q_r = q_rot[:, :MLA_ROPE]
    for i in range(1, LANES // MLA_ROPE):
        q_r = q_r + q_rot[:, i * MLA_ROPE:(i + 1) * MLA_ROPE]
    q_r = q_r.astype(BF16)

    kn = knew_ref[0].astype(BF16)
    s = lax.dot_general(qb, kn, (((1,), (1,)), ((), ())), preferred_element_type=F32)
    row = lax.broadcasted_iota(jnp.int32, s.shape, 0)
    col = lax.broadcasted_iota(jnp.int32, s.shape, 1)
    s = jnp.where(row % t_new >= col, s, -jnp.inf)
    m = jnp.max(s, axis=-1, keepdims=True)
    p = jnp.exp(s - m)
    l = jnp.sum(p, axis=-1, keepdims=True)
    acc = jnp.dot(p.astype(BF16), kn[:, :KV_RANK], preferred_element_type=F32)

    for c in page_copies(lambda j: 0, slot):
        c.wait()

    for c in range(n_pages * page // chunk):
        ck = ckbuf[slot, pl.ds(c * chunk, chunk), :].astype(BF16)
        kp = kpbuf[slot, :, pl.ds(c * chunk, chunk)].astype(BF16)
        s = (lax.dot_general(q_lat, ck, (((1,), (1,)), ((), ())), preferred_element_type=F32)
             + jnp.dot(q_r, kp, preferred_element_type=F32))
        m_new = jnp.maximum(m, jnp.max(s, axis=-1, keepdims=True))
        corr = jnp.exp(m - m_new)
        p = jnp.exp(s - m_new)
        l = l * corr + jnp.sum(p, axis=-1, keepdims=True)
        acc = acc * corr + jnp.dot(p.astype(BF16), ck, preferred_element_type=F32)
        m = m_new
    o_ref[0] = _attn_finish(acc / l, wuv_ref, gain_ref, t_new)


def _mla_sample(qcat, kcat, cache_ckv, cache_kr_t, page_table, li, wuv_exp, gain):
    b, _, t, _ = qcat.shape
    n_pages = page_table.shape[1]
    page, rank = cache_ckv.shape[2], cache_ckv.shape[3]
    rope_w = cache_kr_t.shape[2]
    chunk = min(8192, n_pages * page)
    grid_spec = pltpu.PrefetchScalarGridSpec(
        num_scalar_prefetch=1,
        grid=(b,),
        in_specs=[pl.BlockSpec((1, MLA_HEADS, t, QCAT), lambda bi, pt: (bi, 0, 0, 0)),
                  pl.BlockSpec((1, t, QCAT), lambda bi, pt: (bi, 0, 0)),
                  pl.BlockSpec(memory_space=pl.ANY),
                  pl.BlockSpec(memory_space=pl.ANY),
                  pl.BlockSpec(wuv_exp.shape, lambda bi, pt: (0, 0, 0)),
                  pl.BlockSpec(gain.shape, lambda bi, pt: (0, 0))],
        out_specs=pl.BlockSpec((1, t, MLA_WIDTH), lambda bi, pt: (bi, 0, 0)),
        scratch_shapes=[pltpu.VMEM((2, n_pages * page, rank), F32),
                        pltpu.VMEM((2, rope_w, n_pages * page), F32),
                        pltpu.SemaphoreType.DMA((2, 2))],
    )
    return pl.pallas_call(
        functools.partial(_mla_sample_kernel, li=li, n_pages=n_pages, page=page, t_new=t, chunk=chunk),
        grid_spec=grid_spec,
        out_shape=jax.ShapeDtypeStruct((b, t, MLA_WIDTH), F32),
        compiler_params=_cparams(("arbitrary",)),
        name="mla_sample",
    )(page_table, qcat, kcat, cache_ckv, cache_kr_t, wuv_exp, gain)


def _outproj_kernel(ret_ref, lru_ref, mla_ref, x_ref, w_ref, g1_ref, gt1_ref, g2_ref, sc2_ref, sh2_ref,
                    x1_ref, h2_ref):
    bt, lt, d = x_ref.shape
    rows = bt * lt
    mix = jnp.concatenate([ret_ref[...].reshape(rows, RET_WIDTH), lru_ref[...].reshape(rows, LRU_WIDTH),
                           mla_ref[...].reshape(rows, MLA_WIDTH)], axis=1).astype(BF16)
    y = jnp.dot(mix, w_ref[...], preferred_element_type=F32).reshape(bt, lt, d)
    x1 = x_ref[...] + gt1_ref[...] * _rms(y, g1_ref[...])
    x1_ref[...] = x1
    h2_ref[...] = _rms(x1, g2_ref[...]) * (1.0 + sc2_ref[...]) + sh2_ref[...]


def _outproj(ret, lru, mla, x, w_out, g1, gt1, g2, sc2, sh2, bt, lt):
    b, l, d = x.shape
    tok = lambda w: pl.BlockSpec((bt, lt, w), lambda i, j: (i, j, 0))
    mod = pl.BlockSpec((bt, 1, d), lambda i, j: (i, 0, 0))
    gsp = pl.BlockSpec((1, 1, d), lambda i, j: (0, 0, 0))
    return pl.pallas_call(
        _outproj_kernel,
        grid=(b // bt, l // lt),
        in_specs=[tok(RET_WIDTH), tok(LRU_WIDTH), tok(MLA_WIDTH), tok(d),
                  pl.BlockSpec(w_out.shape, lambda i, j: (0, 0)), gsp, mod, gsp, mod, mod],
        out_specs=[tok(d), tok(d)],
        out_shape=[jax.ShapeDtypeStruct((b, l, d), F32)] * 2,
        compiler_params=_cparams(("arbitrary", "arbitrary")),
        name="outproj",
    )(ret, lru, mla, x, w_out, g1, gt1, g2, sc2, sh2)


def _split_dot_t(w, x):
    dn = (((1,), (1,)), ((), ()))
    wh = w.astype(BF16)
    wl = (w - wh.astype(F32)).astype(BF16)
    xh = x.astype(BF16)
    xl = (x - xh.astype(F32)).astype(BF16)
    return (lax.dot_general(wh, xh, dn, preferred_element_type=F32)
            + lax.dot_general(wh, xl, dn, preferred_element_type=F32)
            + lax.dot_general(wl, xh, dn, preferred_element_type=F32))


def _router_kernel(x_ref, w_ref, b_ref, upper_ref, lower_ref, eid_ref, pos_ref, gate_ref, cnt_ref, cnt_sc):
    ti = pl.program_id(0)
    tm = x_ref.shape[0]

    @pl.when(ti == 0)
    def _():
        cnt_sc[...] = jnp.zeros(cnt_sc.shape, F32)

    logits = _split_dot_t(w_ref[...], x_ref[...])
    scores = jax.nn.sigmoid(logits)
    biased = scores + b_ref[...]
    groups = [biased[g * GROUP_SIZE:(g + 1) * GROUP_SIZE, :] for g in range(N_GROUPS)]
    sub = lax.broadcasted_iota(jnp.int32, (GROUP_SIZE, tm), 0)
    gs = []
    for bg in groups:
        m1 = jnp.max(bg, axis=0, keepdims=True)
        i1 = jnp.min(jnp.where(bg == m1, sub, GROUP_SIZE), axis=0, keepdims=True)
        m2 = jnp.max(jnp.where(sub == i1, -jnp.inf, bg), axis=0, keepdims=True)
        gs.append(m1 + m2)
    masked = []
    for g in range(N_GROUPS):
        rank = jnp.zeros((1, tm), F32)
        for g2 in range(N_GROUPS):
            if g2 != g:
                ahead = (gs[g2] >= gs[g]) if g2 < g else (gs[g2] > gs[g])
                rank = rank + jnp.where(ahead, 1.0, 0.0)
        masked.append(jnp.where(rank < TOPK_GROUPS, groups[g], -jnp.inf))
    sels = []
    for g in range(N_GROUPS):
        mine = masked[g]
        rank = jnp.zeros((GROUP_SIZE, tm), F32)
        for g2 in range(N_GROUPS):
            for s2 in range(GROUP_SIZE):
                other = jnp.broadcast_to(masked[g2][s2:s2 + 1, :], (GROUP_SIZE, tm))
                if g2 < g:
                    rank = rank + jnp.where(other >= mine, 1.0, 0.0)
                elif g2 > g:
                    rank = rank + jnp.where(other > mine, 1.0, 0.0)
                else:
                    tie = jnp.where(sub > s2, 1.0, 0.0)
                    rank = rank + jnp.where(other > mine, 1.0, jnp.where(other == mine, tie, 0.0))
        sels.append(jnp.where(rank < TOP_K, 1.0, 0.0))
    self32 = jnp.concatenate(sels, axis=0)
    sel = self32 > 0.5
    picked = jnp.where(sel, scores, 0.0)
    gate = picked / jnp.sum(picked, axis=0, keepdims=True) * ROUTED_SCALE
    selb = self32.astype(BF16)
    pos = jnp.dot(selb, upper_ref[...], preferred_element_type=F32) + cnt_sc[:, :1]
    cnt_sc[...] = cnt_sc[...] + jnp.sum(self32, axis=1, keepdims=True)
    slot = jnp.dot(lower_ref[...], selb, preferred_element_type=F32)
    efl = lax.broadcasted_iota(jnp.int32, (N_EXPERTS, tm), 0).astype(F32)
    for kk in range(TOP_K):
        hit = sel & (slot == float(kk))
        eid_ref[kk:kk + 1, :] = jnp.sum(jnp.where(hit, efl, 0.0), axis=0, keepdims=True).astype(jnp.int32)
        pos_ref[kk:kk + 1, :] = jnp.sum(jnp.where(hit, pos, 0.0), axis=0, keepdims=True).astype(jnp.int32)
        gate_ref[kk:kk + 1, :] = jnp.sum(jnp.where(hit, gate, 0.0), axis=0, keepdims=True)

    @pl.when(ti == pl.num_programs(0) - 1)
    def _():
        cnt_ref[...] = cnt_sc[...]


def _router(h2, w_rt, b_r, tm):
    t, d = h2.shape
    upper = jnp.triu(jnp.ones((tm, tm), F32), 1).astype(BF16)
    lower = jnp.tril(jnp.ones((N_EXPERTS, N_EXPERTS), F32), -1).astype(BF16)
    kt = pl.BlockSpec((TOP_K, tm), lambda i: (0, i))
    cst = lambda a: pl.BlockSpec(a.shape, lambda i: (0,) * a.ndim)
    return pl.pallas_call(
        _router_kernel,
        grid=(t // tm,),
        in_specs=[pl.BlockSpec((tm, d), lambda i: (i, 0)), cst(w_rt), cst(b_r), cst(upper), cst(lower)],
        out_specs=[kt, kt, kt, pl.BlockSpec((N_EXPERTS, LANES), lambda i: (0, 0))],
        out_shape=[jax.ShapeDtypeStruct((TOP_K, t), jnp.int32), jax.ShapeDtypeStruct((TOP_K, t), jnp.int32),
                   jax.ShapeDtypeStruct((TOP_K, t), F32), jax.ShapeDtypeStruct((N_EXPERTS, LANES), F32)],
        scratch_shapes=[pltpu.VMEM((N_EXPERTS, LANES), F32)],
        compiler_params=_cparams(("arbitrary",)),
        name="router",
    )(h2, w_rt, b_r, upper, lower)


PACK_ROWS = 4


def _pack_rows(x):
    t, d = x.shape
    bits = lax.bitcast_convert_type(x.astype(BF16), jnp.uint16).astype(jnp.uint32)
    word = (bits[:, d // 2:] << 16) | bits[:, :d // 2]
    return lax.bitcast_convert_type(word, jnp.int32).reshape(t * PACK_ROWS, LANES)


def _unpack_words(w):
    lo = lax.bitcast_convert_type(lax.shift_left(w, jnp.int32(16)), F32)
    hi = lax.bitcast_convert_type(w & jnp.int32(-65536), F32)
    return lo, hi


def _dispatch_kernel(dest_ref, x_ref, rows_in_ref, rows_ref, sem):
    del rows_in_ref
    tm = x_ref.shape[0] // PACK_ROWS

    def row_copy(t, d):
        return pltpu.make_async_copy(x_ref.at[pl.ds(pl.multiple_of(t * PACK_ROWS, PACK_ROWS), PACK_ROWS), :],
                                     rows_ref.at[pl.ds(pl.multiple_of(d * PACK_ROWS, PACK_ROWS), PACK_ROWS), :], sem)

    def issue(t, c):
        for kk in range(TOP_K):
            row_copy(t, dest_ref[t * TOP_K + kk]).start()
        return c

    def drain(t, c):
        for kk in range(TOP_K):
            row_copy(0, 0).wait()
        return c

    lax.fori_loop(0, tm, issue, 0)
    lax.fori_loop(0, tm, drain, 0)


def _dispatch(dest_flat, h2p, n_rows, tm):
    t = h2p.shape[0] // PACK_ROWS
    rows0 = jnp.zeros((n_rows * PACK_ROWS, LANES), jnp.int32)
    return pl.pallas_call(
        _dispatch_kernel,
        grid=(t // tm,),
        in_specs=[pl.BlockSpec((tm * TOP_K,), lambda i: (i,), memory_space=pltpu.SMEM),
                  pl.BlockSpec((tm * PACK_ROWS, LANES), lambda i: (i, 0)),
                  pl.BlockSpec(memory_space=pl.ANY)],
        out_specs=pl.BlockSpec(memory_space=pl.ANY),
        out_shape=jax.ShapeDtypeStruct((n_rows * PACK_ROWS, LANES), jnp.int32),
        scratch_shapes=[pltpu.SemaphoreType.DMA(())],
        input_output_aliases={2: 0},
        compiler_params=_cparams(("arbitrary",)),
        name="moe_dispatch",
    )(dest_flat, h2p, rows0)


def _expert_kernel(be_ref, nu_ref, rows_ref, wgu_ref, wd_ref, o_ref, wgu_sc, wd_sc):
    i = pl.program_id(0)
    prev = be_ref[jnp.maximum(i - 1, 0)]
    fresh = (i == 0) | (be_ref[i] != prev)

    @pl.when(fresh)
    def _():
        wgu_sc[...] = wgu_ref[...].astype(BF16)
        wd_sc[...] = wd_ref[...].astype(BF16)

    @pl.when(i < nu_ref[0])
    def _():
        blk = rows_ref.shape[0] // PACK_ROWS
        halves = [_unpack_words(rows_ref[pl.ds(j, blk, stride=PACK_ROWS), :]) for j in range(PACK_ROWS)]
        x = jnp.concatenate([h[0] for h in halves] + [h[1] for h in halves], axis=1).astype(BF16)
        gu = jnp.dot(x, wgu_sc[...], preferred_element_type=F32)
        act = _silu(gu[:, :D_EXPERT]) * gu[:, D_EXPERT:]
        y = jnp.dot(act.astype(BF16), wd_sc[...], preferred_element_type=F32)
        bits = lax.bitcast_convert_type(y.astype(BF16).astype(F32), jnp.int32)
        for j in range(PACK_ROWS):
            lo = lax.shift_right_logical(bits[:, j * LANES:(j + 1) * LANES], jnp.int32(16))
            hi = bits[:, (j + PACK_ROWS) * LANES:(j + PACK_ROWS + 1) * LANES] & jnp.int32(-65536)
            o_ref[pl.ds(j, blk, stride=PACK_ROWS), :] = hi | lo


def _experts(block_e, n_used, rows, w_gu, w_down, li, blk):
    n_rows = rows.shape[0] // PACK_ROWS
    n_blocks = n_rows // blk
    d, de2 = w_gu.shape[-2:]
    de = w_down.shape[-2]

    def rmap(i, be, nu):
        return (jnp.minimum(i, nu[0] - 1), 0)

    grid_spec = pltpu.PrefetchScalarGridSpec(
        num_scalar_prefetch=2,
        grid=(n_blocks,),
        in_specs=[pl.BlockSpec((blk * PACK_ROWS, LANES), rmap),
                  pl.BlockSpec((None, None, d, de2), lambda i, be, nu: (li, be[i], 0, 0)),
                  pl.BlockSpec((None, None, de, d), lambda i, be, nu: (li, be[i], 0, 0))],
        out_specs=pl.BlockSpec((blk * PACK_ROWS, LANES), rmap),
        scratch_shapes=[pltpu.VMEM((d, de2), BF16), pltpu.VMEM((de, d), BF16)],
    )
    return pl.pallas_call(
        _expert_kernel,
        grid_spec=grid_spec,
        out_shape=jax.ShapeDtypeStruct(rows.shape, jnp.int32),
        compiler_params=_cparams(("arbitrary",)),
        name="moe_experts",
    )(block_e, n_used, rows, w_gu, w_down)


def _combine_kernel(dest_ref, dnext_ref, gate_ref, orow_ref, h2_ref, x1_ref, wsgu_ref, wsd_ref, g3_ref, gt2_ref,
                    y_ref, buf, sems):
    bt, lt, d = x1_ref.shape
    tm = bt * lt
    ns = PACK_ROWS
    step = pl.program_id(0) * pl.num_programs(1) + pl.program_id(1)
    nsteps = pl.num_programs(0) * pl.num_programs(1)
    slot = step % 2

    def row_copy(src_row, t, kk, sl):
        return pltpu.make_async_copy(orow_ref.at[pl.ds(pl.multiple_of(src_row * ns, ns), ns), :],
                                     buf.at[sl, kk, pl.ds(pl.multiple_of(t * ns, ns), ns), :],
                                     sems.at[sl])

    def issue_tile(dref, sl):
        def issue(t, c):
            for kk in range(TOP_K):
                row_copy(dref[t * TOP_K + kk], t, kk, sl).start()
            return c
        lax.fori_loop(0, tm, issue, 0)

    @pl.when(step == 0)
    def _():
        issue_tile(dest_ref, 0)

    @pl.when(step + 1 < nsteps)
    def _():
        issue_tile(dnext_ref, 1 - slot)

    hb = h2_ref[...].astype(BF16)
    gu = jnp.dot(hb, wsgu_ref[...], preferred_element_type=F32)
    ds = wsd_ref.shape[0]
    f = jnp.dot((_silu(gu[:, :ds]) * gu[:, ds:]).astype(BF16), wsd_ref[...], preferred_element_type=F32)

    def drain(t, c):
        for kk in range(TOP_K):
            row_copy(0, 0, 0, slot).wait()
        return c

    lax.fori_loop(0, tm, drain, 0)
    gate = gate_ref[...]
    fs = [f[:, j * LANES:(j + 1) * LANES] for j in range(2 * ns)]
    for kk in range(TOP_K):
        gk = jnp.broadcast_to(gate[:, kk:kk + 1], (tm, LANES))
        for j in range(ns):
            lo, hi = _unpack_words(buf[slot, kk, pl.ds(j, tm, stride=ns), :])
            fs[j] = fs[j] + lo * gk
            fs[j + ns] = fs[j + ns] + hi * gk
    f = jnp.concatenate(fs, axis=1)
    y_ref[...] = x1_ref[...] + gt2_ref[...] * _rms(f, g3_ref[...]).reshape(bt, lt, d)


def _combine(dest_flat, gate_tk, out_rows, h2, x1, ws_gu, ws_down, g3, gt2, bt, lt):
    b, l, d = x1.shape
    tm = bt * lt
    nl = l // lt
    nsteps = (b // bt) * nl
    tok = pl.BlockSpec((bt, lt, d), lambda i, j: (i, j, 0))
    flat = lambda w: pl.BlockSpec((tm, w), lambda i, j: (i * nl + j, 0))
    return pl.pallas_call(
        _combine_kernel,
        grid=(b // bt, nl),
        in_specs=[pl.BlockSpec((tm * TOP_K,), lambda i, j: (i * nl + j,), memory_space=pltpu.SMEM),
                  pl.BlockSpec((tm * TOP_K,), lambda i, j: (jnp.minimum(i * nl + j + 1, nsteps - 1),),
                               memory_space=pltpu.SMEM),
                  flat(TOP_K), pl.BlockSpec(memory_space=pl.ANY), flat(d), tok,
                  pl.BlockSpec(ws_gu.shape, lambda i, j: (0, 0)), pl.BlockSpec(ws_down.shape, lambda i, j: (0, 0)),
                  pl.BlockSpec((1, 1, d), lambda i, j: (0, 0, 0)),
                  pl.BlockSpec((bt, 1, d), lambda i, j: (i, 0, 0))],
        out_specs=tok,
        out_shape=jax.ShapeDtypeStruct((b, l, d), F32),
        scratch_shapes=[pltpu.VMEM((2, TOP_K, tm * PACK_ROWS, LANES), jnp.int32), pltpu.SemaphoreType.DMA((2,))],
        compiler_params=_cparams(("arbitrary", "arbitrary")),
        name="moe_combine",
    )(dest_flat, dest_flat, gate_tk, out_rows, h2, x1, ws_gu, ws_down, g3, gt2)


def _moe(h2, x1, gt2, g3, w_rt, b_r, w_exp_gu, w_exp_down, ws_gu, ws_down, li, bt, lt, blk):
    b, l, d = x1.shape
    t = b * l
    h2f = h2.reshape(t, d)
    tm_r = min(512, t)
    eid, pos, gate, cnt = _router(h2f, w_rt, b_r, tm_r)
    counts = cnt[:, 0].astype(jnp.int32)
    padded = (counts + blk - 1) // blk * blk
    pends = jnp.cumsum(padded)
    pstart = pends - padded
    n_blocks = -(-(t * TOP_K) // blk) + N_EXPERTS
    starts = jnp.arange(n_blocks, dtype=jnp.int32) * blk
    block_e = jnp.minimum(jnp.sum((pends[None, :] <= starts[:, None]).astype(jnp.int32), axis=1), N_EXPERTS - 1)
    n_used = (pends[-1:] // blk).astype(jnp.int32)
    first_row = jnp.sum(jnp.where(eid[:, :, None] == jnp.arange(N_EXPERTS, dtype=jnp.int32), pstart, 0), axis=-1)
    dest = (first_row + pos).T.reshape(t * TOP_K)
    gate_tk = gate.T
    tm_d = min(512, t)
    rows = _dispatch(dest, _pack_rows(h2f), n_blocks * blk, tm_d)
    out_rows = _experts(block_e, n_used, rows, w_exp_gu, w_exp_down, li, blk)
    return _combine(dest, gate_tk, out_rows, h2f, x1, ws_gu, ws_down, g3, gt2, bt, lt)


def _proj_perm():
    base = np.arange(C_QN)
    mq0 = C_QN
    qn = np.concatenate([mq0 + h * (MLA_NOPE + MLA_ROPE) + np.arange(MLA_NOPE) for h in range(MLA_HEADS)])
    qr = np.concatenate([mq0 + h * (MLA_NOPE + MLA_ROPE) + MLA_NOPE + np.arange(MLA_ROPE) for h in range(MLA_HEADS)])
    mc0 = mq0 + MLA_HEADS * (MLA_NOPE + MLA_ROPE)
    mc = mc0 + np.arange(KV_RANK)
    mk = np.tile(mc0 + KV_RANK + np.arange(MLA_ROPE), LANES // MLA_ROPE)
    return np.concatenate([base, qn, qr, mc, mk])


def _rope_tables(pos, dim):
    half = dim // 2
    inv = ROPE_BASE ** (-jnp.arange(0, dim, 2, dtype=F32) / dim)
    ang = pos.astype(F32)[:, None] * inv[None, :]
    cos, sin = jnp.cos(ang), jnp.sin(ang)
    reps = LANES // dim
    cos_t = jnp.tile(jnp.concatenate([cos, cos], axis=1), (1, reps))
    sin_t = jnp.tile(jnp.concatenate([-sin, sin], axis=1), (1, reps))
    return cos_t, sin_t


def _block_diag(blocks):
    n, a, b = blocks.shape
    eye = jnp.eye(n, dtype=blocks.dtype)
    return (eye[:, None, :, None] * blocks[:, :, None, :]).reshape(n * a, n * b)


def _layer(x, mod, pos_tabs, li, p, past, tiles):
    b, l, d = x.shape
    bt, lt = tiles['bt'], tiles['lt']
    sh1, sc1, gt1, sh2, sc2, gt2 = [m[:, None, :] for m in jnp.split(mod, 6, axis=-1)]
    g = p['norm_gains']
    gains = [g[i][None, None, :] for i in range(4)]
    gain = p['grp_gain']
    rq, rk, rv, rg, lx, lg, qcat, kcat, ckv, kr = _inproj(
        x, sc1, sh1, gains[0], p['w_in_p'], p['wuk_bd'], p['kv_norm'][None, :], pos_tabs, bt, lt, tiles['qdtype'])
    if past is None:
        r0 = jnp.zeros((b, RET_HEADS, RET_DK, RET_DV), F32)
        buf0 = jnp.zeros((b, CONV_W - 1, LRU_WIDTH), F32)
        h0 = jnp.zeros((b, 1, LRU_WIDTH), F32)
    else:
        r0, buf0, h0 = past['ret'], past['conv'], past['lru'][:, None, :]
    ret_out, r_new = _retention(rq, rk, rv, rg, r0, gain[None, :RET_WIDTH])
    lru_out, conv_new, h_new = _rglru(
        lx, lg, buf0, h0, p['conv_w'], p['conv_b'][None, :], p['wai'], p['bai'], p['lru_lambda'][None, :],
        gain[None, RET_WIDTH:RET_WIDTH + LRU_WIDTH], tiles['tc'])
    mla_gain = gain[None, RET_WIDTH + LRU_WIDTH:]
    if past is None:
        mla_out = _mla_prompt(qcat, kcat, p['wuv_exp'], mla_gain, tiles['qb'], tiles['kb'])
    else:
        mla_out = _mla_sample(qcat, kcat, past['cache_ckv'], past['cache_kr_t'], past['page_table'], li,
                              p['wuv_exp'], mla_gain)
    x1, h2 = _outproj(ret_out, lru_out, mla_out, x, p['w_out'], gains[1], gt1, gains[2], sc2, sh2, bt, lt)
    y = _moe(h2, x1, gt2, gains[3], p['w_rt'], p['b_router'][:, None], p['w_exp_gu'], p['w_exp_down'],
             p['ws_gu'], p['ws_down'], li, tiles['cbt'], tiles['clt'], tiles['blk'])
    return y, (ckv, kr, r_new, conv_new, h_new[:, 0])


def _largest_tile(n, cap):
    t = min(n, cap)
    while n % t:
        t //= 2
    return t


def kernel(x_prompt, x_sample, cache_ckv, cache_krope, state_ret, state_conv, state_lru, page_table, c_prompt, c_sample, w_ada, b_ada, norm_gains, w_in, w_out, grp_gain, conv_w, conv_b, w_lru_a, b_lru_a, w_lru_i, b_lru_i, lru_lambda, kv_norm, w_ukv, w_router, b_router, w_exp_gu, w_exp_down, w_sh_gu, w_sh_down):
    depth = w_in.shape[0]
    bp, lp, d = x_prompt.shape
    bs, ls, _ = x_sample.shape
    past_len = page_table.shape[1] * cache_ckv.shape[2]

    n_c = bp + bs
    n_cp = -(-n_c // SUBLANES) * SUBLANES
    c_all = jnp.concatenate([c_prompt, c_sample, jnp.zeros((n_cp - n_c, d), F32)], axis=0)
    mod_all = _adaln(c_all, w_ada, b_ada[:, None, :])

    w_in_p = w_in[:, :, _proj_perm()].astype(BF16)
    w_uk = w_ukv[..., :MLA_NOPE]
    w_uv = w_ukv[..., MLA_NOPE:]
    eye_h = jnp.eye(MLA_HEADS, dtype=F32)

    pos_p = jnp.arange(lp)
    pos_s = past_len + jnp.arange(ls)
    tabs_p = _rope_tables(pos_p, RET_DK) + _rope_tables(pos_p, MLA_ROPE)
    tabs_s = _rope_tables(pos_s, RET_DK) + _rope_tables(pos_s, MLA_ROPE)

    lt_p = _largest_tile(lp, 512)
    kb_p = _largest_tile(lp, 1024)
    tiles_p = dict(bt=1, lt=lt_p, qdtype=BF16, tc=_largest_tile(lp, 256), qb=_largest_tile(lp, min(256, kb_p // 2)),
                   kb=kb_p, cbt=1, clt=_largest_tile(lp, 128), blk=256 if bp * lp >= 4096 else 128)
    bt_s = _largest_tile(bs, 32)
    tiles_s = dict(bt=bt_s, lt=ls, qdtype=F32, tc=ls,
                   cbt=_largest_tile(bs, 16), clt=ls, blk=128)

    cache_kr_t = jnp.swapaxes(cache_krope, 2, 3)
    y_p, y_s = x_prompt, x_sample
    st_p, st_s = [], []
    for li in range(depth):
        wuk_bd = _block_diag(jnp.transpose(w_uk[li], (1, 2, 0))).astype(BF16)
        wuv_h = jnp.transpose(w_uv[li], (1, 0, 2))
        wuv_exp = (wuv_h[:, :, None, :] * eye_h[:, None, :, None]).reshape(
            MLA_HEADS, KV_RANK, MLA_WIDTH).astype(BF16)
        wai = jnp.concatenate([_block_diag(w_lru_a[li]), _block_diag(w_lru_i[li])], axis=1).astype(BF16)
        bai = jnp.concatenate([b_lru_a[li], b_lru_i[li]])[None, :]
        p = dict(norm_gains=norm_gains[li], w_in_p=w_in_p[li], wuk_bd=wuk_bd, wuv_exp=wuv_exp,
                 w_out=w_out[li].astype(BF16), grp_gain=grp_gain[li], conv_w=conv_w[li], conv_b=conv_b[li],
                 wai=wai, bai=bai, lru_lambda=lru_lambda[li], kv_norm=kv_norm[li],
                 w_rt=w_router[li].T, b_router=b_router[li], w_exp_gu=w_exp_gu, w_exp_down=w_exp_down,
                 ws_gu=w_sh_gu[li].astype(BF16), ws_down=w_sh_down[li].astype(BF16))
        y_p, sp = _layer(y_p, mod_all[li, :bp], tabs_p, li, p, None, tiles_p)
        past = dict(ret=state_ret[li], conv=state_conv[li], lru=state_lru[li], cache_ckv=cache_ckv,
                    cache_kr_t=cache_kr_t, page_table=page_table)
        y_s, ss = _layer(y_s, mod_all[li, bp:bp + bs], tabs_s, li, p, past, tiles_s)
        st_p.append(sp)
        st_s.append(ss)
    stack = lambda sts, i: jnp.stack([s[i] for s in sts])
    return (y_p, y_s) + tuple(stack(st_p, i) for i in range(5)) + tuple(stack(st_s, i) for i in range(5))
```

```python
import functools

import numpy as np
import jax
import jax.numpy as jnp
from jax import lax
from jax.experimental import pallas as pl
from jax.experimental.pallas import tpu as pltpu

F32 = jnp.float32
BF16 = jnp.bfloat16
EPS = 1e-6
ROPE_BASE = 10000.0

RET_HEADS = 4
RET_DK = 64
RET_DV = 64
RET_WIDTH = RET_HEADS * RET_DV
RET_CHUNK = 128
LRU_WIDTH = 256
LRU_BLOCKS = 4
LRU_BLOCK = LRU_WIDTH // LRU_BLOCKS
CONV_W = 4
LRU_C = 8.0
MLA_HEADS = 8
MLA_NOPE = 64
MLA_ROPE = 32
MLA_V = 64
KV_RANK = 128
MLA_WIDTH = MLA_HEADS * MLA_V
N_EXPERTS = 64
TOP_K = 8
N_GROUPS = 8
GROUP_SIZE = N_EXPERTS // N_GROUPS
TOPK_GROUPS = 4
D_EXPERT = 256
ROUTED_SCALE = 2.5

LANES = 128
SUBLANES = 8
QCAT = 2 * LANES
C_RQ, C_RK, C_RV, C_RG, C_LX, C_LG = 0, 256, 512, 768, 1024, 1280
C_QN = 1536
C_QR = C_QN + MLA_HEADS * MLA_NOPE
C_MC = C_QR + MLA_HEADS * MLA_ROPE
C_MK = C_MC + KV_RANK
N_PROJ = C_MK + LANES


def _cparams(sem, vmem_mb=48):
    return pltpu.CompilerParams(dimension_semantics=sem, vmem_limit_bytes=vmem_mb * 2 ** 20)


def _rms(x, g):
    return x * lax.rsqrt(jnp.mean(x * x, axis=-1, keepdims=True) + EPS) * g


def _silu(x):
    return x * jax.nn.sigmoid(x)


def _adaln_kernel(c_ref, w_ref, b_ref, o_ref):
    c = c_ref[...]
    s = _silu(c).astype(BF16)
    o_ref[...] = jnp.dot(s, w_ref[...].astype(BF16), preferred_element_type=F32) + b_ref[...]


def _adaln(c, w_ada, b_ada):
    depth, d, n = w_ada.shape
    bp = c.shape[0]
    tn = 768
    return pl.pallas_call(
        _adaln_kernel,
        grid=(depth, n // tn),
        in_specs=[pl.BlockSpec((bp, d), lambda l, j: (0, 0)),
                  pl.BlockSpec((None, d, tn), lambda l, j: (l, 0, j)),
                  pl.BlockSpec((None, 1, tn), lambda l, j: (l, 0, j))],
        out_specs=pl.BlockSpec((None, bp, tn), lambda l, j: (l, 0, j)),
        out_shape=jax.ShapeDtypeStruct((depth, bp, n), F32),
        compiler_params=_cparams(("arbitrary", "arbitrary")),
        name="adaln",
    )(c, w_ada, b_ada)


def _rope_slab(v, cos, sin_signed, half):
    lane = lax.broadcasted_iota(jnp.int32, v.shape, 1)
    first = (lane % (2 * half)) < half
    partner = jnp.where(first, pltpu.roll(v, LANES - half, 1), pltpu.roll(v, half, 1))
    return v * cos + partner * sin_signed


def _rope(v, cos, sin_signed, half):
    n = v.shape[1] // LANES
    outs = [_rope_slab(v[:, i * LANES:(i + 1) * LANES], cos, sin_signed, half) for i in range(n)]
    return outs[0] if n == 1 else jnp.concatenate(outs, axis=1)


def _inproj_kernel(x_ref, sc_ref, sh_ref, g_ref, w_ref, wuk_ref, kvg_ref, c64_ref, s64_ref, c32_ref, s32_ref,
                   rq_ref, rk_ref, rv_ref, rg_ref, lx_ref, lg_ref, qcat_ref, kcat_ref, ckv_ref, kr_ref):
    bt, lt, d = x_ref.shape
    rows = bt * lt
    h = _rms(x_ref[...], g_ref[...]) * (1.0 + sc_ref[...]) + sh_ref[...]
    hb = h.reshape(rows, d).astype(BF16)

    def proj(a, b):
        return jnp.dot(hb, w_ref[:, a:b], preferred_element_type=F32)

    def table(ref):
        t = ref[...]
        if bt > 1:
            t = jnp.broadcast_to(t[None], (bt, lt, LANES)).reshape(rows, LANES)
        return t

    c64, s64, c32, s32 = table(c64_ref), table(s64_ref), table(c32_ref), table(s32_ref)
    rq_ref[...] = _rope(proj(C_RQ, C_RK), c64, s64, RET_DK // 2).reshape(bt, lt, RET_WIDTH)
    rk_ref[...] = _rope(proj(C_RK, C_RV), c64, s64, RET_DK // 2).reshape(bt, lt, RET_WIDTH)
    rv_ref[...] = proj(C_RV, C_RG).reshape(bt, lt, RET_WIDTH)
    rg_ref[...] = proj(C_RG, C_LX).reshape(bt, lt, RET_WIDTH)
    lx_ref[...] = proj(C_LX, C_LG).reshape(bt, lt, LRU_WIDTH)
    lg_ref[...] = proj(C_LG, C_QN).reshape(bt, lt, LRU_WIDTH)

    scale = (MLA_NOPE + MLA_ROPE) ** -0.5
    q_lat = jnp.dot(proj(C_QN, C_QR).astype(BF16), wuk_ref[...], preferred_element_type=F32) * scale
    q_rope = _rope(proj(C_QR, C_MC), c32, s32, MLA_ROPE // 2) * scale
    lane = lax.broadcasted_iota(jnp.int32, (rows, LANES), 1)
    heads_per_slab = LANES // MLA_ROPE
    for hd in range(MLA_HEADS):
        slab = q_rope[:, (hd // heads_per_slab) * LANES:(hd // heads_per_slab + 1) * LANES]
        own = jnp.where(lane // MLA_ROPE == hd % heads_per_slab, slab, 0.0)
        qh = jnp.concatenate([q_lat[:, hd * KV_RANK:(hd + 1) * KV_RANK], own], axis=1)
        qcat_ref[:, hd, :, :] = qh.reshape(bt, lt, QCAT).astype(qcat_ref.dtype)

    ckv = _rms(proj(C_MC, C_MK), kvg_ref[...])
    kr4 = _rope(proj(C_MK, N_PROJ), c32, s32, MLA_ROPE // 2)
    ckv_ref[...] = ckv.reshape(bt, lt, KV_RANK)
    kr_ref[...] = kr4[:, :MLA_ROPE].reshape(bt, lt, MLA_ROPE)
    kcat_ref[...] = jnp.concatenate([ckv, kr4], axis=1).reshape(bt, lt, QCAT).astype(kcat_ref.dtype)


def _inproj(x, sc, sh, g, w_in_p, wuk_bd, kv_gain, tabs, bt, lt, qdtype):
    b, l, d = x.shape
    grid = (b // bt, l // lt)
    c64, s64, c32, s32 = tabs
    tok = lambda w: pl.BlockSpec((bt, lt, w), lambda i, j: (i, j, 0))
    mod = pl.BlockSpec((bt, 1, d), lambda i, j: (i, 0, 0))
    const2 = lambda a: pl.BlockSpec(a.shape, lambda i, j: (0, 0))
    tab = pl.BlockSpec((lt, LANES), lambda i, j: (j, 0))
    out_shape = [jax.ShapeDtypeStruct((b, l, 256), F32)] * 6 + [
        jax.ShapeDtypeStruct((b, MLA_HEADS, l, QCAT), qdtype),
        jax.ShapeDtypeStruct((b, l, QCAT), qdtype),
        jax.ShapeDtypeStruct((b, l, KV_RANK), F32),
        jax.ShapeDtypeStruct((b, l, MLA_ROPE), F32)]
    out_specs = [tok(256)] * 6 + [
        pl.BlockSpec((bt, MLA_HEADS, lt, QCAT), lambda i, j: (i, 0, j, 0)),
        tok(QCAT), tok(KV_RANK), tok(MLA_ROPE)]
    return pl.pallas_call(
        _inproj_kernel,
        grid=grid,
        in_specs=[tok(d), mod, mod, pl.BlockSpec((1, 1, d), lambda i, j: (0, 0, 0)),
                  const2(w_in_p), const2(wuk_bd), const2(kv_gain), tab, tab, tab, tab],
        out_specs=out_specs,
        out_shape=out_shape,
        compiler_params=_cparams(("arbitrary", "arbitrary")),
        name="inproj",
    )(x, sc, sh, g, w_in_p, wuk_bd, kv_gain, c64, s64, c32, s32)


def _retention_kernel(q_ref, k_ref, v_ref, g_ref, r0_ref, dmat_ref, xi_ref, zeta_ref, dec_ref, gain_ref,
                      o_ref, rfin_ref, r_sc):
    ci = pl.program_id(1)

    @pl.when(ci == 0)
    def _():
        r_sc[...] = r0_ref[0]

    q = q_ref[0]
    k = k_ref[0] * (RET_DK ** -0.5)
    v = v_ref[0]
    outs = []
    for hd in range(RET_HEADS):
        sl = slice(hd * RET_DK, (hd + 1) * RET_DK)
        qh = q[:, sl].astype(BF16)
        kh = k[:, sl]
        vh = v[:, sl].astype(BF16)
        s = lax.dot_general(qh, kh.astype(BF16), (((1,), (1,)), ((), ())), preferred_element_type=F32)
        inner = jnp.dot((s * dmat_ref[hd]).astype(BF16), vh, preferred_element_type=F32)
        r = r_sc[hd]
        cross = jnp.dot(qh, r.astype(BF16), preferred_element_type=F32) * xi_ref[:, hd:hd + 1]
        o = inner + cross
        mu = jnp.mean(o, axis=-1, keepdims=True)
        oc = o - mu
        var = jnp.mean(oc * oc, axis=-1, keepdims=True)
        outs.append(oc * lax.rsqrt(var + EPS))
        kz = (kh * zeta_ref[:, hd:hd + 1]).astype(BF16)
        kv = lax.dot_general(kz, vh, (((0,), (0,)), ((), ())), preferred_element_type=F32)
        r_sc[hd] = dec_ref[hd] * r + kv
    o_ref[0] = jnp.concatenate(outs, axis=1) * gain_ref[...] * _silu(g_ref[0])

    @pl.when(ci == pl.num_programs(1) - 1)
    def _():
        rfin_ref[0] = r_sc[...]


def _retention_consts(c):
    log_g = jnp.log1p(-jnp.exp2(-5.0 - jnp.arange(RET_HEADS, dtype=F32)))
    i = jnp.arange(c, dtype=F32)
    diff = i[:, None] - i[None, :]
    dmat = jnp.where(diff >= 0, jnp.exp(jnp.maximum(diff, 0.0)[None] * log_g[:, None, None]), 0.0)
    xi = jnp.exp((i[:, None] + 1.0) * log_g[None, :])
    zeta = jnp.exp((c - 1.0 - i)[:, None] * log_g[None, :])
    dec = jnp.broadcast_to(jnp.exp(c * log_g)[:, None, None], (RET_HEADS, 1, RET_DV))
    return dmat, xi, zeta, dec


def _retention(rq, rk, rv, rg, r0, gain):
    b, l, _ = rq.shape
    c = RET_CHUNK if l % RET_CHUNK == 0 else l
    n = l // c
    dmat, xi, zeta, dec = _retention_consts(c)
    tok = pl.BlockSpec((1, c, RET_WIDTH), lambda i, j: (i, j, 0))
    st = pl.BlockSpec((1, RET_HEADS, RET_DK, RET_DV), lambda i, j: (i, 0, 0, 0))
    cst = lambda a: pl.BlockSpec(a.shape, lambda i, j: (0,) * a.ndim)
    return pl.pallas_call(
        _retention_kernel,
        grid=(b, n),
        in_specs=[tok, tok, tok, tok, st, cst(dmat), cst(xi), cst(zeta), cst(dec), cst(gain)],
        out_specs=[tok, st],
        out_shape=[jax.ShapeDtypeStruct((b, l, RET_WIDTH), F32),
                   jax.ShapeDtypeStruct((b, RET_HEADS, RET_DK, RET_DV), F32)],
        scratch_shapes=[pltpu.VMEM((RET_HEADS, RET_DK, RET_DV), F32)],
        compiler_params=_cparams(("arbitrary", "arbitrary")),
        name="retention",
    )(rq, rk, rv, rg, r0, dmat, xi, zeta, dec, gain)


def _rglru_kernel(x_ref, gate_ref, buf_ref, h0_ref, cw_ref, cb_ref, wai_ref, bai_ref, lam_ref, gain_ref,
                  o_ref, conv_ref, hl_ref, xin_sc, a_sc, b_sc, h_sc, hc_sc):
    ci = pl.program_id(1)
    tc = x_ref.shape[1]
    pad = SUBLANES - (CONV_W - 1)

    @pl.when(ci == 0)
    def _():
        xin_sc[pad:SUBLANES, :] = buf_ref[0]
        hc_sc[...] = h0_ref[0]

    xin_sc[SUBLANES:SUBLANES + tc, :] = x_ref[0]
    xc = cb_ref[...]
    for kk in range(CONV_W):
        xc = xc + xin_sc[pad + kk:pad + kk + tc, :] * cw_ref[kk:kk + 1, :]
    tail = xin_sc[SUBLANES + tc - (CONV_W - 1):SUBLANES + tc, :]
    xin_sc[pad:SUBLANES, :] = tail

    z = jnp.dot(xc.astype(BF16), wai_ref[...], preferred_element_type=F32) + bai_ref[...]
    r = jax.nn.sigmoid(z[:, :LRU_WIDTH])
    ig = jax.nn.sigmoid(z[:, LRU_WIDTH:])
    log_a = -LRU_C * r * jax.nn.softplus(-lam_ref[...])
    a = jnp.exp(log_a)
    a_sc[...] = a
    b_sc[...] = jnp.sqrt(-jnp.tanh(log_a) * (a * a + 1.0)) * ig * xc

    def body(t, h):
        h = a_sc[pl.ds(t, 1), :] * h + b_sc[pl.ds(t, 1), :]
        h_sc[pl.ds(t, 1), :] = h
        return h

    h_last = lax.fori_loop(0, tc, body, hc_sc[...], unroll=8)
    hc_sc[...] = h_last
    y = h_sc[...] * jax.nn.gelu(gate_ref[0])
    o_ref[0] = _rms(y, gain_ref[...])

    @pl.when(ci == pl.num_programs(1) - 1)
    def _():
        conv_ref[0] = tail
        hl_ref[0] = h_last


def _rglru(lx, lg, buf0, h0, conv_w, conv_b, wai, bai, lam, gain, tc):
    b, l, w = lx.shape
    tok = pl.BlockSpec((1, tc, w), lambda i, j: (i, j, 0))
    cst = lambda a: pl.BlockSpec(a.shape, lambda i, j: (0,) * a.ndim)
    bufs = pl.BlockSpec((1, CONV_W - 1, w), lambda i, j: (i, 0, 0))
    hs = pl.BlockSpec((1, 1, w), lambda i, j: (i, 0, 0))
    return pl.pallas_call(
        _rglru_kernel,
        grid=(b, l // tc),
        in_specs=[tok, tok, bufs, hs, cst(conv_w), cst(conv_b), cst(wai), cst(bai), cst(lam), cst(gain)],
        out_specs=[tok, bufs, hs],
        out_shape=[jax.ShapeDtypeStruct((b, l, w), F32),
                   jax.ShapeDtypeStruct((b, CONV_W - 1, w), F32),
                   jax.ShapeDtypeStruct((b, 1, w), F32)],
        scratch_shapes=[pltpu.VMEM((SUBLANES + tc, w), F32), pltpu.VMEM((tc, w), F32), pltpu.VMEM((tc, w), F32),
                        pltpu.VMEM((tc, w), F32), pltpu.VMEM((1, w), F32)],
        compiler_params=_cparams(("arbitrary", "arbitrary")),
        name="rglru",
    )(lx, lg, buf0, h0, conv_w, conv_b, wai, bai, lam, gain)


def _softmax_update(s, v, m_sc, l_sc, acc_sc, rows=slice(None)):
    n = s.shape[1]
    m_prev = m_sc[rows, :]
    m_new = jnp.maximum(m_prev, jnp.max(s, axis=-1, keepdims=True))
    corr = jnp.exp(m_prev - m_new)
    p = jnp.exp(s - jnp.concatenate([m_new] * (n // LANES), axis=1))
    l_sc[rows, :] = l_sc[rows, :] * corr + jnp.sum(p, axis=-1, keepdims=True)
    acc_sc[rows, :] = acc_sc[rows, :] * corr + jnp.dot(p.astype(BF16), v, preferred_element_type=F32)
    m_sc[rows, :] = m_new


def _attn_finish(o, wuv_ref, gain_ref, rows_per_head):
    o = o.astype(BF16)
    om = None
    for hd in range(MLA_HEADS):
        part = jnp.dot(o[hd * rows_per_head:(hd + 1) * rows_per_head], wuv_ref[hd], preferred_element_type=F32)
        om = part if om is None else om + part
    return _rms(om, gain_ref[...])


def _flash_kernel(q_ref, k_ref, wuv_ref, gain_ref, o_ref, m_sc, l_sc, acc_sc, *, qb, kb):
    qi = pl.program_id(1)
    r = MLA_HEADS * qb
    m_sc[...] = jnp.full((r, LANES), -jnp.inf, F32)
    l_sc[...] = jnp.zeros((r, LANES), F32)
    acc_sc[...] = jnp.zeros((r, LANES), F32)
    q0 = qi * qb
    nfull = q0 // kb
    halves = [slice(i * (r // 2), (i + 1) * (r // 2)) for i in range(2)]
    q = q_ref[0].reshape(r, QCAT)
    qs = [q[h] for h in halves]

    def step(start, width, masked):
        kblk = k_ref[0, pl.ds(start, width), :]
        v = kblk[:, :KV_RANK]
        for h, qh in zip(halves, qs):
            s = lax.dot_general(qh, kblk, (((1,), (1,)), ((), ())), preferred_element_type=F32)
            if masked:
                row = lax.broadcasted_iota(jnp.int32, s.shape, 0)
                col = lax.broadcasted_iota(jnp.int32, s.shape, 1)
                s = jnp.where(start + col <= q0 + row % qb, s, -jnp.inf)
            _softmax_update(s, v, m_sc, l_sc, acc_sc, h)

    def body(j, carry):
        step(pl.multiple_of(j * kb, kb), kb, False)
        return carry

    lax.fori_loop(0, nfull, body, 0)
    half = kb // 2
    base = pl.multiple_of(nfull * kb, kb)
    upper = (q0 - base) >= half

    @pl.when(upper)
    def _():
        step(base, half, False)

    step(pl.multiple_of(base + jnp.where(upper, half, 0), half), half, True)
    o_ref[0] = _attn_finish(acc_sc[...] / l_sc[...], wuv_ref, gain_ref, qb)


def _mla_prompt(qcat, kcat, wuv_exp, gain, qb, kb):
    b, _, s, _ = qcat.shape
    r = MLA_HEADS * qb
    return pl.pallas_call(
        functools.partial(_flash_kernel, qb=qb, kb=kb),
        grid=(b, s // qb),
        in_specs=[pl.BlockSpec((1, MLA_HEADS, qb, QCAT), lambda i, j: (i, 0, j, 0)),
                  pl.BlockSpec((1, s, QCAT), lambda i, j: (i, 0, 0)),
                  pl.BlockSpec(wuv_exp.shape, lambda i, j: (0, 0, 0)),
                  pl.BlockSpec(gain.shape, lambda i, j: (0, 0))],
        out_specs=pl.BlockSpec((1, qb, MLA_WIDTH), lambda i, j: (i, j, 0)),
        out_shape=jax.ShapeDtypeStruct((b, s, MLA_WIDTH), F32),
        scratch_shapes=[pltpu.VMEM((r, LANES), F32)] * 3,
        compiler_params=_cparams(("arbitrary", "arbitrary")),
        name="mla_prompt",
    )(qcat, kcat, wuv_exp, gain)


def _mla_sample_kernel(pt_ref, q_ref, knew_ref, ck_hbm, kp_hbm, wuv_ref, gain_ref, o_ref, ckbuf, kpbuf, sems,
                       *, li, n_pages, page, t_new, chunk):
    bi = pl.program_id(0)
    nb = pl.num_programs(0)
    r = MLA_HEADS * t_new

    def page_copies(pg_of, slot):
        out = []
        for j in range(n_pages):
            pg = pg_of(j)
            out.append(pltpu.make_async_copy(ck_hbm.at[li, pg], ckbuf.at[slot, pl.ds(j * page, page), :],
                                             sems.at[0, slot]))
            out.append(pltpu.make_async_copy(kp_hbm.at[li, pg], kpbuf.at[slot, :, pl.ds(j * page, page)],
                                             sems.at[1, slot]))
        return out

    @pl.when(bi == 0)
    def _():
        for c in page_copies(lambda j: pt_ref[0, j], 0):
            c.start()

    @pl.when(bi + 1 < nb)
    def _():
        for c in page_copies(lambda j: pt_ref[bi + 1, j], (bi + 1) % 2):
            c.start()

    slot = bi % 2
    q = q_ref[0].reshape(r, QCAT)
    qb = q.astype(BF16)
    q_lat = qb[:, :KV_RANK]
    q_rot = q[:, KV_RANK:].astype(F32)
    q_r = q_rot[:, :MLA_ROPE]
    for i in range(1, LANES // MLA_ROPE):
        q_r = q_r + q_rot[:, i * MLA_ROPE:(i + 1) * MLA_ROPE]
    q_r = q_r.astype(BF16)

    kn = knew_ref[0].astype(BF16)
    s = lax.dot_general(qb, kn, (((1,), (1,)), ((), ())), preferred_element_type=F32)
    row = lax.broadcasted_iota(jnp.int32, s.shape, 0)
    col = lax.broadcasted_iota(jnp.int32, s.shape, 1)
    s = jnp.where(row % t_new >= col, s, -jnp.inf)
    m = jnp.max(s, axis=-1, keepdims=True)
    p = jnp.exp(s - m)
    l = jnp.sum(p, axis=-1, keepdims=True)
    acc = jnp.dot(p.astype(BF16), kn[:, :KV_RANK], preferred_element_type=F32)

    for c in page_copies(lambda j: 0, slot):
        c.wait()

    for c in range(n_pages * page // chunk):
        ck = ckbuf[slot, pl.ds(c * chunk, chunk), :].astype(BF16)
        kp = kpbuf[slot, :, pl.ds(c * chunk, chunk)].astype(BF16)
        s = (lax.dot_general(q_lat, ck, (((1,), (1,)), ((), ())), preferred_element_type=F32)
             + jnp.dot(q_r, kp, preferred_element_type=F32))
        m_new = jnp.maximum(m, jnp.max(s, axis=-1, keepdims=True))
        corr = jnp.exp(m - m_new)
        p = jnp.exp(s - m_new)
        l = l * corr + jnp.sum(p, axis=-1, keepdims=True)
        acc = acc * corr + jnp.dot(p.astype(BF16), ck, preferred_element_type=F32)
        m = m_new
    o_ref[0] = _attn_finish(acc / l, wuv_ref, gain_ref, t_new)


def _mla_sample(qcat, kcat, cache_ckv, cache_kr_t, page_table, li, wuv_exp, gain):
    b, _, t, _ = qcat.shape
    n_pages = page_table.shape[1]
    page, rank = cache_ckv.shape[2], cache_ckv.shape[3]
    rope_w = cache_kr_t.shape[2]
    chunk = min(8192, n_pages * page)
    grid_spec = pltpu.PrefetchScalarGridSpec(
        num_scalar_prefetch=1,
        grid=(b,),
        in_specs=[pl.BlockSpec((1, MLA_HEADS, t, QCAT), lambda bi, pt: (bi, 0, 0, 0)),
                  pl.BlockSpec((1, t, QCAT), lambda bi, pt: (bi, 0, 0)),
                  pl.BlockSpec(memory_space=pl.ANY),
                  pl.BlockSpec(memory_space=pl.ANY),
                  pl.BlockSpec(wuv_exp.shape, lambda bi, pt: (0, 0, 0)),
                  pl.BlockSpec(gain.shape, lambda bi, pt: (0, 0))],
        out_specs=pl.BlockSpec((1, t, MLA_WIDTH), lambda bi, pt: (bi, 0, 0)),
        scratch_shapes=[pltpu.VMEM((2, n_pages * page, rank), F32),
                        pltpu.VMEM((2, rope_w, n_pages * page), F32),
                        pltpu.SemaphoreType.DMA((2, 2))],
    )
    return pl.pallas_call(
        functools.partial(_mla_sample_kernel, li=li, n_pages=n_pages, page=page, t_new=t, chunk=chunk),
        grid_spec=grid_spec,
        out_shape=jax.ShapeDtypeStruct((b, t, MLA_WIDTH), F32),
        compiler_params=_cparams(("arbitrary",)),
        name="mla_sample",
    )(page_table, qcat, kcat, cache_ckv, cache_kr_t, wuv_exp, gain)


def _outproj_kernel(ret_ref, lru_ref, mla_ref, x_ref, w_ref, g1_ref, gt1_ref, g2_ref, sc2_ref, sh2_ref,
                    x1_ref, h2_ref):
    bt, lt, d = x_ref.shape
    rows = bt * lt
    mix = jnp.concatenate([ret_ref[...].reshape(rows, RET_WIDTH), lru_ref[...].reshape(rows, LRU_WIDTH),
                           mla_ref[...].reshape(rows, MLA_WIDTH)], axis=1).astype(BF16)
    y = jnp.dot(mix, w_ref[...], preferred_element_type=F32).reshape(bt, lt, d)
    x1 = x_ref[...] + gt1_ref[...] * _rms(y, g1_ref[...])
    x1_ref[...] = x1
    h2_ref[...] = _rms(x1, g2_ref[...]) * (1.0 + sc2_ref[...]) + sh2_ref[...]


def _outproj(ret, lru, mla, x, w_out, g1, gt1, g2, sc2, sh2, bt, lt):
    b, l, d = x.shape
    tok = lambda w: pl.BlockSpec((bt, lt, w), lambda i, j: (i, j, 0))
    mod = pl.BlockSpec((bt, 1, d), lambda i, j: (i, 0, 0))
    gsp = pl.BlockSpec((1, 1, d), lambda i, j: (0, 0, 0))
    return pl.pallas_call(
        _outproj_kernel,
        grid=(b // bt, l // lt),
        in_specs=[tok(RET_WIDTH), tok(LRU_WIDTH), tok(MLA_WIDTH), tok(d),
                  pl.BlockSpec(w_out.shape, lambda i, j: (0, 0)), gsp, mod, gsp, mod, mod],
        out_specs=[tok(d), tok(d)],
        out_shape=[jax.ShapeDtypeStruct((b, l, d), F32)] * 2,
        compiler_params=_cparams(("arbitrary", "arbitrary")),
        name="outproj",
    )(ret, lru, mla, x, w_out, g1, gt1, g2, sc2, sh2)


def _split_dot_t(w, x):
    dn = (((1,), (1,)), ((), ()))
    wh = w.astype(BF16)
    wl = (w - wh.astype(F32)).astype(BF16)
    xh = x.astype(BF16)
    xl = (x - xh.astype(F32)).astype(BF16)
    return (lax.dot_general(wh, xh, dn, preferred_element_type=F32)
            + lax.dot_general(wh, xl, dn, preferred_element_type=F32)
            + lax.dot_general(wl, xh, dn, preferred_element_type=F32))


def _router_kernel(x_ref, w_ref, b_ref, upper_ref, lower_ref, eid_ref, pos_ref, gate_ref, cnt_ref, cnt_sc):
    ti = pl.program_id(0)
    tm = x_ref.shape[0]

    @pl.when(ti == 0)
    def _():
        cnt_sc[...] = jnp.zeros(cnt_sc.shape, F32)

    logits = _split_dot_t(w_ref[...], x_ref[...])
    scores = jax.nn.sigmoid(logits)
    biased = scores + b_ref[...]
    groups = [biased[g * GROUP_SIZE:(g + 1) * GROUP_SIZE, :] for g in range(N_GROUPS)]
    sub = lax.broadcasted_iota(jnp.int32, (GROUP_SIZE, tm), 0)
    gs = []
    for bg in groups:
        m1 = jnp.max(bg, axis=0, keepdims=True)
        i1 = jnp.min(jnp.where(bg == m1, sub, GROUP_SIZE), axis=0, keepdims=True)
        m2 = jnp.max(jnp.where(sub == i1, -jnp.inf, bg), axis=0, keepdims=True)
        gs.append(m1 + m2)
    masked = []
    for g in range(N_GROUPS):
        rank = jnp.zeros((1, tm), F32)
        for g2 in range(N_GROUPS):
            if g2 != g:
                ahead = (gs[g2] >= gs[g]) if g2 < g else (gs[g2] > gs[g])
                rank = rank + jnp.where(ahead, 1.0, 0.0)
        masked.append(jnp.where(rank < TOPK_GROUPS, groups[g], -jnp.inf))
    sels = []
    for g in range(N_GROUPS):
        mine = masked[g]
        rank = jnp.zeros((GROUP_SIZE, tm), F32)
        for g2 in range(N_GROUPS):
            for s2 in range(GROUP_SIZE):
                other = jnp.broadcast_to(masked[g2][s2:s2 + 1, :], (GROUP_SIZE, tm))
                if g2 < g:
                    rank = rank + jnp.where(other >= mine, 1.0, 0.0)
                elif g2 > g:
                    rank = rank + jnp.where(other > mine, 1.0, 0.0)
                else:
                    tie = jnp.where(sub > s2, 1.0, 0.0)
                    rank = rank + jnp.where(other > mine, 1.0, jnp.where(other == mine, tie, 0.0))
        sels.append(jnp.where(rank < TOP_K, 1.0, 0.0))
    self32 = jnp.concatenate(sels, axis=0)
    sel = self32 > 0.5
    picked = jnp.where(sel, scores, 0.0)
    gate = picked / jnp.sum(picked, axis=0, keepdims=True) * ROUTED_SCALE
    selb = self32.astype(BF16)
    pos = jnp.dot(selb, upper_ref[...], preferred_element_type=F32) + cnt_sc[:, :1]
    cnt_sc[...] = cnt_sc[...] + jnp.sum(self32, axis=1, keepdims=True)
    slot = jnp.dot(lower_ref[...], selb, preferred_element_type=F32)
    efl = lax.broadcasted_iota(jnp.int32, (N_EXPERTS, tm), 0).astype(F32)
    for kk in range(TOP_K):
        hit = sel & (slot == float(kk))
        eid_ref[kk:kk + 1, :] = jnp.sum(jnp.where(hit, efl, 0.0), axis=0, keepdims=True).astype(jnp.int32)
        pos_ref[kk:kk + 1, :] = jnp.sum(jnp.where(hit, pos, 0.0), axis=0, keepdims=True).astype(jnp.int32)
        gate_ref[kk:kk + 1, :] = jnp.sum(jnp.where(hit, gate, 0.0), axis=0, keepdims=True)

    @pl.when(ti == pl.num_programs(0) - 1)
    def _():
        cnt_ref[...] = cnt_sc[...]


def _router(h2, w_rt, b_r, tm):
    t, d = h2.shape
    upper = jnp.triu(jnp.ones((tm, tm), F32), 1).astype(BF16)
    lower = jnp.tril(jnp.ones((N_EXPERTS, N_EXPERTS), F32), -1).astype(BF16)
    kt = pl.BlockSpec((TOP_K, tm), lambda i: (0, i))
    cst = lambda a: pl.BlockSpec(a.shape, lambda i: (0,) * a.ndim)
    return pl.pallas_call(
        _router_kernel,
        grid=(t // tm,),
        in_specs=[pl.BlockSpec((tm, d), lambda i: (i, 0)), cst(w_rt), cst(b_r), cst(upper), cst(lower)],
        out_specs=[kt, kt, kt, pl.BlockSpec((N_EXPERTS, LANES), lambda i: (0, 0))],
        out_shape=[jax.ShapeDtypeStruct((TOP_K, t), jnp.int32), jax.ShapeDtypeStruct((TOP_K, t), jnp.int32),
                   jax.ShapeDtypeStruct((TOP_K, t), F32), jax.ShapeDtypeStruct((N_EXPERTS, LANES), F32)],
        scratch_shapes=[pltpu.VMEM((N_EXPERTS, LANES), F32)],
        compiler_params=_cparams(("arbitrary",)),
        name="router",
    )(h2, w_rt, b_r, upper, lower)


PACK_ROWS = 4


def _pack_rows(x):
    t, d = x.shape
    bits = lax.bitcast_convert_type(x.astype(BF16), jnp.uint16).astype(jnp.uint32)
    word = (bits[:, d // 2:] << 16) | bits[:, :d // 2]
    return lax.bitcast_convert_type(word, jnp.int32).reshape(t * PACK_ROWS, LANES)


def _unpack_words(w):
    lo = lax.bitcast_convert_type(lax.shift_left(w, jnp.int32(16)), F32)
    hi = lax.bitcast_convert_type(w & jnp.int32(-65536), F32)
    return lo, hi


def _dispatch_kernel(dest_ref, x_ref, rows_in_ref, rows_ref, sem):
    del rows_in_ref
    tm = x_ref.shape[0] // PACK_ROWS

    def row_copy(t, d):
        return pltpu.make_async_copy(x_ref.at[pl.ds(pl.multiple_of(t * PACK_ROWS, PACK_ROWS), PACK_ROWS), :],
                                     rows_ref.at[pl.ds(pl.multiple_of(d * PACK_ROWS, PACK_ROWS), PACK_ROWS), :], sem)

    def issue(t, c):
        for kk in range(TOP_K):
            row_copy(t, dest_ref[t * TOP_K + kk]).start()
        return c

    def drain(t, c):
        for kk in range(TOP_K):
            row_copy(0, 0).wait()
        return c

    lax.fori_loop(0, tm, issue, 0)
    lax.fori_loop(0, tm, drain, 0)


def _dispatch(dest_flat, h2p, rows0, tm):
    t = h2p.shape[0] // PACK_ROWS
    return pl.pallas_call(
        _dispatch_kernel,
        grid=(t // tm,),
        in_specs=[pl.BlockSpec((tm * TOP_K,), lambda i: (i,), memory_space=pltpu.SMEM),
                  pl.BlockSpec((tm * PACK_ROWS, LANES), lambda i: (i, 0)),
                  pl.BlockSpec(memory_space=pl.ANY)],
        out_specs=pl.BlockSpec(memory_space=pl.ANY),
        out_shape=jax.ShapeDtypeStruct(rows0.shape, jnp.int32),
        scratch_shapes=[pltpu.SemaphoreType.DMA(())],
        input_output_aliases={2: 0},
        compiler_params=_cparams(("arbitrary",)),
        name="moe_dispatch",
    )(dest_flat, h2p, rows0)


def _expert_kernel(be_ref, nu_ref, rows_ref, wgu_ref, wd_ref, o_ref, wgu_sc, wd_sc):
    i = pl.program_id(0)
    prev = be_ref[jnp.maximum(i - 1, 0)]
    fresh = (i == 0) | (be_ref[i] != prev)

    @pl.when(fresh)
    def _():
        wgu_sc[...] = wgu_ref[...].astype(BF16)
        wd_sc[...] = wd_ref[...].astype(BF16)

    @pl.when(i < nu_ref[0])
    def _():
        blk = rows_ref.shape[0] // PACK_ROWS
        halves = [_unpack_words(rows_ref[pl.ds(j, blk, stride=PACK_ROWS), :]) for j in range(PACK_ROWS)]
        x = jnp.concatenate([h[0] for h in halves] + [h[1] for h in halves], axis=1).astype(BF16)
        gu = jnp.dot(x, wgu_sc[...], preferred_element_type=F32)
        act = _silu(gu[:, :D_EXPERT]) * gu[:, D_EXPERT:]
        y = jnp.dot(act.astype(BF16), wd_sc[...], preferred_element_type=F32)
        bits = lax.bitcast_convert_type(y.astype(BF16).astype(F32), jnp.int32)
        for j in range(PACK_ROWS):
            lo = lax.shift_right_logical(bits[:, j * LANES:(j + 1) * LANES], jnp.int32(16))
            hi = bits[:, (j + PACK_ROWS) * LANES:(j + PACK_ROWS + 1) * LANES] & jnp.int32(-65536)
            o_ref[pl.ds(j, blk, stride=PACK_ROWS), :] = hi | lo


def _experts(block_e, n_used, rows, w_gu, w_down, li, blk):
    n_rows = rows.shape[0] // PACK_ROWS
    n_blocks = n_rows // blk
    d, de2 = w_gu.shape[-2:]
    de = w_down.shape[-2]

    def rmap(i, be, nu):
        return (jnp.minimum(i, nu[0] - 1), 0)

    grid_spec = pltpu.PrefetchScalarGridSpec(
        num_scalar_prefetch=2,
        grid=(n_blocks,),
        in_specs=[pl.BlockSpec((blk * PACK_ROWS, LANES), rmap),
                  pl.BlockSpec((None, None, d, de2), lambda i, be, nu: (li, be[i], 0, 0)),
                  pl.BlockSpec((None, None, de, d), lambda i, be, nu: (li, be[i], 0, 0))],
        out_specs=pl.BlockSpec((blk * PACK_ROWS, LANES), rmap),
        scratch_shapes=[pltpu.VMEM((d, de2), BF16), pltpu.VMEM((de, d), BF16)],
    )
    return pl.pallas_call(
        _expert_kernel,
        grid_spec=grid_spec,
        out_shape=jax.ShapeDtypeStruct(rows.shape, jnp.int32),
        compiler_params=_cparams(("arbitrary",)),
        name="moe_experts",
    )(block_e, n_used, rows, w_gu, w_down)


def _combine_kernel(dest_ref, dnext_ref, gate_ref, orow_ref, h2_ref, x1_ref, wsgu_ref, wsd_ref, g3_ref, gt2_ref,
                    y_ref, buf, sems):
    bt, lt, d = x1_ref.shape
    tm = bt * lt
    ns = PACK_ROWS
    step = pl.program_id(0) * pl.num_programs(1) + pl.program_id(1)
    nsteps = pl.num_programs(0) * pl.num_programs(1)
    slot = step % 2

    def row_copy(src_row, t, kk, sl):
        return pltpu.make_async_copy(orow_ref.at[pl.ds(pl.multiple_of(src_row * ns, ns), ns), :],
                                     buf.at[sl, kk, pl.ds(pl.multiple_of(t * ns, ns), ns), :],
                                     sems.at[sl])

    def issue_tile(dref, sl):
        def issue(t, c):
            for kk in range(TOP_K):
                row_copy(dref[t * TOP_K + kk], t, kk, sl).start()
            return c
        lax.fori_loop(0, tm, issue, 0)

    @pl.when(step == 0)
    def _():
        issue_tile(dest_ref, 0)

    @pl.when(step + 1 < nsteps)
    def _():
        issue_tile(dnext_ref, 1 - slot)

    hb = h2_ref[...].astype(BF16)
    gu = jnp.dot(hb, wsgu_ref[...], preferred_element_type=F32)
    ds = wsd_ref.shape[0]
    f = jnp.dot((_silu(gu[:, :ds]) * gu[:, ds:]).astype(BF16), wsd_ref[...], preferred_element_type=F32)

    def drain(t, c):
        for kk in range(TOP_K):
            row_copy(0, 0, 0, slot).wait()
        return c

    lax.fori_loop(0, tm, drain, 0)
    gate = gate_ref[...]
    fs = [f[:, j * LANES:(j + 1) * LANES] for j in range(2 * ns)]
    for kk in range(TOP_K):
        gk = jnp.broadcast_to(gate[:, kk:kk + 1], (tm, LANES))
        for j in range(ns):
            lo, hi = _unpack_words(buf[slot, kk, pl.ds(j, tm, stride=ns), :])
            fs[j] = fs[j] + lo * gk
            fs[j + ns] = fs[j + ns] + hi * gk
    f = jnp.concatenate(fs, axis=1)
    y_ref[...] = x1_ref[...] + gt2_ref[...] * _rms(f, g3_ref[...]).reshape(bt, lt, d)


def _combine(dest_flat, gate_tk, out_rows, h2, x1, ws_gu, ws_down, g3, gt2, bt, lt):
    b, l, d = x1.shape
    tm = bt * lt
    nl = l // lt
    nsteps = (b // bt) * nl
    tok = pl.BlockSpec((bt, lt, d), lambda i, j: (i, j, 0))
    flat = lambda w: pl.BlockSpec((tm, w), lambda i, j: (i * nl + j, 0))
    return pl.pallas_call(
        _combine_kernel,
        grid=(b // bt, nl),
        in_specs=[pl.BlockSpec((tm * TOP_K,), lambda i, j: (i * nl + j,), memory_space=pltpu.SMEM),
                  pl.BlockSpec((tm * TOP_K,), lambda i, j: (jnp.minimum(i * nl + j + 1, nsteps - 1),),
                               memory_space=pltpu.SMEM),
                  flat(TOP_K), pl.BlockSpec(memory_space=pl.ANY), flat(d), tok,
                  pl.BlockSpec(ws_gu.shape, lambda i, j: (0, 0)), pl.BlockSpec(ws_down.shape, lambda i, j: (0, 0)),
                  pl.BlockSpec((1, 1, d), lambda i, j: (0, 0, 0)),
                  pl.BlockSpec((bt, 1, d), lambda i, j: (i, 0, 0))],
        out_specs=tok,
        out_shape=jax.ShapeDtypeStruct((b, l, d), F32),
        scratch_shapes=[pltpu.VMEM((2, TOP_K, tm * PACK_ROWS, LANES), jnp.int32), pltpu.SemaphoreType.DMA((2,))],
        compiler_params=_cparams(("arbitrary", "arbitrary")),
        name="moe_combine",
    )(dest_flat, dest_flat, gate_tk, out_rows, h2, x1, ws_gu, ws_down, g3, gt2)


def _moe(groups, g3, w_rt, b_r, w_exp_gu, w_exp_down, ws_gu, ws_down, li, blk, rows_init):
    d = groups[0][0].shape[-1]
    sizes = [g[0].shape[0] * g[0].shape[1] for g in groups]
    t = sum(sizes)
    flats = [g[0].reshape(n, d) for g, n in zip(groups, sizes)]
    h2f = flats[0] if len(flats) == 1 else jnp.concatenate(flats, axis=0)
    eid, pos, gate, cnt = _router(h2f, w_rt, b_r, _largest_tile(t, 512))
    counts = cnt[:, 0].astype(jnp.int32)
    padded = (counts + blk - 1) // blk * blk
    pends = jnp.cumsum(padded)
    pstart = pends - padded
    n_blocks = -(-(t * TOP_K) // blk) + N_EXPERTS
    starts = jnp.arange(n_blocks, dtype=jnp.int32) * blk
    block_e = jnp.minimum(jnp.sum((pends[None, :] <= starts[:, None]).astype(jnp.int32), axis=1), N_EXPERTS - 1)
    n_used = (pends[-1:] // blk).astype(jnp.int32)
    first_row = jnp.sum(jnp.where(eid[:, :, None] == jnp.arange(N_EXPERTS, dtype=jnp.int32), pstart, 0), axis=-1)
    dest = (first_row + pos).T.reshape(t * TOP_K)
    gate_tk = gate.T
    if rows_init is None:
        rows_init = jnp.zeros((n_blocks * blk * PACK_ROWS, LANES), jnp.int32)
    rows = _dispatch(dest, _pack_rows(h2f), rows_init, _largest_tile(t, 512))
    out_rows = _experts(block_e, n_used, rows, w_exp_gu, w_exp_down, li, blk)
    outs, off = [], 0
    for (h2, x1, gt2, bt, lt), n, flat in zip(groups, sizes, flats):
        outs.append(_combine(dest[off * TOP_K:(off + n) * TOP_K], gate_tk[off:off + n], out_rows, flat, x1,
                             ws_gu, ws_down, g3, gt2, bt, lt))
        off += n
    return outs, rows


def _permute_w_in(w_in):
    dp, d, _ = w_in.shape
    hd = MLA_NOPE + MLA_ROPE
    mq = w_in[:, :, C_QN:C_QN + MLA_HEADS * hd].reshape(dp, d, MLA_HEADS, hd)
    qn = mq[..., :MLA_NOPE].reshape(dp, d, MLA_HEADS * MLA_NOPE)
    qr = mq[..., MLA_NOPE:].reshape(dp, d, MLA_HEADS * MLA_ROPE)
    mc0 = C_QN + MLA_HEADS * hd
    mc = w_in[:, :, mc0:mc0 + KV_RANK]
    mk = w_in[:, :, mc0 + KV_RANK:mc0 + KV_RANK + MLA_ROPE]
    return jnp.concatenate([w_in[:, :, :C_QN], qn, qr, mc] + [mk] * (LANES // MLA_ROPE), axis=-1).astype(BF16)


def _rope_tables(pos, dim):
    half = dim // 2
    inv = ROPE_BASE ** (-jnp.arange(0, dim, 2, dtype=F32) / dim)
    ang = pos.astype(F32)[:, None] * inv[None, :]
    cos, sin = jnp.cos(ang), jnp.sin(ang)
    reps = LANES // dim
    cos_t = jnp.tile(jnp.concatenate([cos, cos], axis=1), (1, reps))
    sin_t = jnp.tile(jnp.concatenate([-sin, sin], axis=1), (1, reps))
    return cos_t, sin_t


def _block_diag(blocks):
    n, a, b = blocks.shape
    eye = jnp.eye(n, dtype=blocks.dtype)
    return (eye[:, None, :, None] * blocks[:, :, None, :]).reshape(n * a, n * b)


def _mixer(x, mod, pos_tabs, li, p, past, tiles):
    b, l, d = x.shape
    bt, lt = tiles['bt'], tiles['lt']
    sh1, sc1, gt1, sh2, sc2, gt2 = [m[:, None, :] for m in jnp.split(mod, 6, axis=-1)]
    g = p['norm_gains']
    gains = [g[i][None, None, :] for i in range(4)]
    gain = p['grp_gain']
    rq, rk, rv, rg, lx, lg, qcat, kcat, ckv, kr = _inproj(
        x, sc1, sh1, gains[0], p['w_in_p'], p['wuk_bd'], p['kv_norm'][None, :], pos_tabs, bt, lt, tiles['qdtype'])
    if past is None:
        r0 = jnp.zeros((b, RET_HEADS, RET_DK, RET_DV), F32)
        buf0 = jnp.zeros((b, CONV_W - 1, LRU_WIDTH), F32)
        h0 = jnp.zeros((b, 1, LRU_WIDTH), F32)
    else:
        r0, buf0, h0 = past['ret'], past['conv'], past['lru'][:, None, :]
    ret_out, r_new = _retention(rq, rk, rv, rg, r0, gain[None, :RET_WIDTH])
    lru_out, conv_new, h_new = _rglru(
        lx, lg, buf0, h0, p['conv_w'], p['conv_b'][None, :], p['wai'], p['bai'], p['lru_lambda'][None, :],
        gain[None, RET_WIDTH:RET_WIDTH + LRU_WIDTH], tiles['tc'])
    mla_gain = gain[None, RET_WIDTH + LRU_WIDTH:]
    if past is None:
        mla_out = _mla_prompt(qcat, kcat, p['wuv_exp'], mla_gain, tiles['qb'], tiles['kb'])
    else:
        mla_out = _mla_sample(qcat, kcat, past['cache_ckv'], past['cache_kr_t'], past['page_table'], li,
                              p['wuv_exp'], mla_gain)
    x1, h2 = _outproj(ret_out, lru_out, mla_out, x, p['w_out'], gains[1], gt1, gains[2], sc2, sh2, bt, lt)
    return (h2, x1, gt2, tiles['cbt'], tiles['clt']), (ckv, kr, r_new, conv_new, h_new[:, 0])


def _largest_tile(n, cap):
    t = min(n, cap)
    while n % t:
        t //= 2
    return t


def kernel(x_prompt, x_sample, cache_ckv, cache_krope, state_ret, state_conv, state_lru, page_table, c_prompt, c_sample, w_ada, b_ada, norm_gains, w_in, w_out, grp_gain, conv_w, conv_b, w_lru_a, b_lru_a, w_lru_i, b_lru_i, lru_lambda, kv_norm, w_ukv, w_router, b_router, w_exp_gu, w_exp_down, w_sh_gu, w_sh_down):
    depth = w_in.shape[0]
    bp, lp, d = x_prompt.shape
    bs, ls, _ = x_sample.shape
    past_len = page_table.shape[1] * cache_ckv.shape[2]

    n_c = bp + bs
    n_cp = -(-n_c // SUBLANES) * SUBLANES
    c_all = jnp.concatenate([c_prompt, c_sample, jnp.zeros((n_cp - n_c, d), F32)], axis=0)
    mod_all = _adaln(c_all, w_ada, b_ada[:, None, :])

    w_in_p = _permute_w_in(w_in)
    w_uk = w_ukv[..., :MLA_NOPE]
    w_uv = w_ukv[..., MLA_NOPE:]
    eye_h = jnp.eye(MLA_HEADS, dtype=F32)

    pos_p = jnp.arange(lp)
    pos_s = past_len + jnp.arange(ls)
    tabs_p = _rope_tables(pos_p, RET_DK) + _rope_tables(pos_p, MLA_ROPE)
    tabs_s = _rope_tables(pos_s, RET_DK) + _rope_tables(pos_s, MLA_ROPE)

    lt_p = _largest_tile(lp, 512)
    kb_p = _largest_tile(lp, 1024)
    tiles_p = dict(bt=1, lt=lt_p, qdtype=BF16, tc=_largest_tile(lp, 256), qb=_largest_tile(lp, min(256, kb_p // 2)),
                   kb=kb_p, cbt=1, clt=_largest_tile(lp, 128))
    bt_s = _largest_tile(bs, 32)
    tiles_s = dict(bt=bt_s, lt=ls, qdtype=F32, tc=ls,
                   cbt=_largest_tile(bs, 16), clt=ls)

    cache_kr_t = jnp.swapaxes(cache_krope, 2, 3)
    y_p, y_s = x_prompt, x_sample
    blk = 256 if bp * lp + bs * ls >= 4096 else 128
    moe_rows = None
    st_p, st_s = [], []
    for li in range(depth):
        wuk_bd = _block_diag(jnp.transpose(w_uk[li], (1, 2, 0))).astype(BF16)
        wuv_h = jnp.transpose(w_uv[li], (1, 0, 2))
        wuv_exp = (wuv_h[:, :, None, :] * eye_h[:, None, :, None]).reshape(
            MLA_HEADS, KV_RANK, MLA_WIDTH).astype(BF16)
        wai = jnp.concatenate([_block_diag(w_lru_a[li]), _block_diag(w_lru_i[li])], axis=1).astype(BF16)
        bai = jnp.concatenate([b_lru_a[li], b_lru_i[li]])[None, :]
        p = dict(norm_gains=norm_gains[li], w_in_p=w_in_p[li], wuk_bd=wuk_bd, wuv_exp=wuv_exp,
                 w_out=w_out[li].astype(BF16), grp_gain=grp_gain[li], conv_w=conv_w[li], conv_b=conv_b[li],
                 wai=wai, bai=bai, lru_lambda=lru_lambda[li], kv_norm=kv_norm[li],
                 w_rt=w_router[li].T, b_router=b_router[li], w_exp_gu=w_exp_gu, w_exp_down=w_exp_down,
                 ws_gu=w_sh_gu[li].astype(BF16), ws_down=w_sh_down[li].astype(BF16))
        grp_p, sp = _mixer(y_p, mod_all[li, :bp], tabs_p, li, p, None, tiles_p)
        past = dict(ret=state_ret[li], conv=state_conv[li], lru=state_lru[li], cache_ckv=cache_ckv,
                    cache_kr_t=cache_kr_t, page_table=page_table)
        grp_s, ss = _mixer(y_s, mod_all[li, bp:bp + bs], tabs_s, li, p, past, tiles_s)
        (y_p, y_s), moe_rows = _moe([grp_p, grp_s], norm_gains[li][3][None, None, :], p['w_rt'], p['b_router'][:, None],
                                    p['w_exp_gu'], p['w_exp_down'], p['ws_gu'], p['ws_down'], li, blk, moe_rows)
        st_p.append(sp)
        st_s.append(ss)
    stack = lambda sts, i: jnp.stack([s[i] for s in sts])
    return (y_p, y_s) + tuple(stack(st_p, i) for i in range(5)) + tuple(stack(st_s, i) for i in range(5))
```

```python
import functools

import numpy as np
import jax
import jax.numpy as jnp
from jax import lax
from jax.experimental import pallas as pl
from jax.experimental.pallas import tpu as pltpu

F32 = jnp.float32
BF16 = jnp.bfloat16
EPS = 1e-6
ROPE_BASE = 10000.0

RET_HEADS = 4
RET_DK = 64
RET_DV = 64
RET_WIDTH = RET_HEADS * RET_DV
RET_CHUNK = 128
LRU_WIDTH = 256
LRU_BLOCKS = 4
LRU_BLOCK = LRU_WIDTH // LRU_BLOCKS
CONV_W = 4
LRU_C = 8.0
MLA_HEADS = 8
MLA_NOPE = 64
MLA_ROPE = 32
MLA_V = 64
KV_RANK = 128
MLA_WIDTH = MLA_HEADS * MLA_V
N_EXPERTS = 64
TOP_K = 8
N_GROUPS = 8
GROUP_SIZE = N_EXPERTS // N_GROUPS
TOPK_GROUPS = 4
D_EXPERT = 256
ROUTED_SCALE = 2.5

LANES = 128
SUBLANES = 8
QCAT = 2 * LANES
C_RQ, C_RK, C_RV, C_RG, C_LX, C_LG = 0, 256, 512, 768, 1024, 1280
C_QN = 1536
C_QR = C_QN + MLA_HEADS * MLA_NOPE
C_MC = C_QR + MLA_HEADS * MLA_ROPE
C_MK = C_MC + KV_RANK
N_PROJ = C_MK + LANES


def _cparams(sem, vmem_mb=48):
    return pltpu.CompilerParams(dimension_semantics=sem, vmem_limit_bytes=vmem_mb * 2 ** 20)


def _rms(x, g):
    return x * lax.rsqrt(jnp.mean(x * x, axis=-1, keepdims=True) + EPS) * g


def _silu(x):
    return x * jax.nn.sigmoid(x)


def _adaln_kernel(c_ref, w_ref, b_ref, o_ref):
    c = c_ref[...]
    s = _silu(c).astype(BF16)
    o_ref[...] = jnp.dot(s, w_ref[...].astype(BF16), preferred_element_type=F32) + b_ref[...]


def _adaln(c, w_ada, b_ada):
    depth, d, n = w_ada.shape
    bp = c.shape[0]
    tn = 768
    return pl.pallas_call(
        _adaln_kernel,
        grid=(depth, n // tn),
        in_specs=[pl.BlockSpec((bp, d), lambda l, j: (0, 0)),
                  pl.BlockSpec((None, d, tn), lambda l, j: (l, 0, j)),
                  pl.BlockSpec((None, 1, tn), lambda l, j: (l, 0, j))],
        out_specs=pl.BlockSpec((None, bp, tn), lambda l, j: (l, 0, j)),
        out_shape=jax.ShapeDtypeStruct((depth, bp, n), F32),
        compiler_params=_cparams(("arbitrary", "arbitrary")),
        name="adaln",
    )(c, w_ada, b_ada)


def _rope_slab(v, cos, sin_signed, half):
    lane = lax.broadcasted_iota(jnp.int32, v.shape, 1)
    first = (lane % (2 * half)) < half
    partner = jnp.where(first, pltpu.roll(v, LANES - half, 1), pltpu.roll(v, half, 1))
    return v * cos + partner * sin_signed


def _rope(v, cos, sin_signed, half):
    n = v.shape[1] // LANES
    outs = [_rope_slab(v[:, i * LANES:(i + 1) * LANES], cos, sin_signed, half) for i in range(n)]
    return outs[0] if n == 1 else jnp.concatenate(outs, axis=1)


def _inproj_kernel(x_ref, sc_ref, sh_ref, g_ref, w_ref, wuk_ref, kvg_ref, c64_ref, s64_ref, c32_ref, s32_ref,
                   rq_ref, rk_ref, rv_ref, rg_ref, lx_ref, lg_ref, qcat_ref, kcat_ref, ckv_ref, kr_ref):
    bt, lt, d = x_ref.shape
    rows = bt * lt
    h = _rms(x_ref[...], g_ref[...]) * (1.0 + sc_ref[...]) + sh_ref[...]
    hb = h.reshape(rows, d).astype(BF16)

    def proj(a, b):
        return jnp.dot(hb, w_ref[:, a:b], preferred_element_type=F32)

    def table(ref):
        t = ref[...]
        if bt > 1:
            t = jnp.broadcast_to(t[None], (bt, lt, LANES)).reshape(rows, LANES)
        return t

    c64, s64, c32, s32 = table(c64_ref), table(s64_ref), table(c32_ref), table(s32_ref)
    rq_ref[...] = _rope(proj(C_RQ, C_RK), c64, s64, RET_DK // 2).reshape(bt, lt, RET_WIDTH)
    rk_ref[...] = _rope(proj(C_RK, C_RV), c64, s64, RET_DK // 2).reshape(bt, lt, RET_WIDTH)
    rv_ref[...] = proj(C_RV, C_RG).reshape(bt, lt, RET_WIDTH)
    rg_ref[...] = proj(C_RG, C_LX).reshape(bt, lt, RET_WIDTH)
    lx_ref[...] = proj(C_LX, C_LG).reshape(bt, lt, LRU_WIDTH)
    lg_ref[...] = proj(C_LG, C_QN).reshape(bt, lt, LRU_WIDTH)

    scale = (MLA_NOPE + MLA_ROPE) ** -0.5
    q_lat = jnp.dot(proj(C_QN, C_QR).astype(BF16), wuk_ref[...], preferred_element_type=F32) * scale
    q_rope = _rope(proj(C_QR, C_MC), c32, s32, MLA_ROPE // 2) * scale
    lane = lax.broadcasted_iota(jnp.int32, (rows, LANES), 1)
    heads_per_slab = LANES // MLA_ROPE
    for hd in range(MLA_HEADS):
        slab = q_rope[:, (hd // heads_per_slab) * LANES:(hd // heads_per_slab + 1) * LANES]
        own = jnp.where(lane // MLA_ROPE == hd % heads_per_slab, slab, 0.0)
        qh = jnp.concatenate([q_lat[:, hd * KV_RANK:(hd + 1) * KV_RANK], own], axis=1)
        qcat_ref[:, hd, :, :] = qh.reshape(bt, lt, QCAT).astype(qcat_ref.dtype)

    ckv = _rms(proj(C_MC, C_MK), kvg_ref[...])
    kr4 = _rope(proj(C_MK, N_PROJ), c32, s32, MLA_ROPE // 2)
    ckv_ref[...] = ckv.reshape(bt, lt, KV_RANK)
    kr_ref[...] = kr4[:, :MLA_ROPE].reshape(bt, lt, MLA_ROPE)
    kcat_ref[...] = jnp.concatenate([ckv, kr4], axis=1).reshape(bt, lt, QCAT).astype(kcat_ref.dtype)


def _inproj(x, sc, sh, g, w_in_p, wuk_bd, kv_gain, tabs, bt, lt, qdtype):
    b, l, d = x.shape
    grid = (b // bt, l // lt)
    c64, s64, c32, s32 = tabs
    tok = lambda w: pl.BlockSpec((bt, lt, w), lambda i, j: (i, j, 0))
    mod = pl.BlockSpec((bt, 1, d), lambda i, j: (i, 0, 0))
    const2 = lambda a: pl.BlockSpec(a.shape, lambda i, j: (0, 0))
    tab = pl.BlockSpec((lt, LANES), lambda i, j: (j, 0))
    out_shape = [jax.ShapeDtypeStruct((b, l, 256), F32)] * 6 + [
        jax.ShapeDtypeStruct((b, MLA_HEADS, l, QCAT), qdtype),
        jax.ShapeDtypeStruct((b, l, QCAT), qdtype),
        jax.ShapeDtypeStruct((b, l, KV_RANK), F32),
        jax.ShapeDtypeStruct((b, l, MLA_ROPE), F32)]
    out_specs = [tok(256)] * 6 + [
        pl.BlockSpec((bt, MLA_HEADS, lt, QCAT), lambda i, j: (i, 0, j, 0)),
        tok(QCAT), tok(KV_RANK), tok(MLA_ROPE)]
    return pl.pallas_call(
        _inproj_kernel,
        grid=grid,
        in_specs=[tok(d), mod, mod, pl.BlockSpec((1, 1, d), lambda i, j: (0, 0, 0)),
                  const2(w_in_p), const2(wuk_bd), const2(kv_gain), tab, tab, tab, tab],
        out_specs=out_specs,
        out_shape=out_shape,
        compiler_params=_cparams(("arbitrary", "arbitrary")),
        name="inproj",
    )(x, sc, sh, g, w_in_p, wuk_bd, kv_gain, c64, s64, c32, s32)


def _retention_kernel(q_ref, k_ref, v_ref, g_ref, r0_ref, dmat_ref, xi_ref, zeta_ref, dec_ref, gain_ref,
                      o_ref, rfin_ref, r_sc):
    ci = pl.program_id(1)

    @pl.when(ci == 0)
    def _():
        r_sc[...] = r0_ref[...]

    for bi in range(q_ref.shape[0]):
        q = q_ref[bi]
        k = k_ref[bi] * (RET_DK ** -0.5)
        v = v_ref[bi]
        outs = []
        for hd in range(RET_HEADS):
            sl = slice(hd * RET_DK, (hd + 1) * RET_DK)
            qh = q[:, sl].astype(BF16)
            kh = k[:, sl]
            vh = v[:, sl].astype(BF16)
            s = lax.dot_general(qh, kh.astype(BF16), (((1,), (1,)), ((), ())), preferred_element_type=F32)
            inner = jnp.dot((s * dmat_ref[hd]).astype(BF16), vh, preferred_element_type=F32)
            r = r_sc[bi, hd]
            cross = jnp.dot(qh, r.astype(BF16), preferred_element_type=F32) * xi_ref[:, hd:hd + 1]
            o = inner + cross
            mu = jnp.mean(o, axis=-1, keepdims=True)
            oc = o - mu
            var = jnp.mean(oc * oc, axis=-1, keepdims=True)
            outs.append(oc * lax.rsqrt(var + EPS))
            kz = (kh * zeta_ref[:, hd:hd + 1]).astype(BF16)
            kv = lax.dot_general(kz, vh, (((0,), (0,)), ((), ())), preferred_element_type=F32)
            r_sc[bi, hd] = dec_ref[hd] * r + kv
        o_ref[bi] = jnp.concatenate(outs, axis=1) * gain_ref[...] * _silu(g_ref[bi])

    @pl.when(ci == pl.num_programs(1) - 1)
    def _():
        rfin_ref[...] = r_sc[...]


def _retention_consts(c):
    log_g = jnp.log1p(-jnp.exp2(-5.0 - jnp.arange(RET_HEADS, dtype=F32)))
    i = jnp.arange(c, dtype=F32)
    diff = i[:, None] - i[None, :]
    dmat = jnp.where(diff >= 0, jnp.exp(jnp.maximum(diff, 0.0)[None] * log_g[:, None, None]), 0.0)
    xi = jnp.exp((i[:, None] + 1.0) * log_g[None, :])
    zeta = jnp.exp((c - 1.0 - i)[:, None] * log_g[None, :])
    dec = jnp.broadcast_to(jnp.exp(c * log_g)[:, None, None], (RET_HEADS, 1, RET_DV))
    return dmat, xi, zeta, dec


def _retention(rq, rk, rv, rg, r0, gain, bb):
    b, l, _ = rq.shape
    c = RET_CHUNK if l % RET_CHUNK == 0 else l
    n = l // c
    dmat, xi, zeta, dec = _retention_consts(c)
    tok = pl.BlockSpec((bb, c, RET_WIDTH), lambda i, j: (i, j, 0))
    st = pl.BlockSpec((bb, RET_HEADS, RET_DK, RET_DV), lambda i, j: (i, 0, 0, 0))
    cst = lambda a: pl.BlockSpec(a.shape, lambda i, j: (0,) * a.ndim)
    return pl.pallas_call(
        _retention_kernel,
        grid=(b // bb, n),
        in_specs=[tok, tok, tok, tok, st, cst(dmat), cst(xi), cst(zeta), cst(dec), cst(gain)],
        out_specs=[tok, st],
        out_shape=[jax.ShapeDtypeStruct((b, l, RET_WIDTH), F32),
                   jax.ShapeDtypeStruct((b, RET_HEADS, RET_DK, RET_DV), F32)],
        scratch_shapes=[pltpu.VMEM((bb, RET_HEADS, RET_DK, RET_DV), F32)],
        compiler_params=_cparams(("arbitrary", "arbitrary")),
        name="retention",
    )(rq, rk, rv, rg, r0, dmat, xi, zeta, dec, gain)


def _rglru_kernel(x_ref, gate_ref, buf_ref, h0_ref, cw_ref, cb_ref, wai_ref, bai_ref, lam_ref, gain_ref,
                  o_ref, conv_ref, hl_ref, xin_sc, a_sc, b_sc, h_sc, hc_sc):
    ci = pl.program_id(1)
    tc = x_ref.shape[1]
    pad = SUBLANES - (CONV_W - 1)

    @pl.when(ci == 0)
    def _():
        xin_sc[pad:SUBLANES, :] = buf_ref[0]
        hc_sc[...] = h0_ref[0]

    xin_sc[SUBLANES:SUBLANES + tc, :] = x_ref[0]
    xc = cb_ref[...]
    for kk in range(CONV_W):
        xc = xc + xin_sc[pad + kk:pad + kk + tc, :] * cw_ref[kk:kk + 1, :]
    tail = xin_sc[SUBLANES + tc - (CONV_W - 1):SUBLANES + tc, :]
    xin_sc[pad:SUBLANES, :] = tail

    z = jnp.dot(xc.astype(BF16), wai_ref[...], preferred_element_type=F32) + bai_ref[...]
    r = jax.nn.sigmoid(z[:, :LRU_WIDTH])
    ig = jax.nn.sigmoid(z[:, LRU_WIDTH:])
    log_a = -LRU_C * r * jax.nn.softplus(-lam_ref[...])
    a = jnp.exp(log_a)
    a_sc[...] = a
    b_sc[...] = jnp.sqrt(-jnp.tanh(log_a) * (a * a + 1.0)) * ig * xc

    def body(t, h):
        h = a_sc[pl.ds(t, 1), :] * h + b_sc[pl.ds(t, 1), :]
        h_sc[pl.ds(t, 1), :] = h
        return h

    h_last = lax.fori_loop(0, tc, body, hc_sc[...], unroll=8)
    hc_sc[...] = h_last
    y = h_sc[...] * jax.nn.gelu(gate_ref[0])
    o_ref[0] = _rms(y, gain_ref[...])

    @pl.when(ci == pl.num_programs(1) - 1)
    def _():
        conv_ref[0] = tail
        hl_ref[0] = h_last


def _rglru(lx, lg, buf0, h0, conv_w, conv_b, wai, bai, lam, gain, tc):
    b, l, w = lx.shape
    tok = pl.BlockSpec((1, tc, w), lambda i, j: (i, j, 0))
    cst = lambda a: pl.BlockSpec(a.shape, lambda i, j: (0,) * a.ndim)
    bufs = pl.BlockSpec((1, CONV_W - 1, w), lambda i, j: (i, 0, 0))
    hs = pl.BlockSpec((1, 1, w), lambda i, j: (i, 0, 0))
    return pl.pallas_call(
        _rglru_kernel,
        grid=(b, l // tc),
        in_specs=[tok, tok, bufs, hs, cst(conv_w), cst(conv_b), cst(wai), cst(bai), cst(lam), cst(gain)],
        out_specs=[tok, bufs, hs],
        out_shape=[jax.ShapeDtypeStruct((b, l, w), F32),
                   jax.ShapeDtypeStruct((b, CONV_W - 1, w), F32),
                   jax.ShapeDtypeStruct((b, 1, w), F32)],
        scratch_shapes=[pltpu.VMEM((SUBLANES + tc, w), F32), pltpu.VMEM((tc, w), F32), pltpu.VMEM((tc, w), F32),
                        pltpu.VMEM((tc, w), F32), pltpu.VMEM((1, w), F32)],
        compiler_params=_cparams(("arbitrary", "arbitrary")),
        name="rglru",
    )(lx, lg, buf0, h0, conv_w, conv_b, wai, bai, lam, gain)


def _softmax_update(s, v, m_sc, l_sc, acc_sc, rows=slice(None)):
    n = s.shape[1]
    m_prev = m_sc[rows, :]
    m_new = jnp.maximum(m_prev, jnp.max(s, axis=-1, keepdims=True))
    corr = jnp.exp(m_prev - m_new)
    p = jnp.exp(s - jnp.concatenate([m_new] * (n // LANES), axis=1))
    l_sc[rows, :] = l_sc[rows, :] * corr + jnp.sum(p, axis=-1, keepdims=True)
    acc_sc[rows, :] = acc_sc[rows, :] * corr + jnp.dot(p.astype(BF16), v, preferred_element_type=F32)
    m_sc[rows, :] = m_new


def _attn_finish(o, wuv_ref, gain_ref, rows_per_head):
    o = o.astype(BF16)
    om = None
    for hd in range(MLA_HEADS):
        part = jnp.dot(o[hd * rows_per_head:(hd + 1) * rows_per_head], wuv_ref[hd], preferred_element_type=F32)
        om = part if om is None else om + part
    return _rms(om, gain_ref[...])


def _flash_kernel(q_ref, k_ref, wuv_ref, gain_ref, o_ref, m_sc, l_sc, acc_sc, *, qb, kb):
    qi = pl.program_id(1)
    r = MLA_HEADS * qb
    m_sc[...] = jnp.full((r, LANES), -jnp.inf, F32)
    l_sc[...] = jnp.zeros((r, LANES), F32)
    acc_sc[...] = jnp.zeros((r, LANES), F32)
    q0 = qi * qb
    nfull = q0 // kb
    halves = [slice(i * (r // 2), (i + 1) * (r // 2)) for i in range(2)]
    q = q_ref[0].reshape(r, QCAT)
    qs = [q[h] for h in halves]

    def step(start, width, masked):
        kblk = k_ref[0, pl.ds(start, width), :]
        v = kblk[:, :KV_RANK]
        for h, qh in zip(halves, qs):
            s = lax.dot_general(qh, kblk, (((1,), (1,)), ((), ())), preferred_element_type=F32)
            if masked:
                row = lax.broadcasted_iota(jnp.int32, s.shape, 0)
                col = lax.broadcasted_iota(jnp.int32, s.shape, 1)
                s = jnp.where(start + col <= q0 + row % qb, s, -jnp.inf)
            _softmax_update(s, v, m_sc, l_sc, acc_sc, h)

    def body(j, carry):
        step(pl.multiple_of(j * kb, kb), kb, False)
        return carry

    lax.fori_loop(0, nfull, body, 0)
    half = kb // 2
    base = pl.multiple_of(nfull * kb, kb)
    upper = (q0 - base) >= half

    @pl.when(upper)
    def _():
        step(base, half, False)

    step(pl.multiple_of(base + jnp.where(upper, half, 0), half), half, True)
    o_ref[0] = _attn_finish(acc_sc[...] / l_sc[...], wuv_ref, gain_ref, qb)


def _mla_prompt(qcat, kcat, wuv_exp, gain, qb, kb):
    b, _, s, _ = qcat.shape
    r = MLA_HEADS * qb
    return pl.pallas_call(
        functools.partial(_flash_kernel, qb=qb, kb=kb),
        grid=(b, s // qb),
        in_specs=[pl.BlockSpec((1, MLA_HEADS, qb, QCAT), lambda i, j: (i, 0, j, 0)),
                  pl.BlockSpec((1, s, QCAT), lambda i, j: (i, 0, 0)),
                  pl.BlockSpec(wuv_exp.shape, lambda i, j: (0, 0, 0)),
                  pl.BlockSpec(gain.shape, lambda i, j: (0, 0))],
        out_specs=pl.BlockSpec((1, qb, MLA_WIDTH), lambda i, j: (i, j, 0)),
        out_shape=jax.ShapeDtypeStruct((b, s, MLA_WIDTH), F32),
        scratch_shapes=[pltpu.VMEM((r, LANES), F32)] * 3,
        compiler_params=_cparams(("arbitrary", "arbitrary")),
        name="mla_prompt",
    )(qcat, kcat, wuv_exp, gain)


def _mla_sample_kernel(pt_ref, q_ref, knew_ref, ck_hbm, kp_hbm, wuv_ref, gain_ref, o_ref, ckbuf, kpbuf, sems,
                       *, li, n_pages, page, t_new, chunk):
    bi = pl.program_id(0)
    nb = pl.num_programs(0)
    r = MLA_HEADS * t_new

    def page_copies(pg_of, slot):
        out = []
        for j in range(n_pages):
            pg = pg_of(j)
            out.append(pltpu.make_async_copy(ck_hbm.at[li, pg], ckbuf.at[slot, pl.ds(j * page, page), :],
                                             sems.at[0, slot]))
            out.append(pltpu.make_async_copy(kp_hbm.at[li, pg], kpbuf.at[slot, :, pl.ds(j * page, page)],
                                             sems.at[1, slot]))
        return out

    @pl.when(bi == 0)
    def _():
        for c in page_copies(lambda j: pt_ref[0, j], 0):
            c.start()

    @pl.when(bi + 1 < nb)
    def _():
        for c in page_copies(lambda j: pt_ref[bi + 1, j], (bi + 1) % 2):
            c.start()

    slot = bi % 2
    q = q_ref[0].reshape(r, QCAT)
    qb = q.astype(BF16)
    q_lat = qb[:, :KV_RANK]
    q_rot = q[:, KV_RANK:].astype(F32)
    q_r = q_rot[:, :MLA_ROPE]
    for i in range(1, LANES // MLA_ROPE):
        q_r = q_r + q_rot[:, i * MLA_ROPE:(i + 1) * MLA_ROPE]
    q_r = q_r.astype(BF16)

    kn = knew_ref[0].astype(BF16)
    s = lax.dot_general(qb, kn, (((1,), (1,)), ((), ())), preferred_element_type=F32)
    row = lax.broadcasted_iota(jnp.int32, s.shape, 0)
    col = lax.broadcasted_iota(jnp.int32, s.shape, 1)
    s = jnp.where(row % t_new >= col, s, -jnp.inf)
    m = jnp.max(s, axis=-1, keepdims=True)
    p = jnp.exp(s - m)
    l = jnp.sum(p, axis=-1, keepdims=True)
    acc = jnp.dot(p.astype(BF16), kn[:, :KV_RANK], preferred_element_type=F32)

    for c in page_copies(lambda j: 0, slot):
        c.wait()

    for c in range(n_pages * page // chunk):
        ck = ckbuf[slot, pl.ds(c * chunk, chunk), :].astype(BF16)
        kp = kpbuf[slot, :, pl.ds(c * chunk, chunk)].astype(BF16)
        s = (lax.dot_general(q_lat, ck, (((1,), (1,)), ((), ())), preferred_element_type=F32)
             + jnp.dot(q_r, kp, preferred_element_type=F32))
        m_new = jnp.maximum(m, jnp.max(s, axis=-1, keepdims=True))
        corr = jnp.exp(m - m_new)
        p = jnp.exp(s - m_new)
        l = l * corr + jnp.sum(p, axis=-1, keepdims=True)
        acc = acc * corr + jnp.dot(p.astype(BF16), ck, preferred_element_type=F32)
        m = m_new
    o_ref[0] = _attn_finish(acc / l, wuv_ref, gain_ref, t_new)


def _mla_sample(qcat, kcat, cache_ckv, cache_kr_t, page_table, li, wuv_exp, gain):
    b, _, t, _ = qcat.shape
    n_pages = page_table.shape[1]
    page, rank = cache_ckv.shape[2], cache_ckv.shape[3]
    rope_w = cache_kr_t.shape[2]
    chunk = min(8192, n_pages * page)
    grid_spec = pltpu.PrefetchScalarGridSpec(
        num_scalar_prefetch=1,
        grid=(b,),
        in_specs=[pl.BlockSpec((1, MLA_HEADS, t, QCAT), lambda bi, pt: (bi, 0, 0, 0)),
                  pl.BlockSpec((1, t, QCAT), lambda bi, pt: (bi, 0, 0)),
                  pl.BlockSpec(memory_space=pl.ANY),
                  pl.BlockSpec(memory_space=pl.ANY),
                  pl.BlockSpec(wuv_exp.shape, lambda bi, pt: (0, 0, 0)),
                  pl.BlockSpec(gain.shape, lambda bi, pt: (0, 0))],
        out_specs=pl.BlockSpec((1, t, MLA_WIDTH), lambda bi, pt: (bi, 0, 0)),
        scratch_shapes=[pltpu.VMEM((2, n_pages * page, rank), F32),
                        pltpu.VMEM((2, rope_w, n_pages * page), F32),
                        pltpu.SemaphoreType.DMA((2, 2))],
    )
    return pl.pallas_call(
        functools.partial(_mla_sample_kernel, li=li, n_pages=n_pages, page=page, t_new=t, chunk=chunk),
        grid_spec=grid_spec,
        out_shape=jax.ShapeDtypeStruct((b, t, MLA_WIDTH), F32),
        compiler_params=_cparams(("arbitrary",)),
        name="mla_sample",
    )(page_table, qcat, kcat, cache_ckv, cache_kr_t, wuv_exp, gain)


def _outproj_kernel(ret_ref, lru_ref, mla_ref, x_ref, w_ref, g1_ref, gt1_ref, g2_ref, sc2_ref, sh2_ref,
                    x1_ref, h2_ref):
    bt, lt, d = x_ref.shape
    rows = bt * lt
    mix = jnp.concatenate([ret_ref[...].reshape(rows, RET_WIDTH), lru_ref[...].reshape(rows, LRU_WIDTH),
                           mla_ref[...].reshape(rows, MLA_WIDTH)], axis=1).astype(BF16)
    y = jnp.dot(mix, w_ref[...], preferred_element_type=F32).reshape(bt, lt, d)
    x1 = x_ref[...] + gt1_ref[...] * _rms(y, g1_ref[...])
    x1_ref[...] = x1
    h2_ref[...] = _rms(x1, g2_ref[...]) * (1.0 + sc2_ref[...]) + sh2_ref[...]


def _outproj(ret, lru, mla, x, w_out, g1, gt1, g2, sc2, sh2, bt, lt):
    b, l, d = x.shape
    tok = lambda w: pl.BlockSpec((bt, lt, w), lambda i, j: (i, j, 0))
    mod = pl.BlockSpec((bt, 1, d), lambda i, j: (i, 0, 0))
    gsp = pl.BlockSpec((1, 1, d), lambda i, j: (0, 0, 0))
    return pl.pallas_call(
        _outproj_kernel,
        grid=(b // bt, l // lt),
        in_specs=[tok(RET_WIDTH), tok(LRU_WIDTH), tok(MLA_WIDTH), tok(d),
                  pl.BlockSpec(w_out.shape, lambda i, j: (0, 0)), gsp, mod, gsp, mod, mod],
        out_specs=[tok(d), tok(d)],
        out_shape=[jax.ShapeDtypeStruct((b, l, d), F32)] * 2,
        compiler_params=_cparams(("arbitrary", "arbitrary")),
        name="outproj",
    )(ret, lru, mla, x, w_out, g1, gt1, g2, sc2, sh2)


def _split_dot_t(w, x):
    dn = (((1,), (1,)), ((), ()))
    wh = w.astype(BF16)
    wl = (w - wh.astype(F32)).astype(BF16)
    xh = x.astype(BF16)
    xl = (x - xh.astype(F32)).astype(BF16)
    return (lax.dot_general(wh, xh, dn, preferred_element_type=F32)
            + lax.dot_general(wh, xl, dn, preferred_element_type=F32)
            + lax.dot_general(wl, xh, dn, preferred_element_type=F32))


def _router_kernel(x_ref, w_ref, b_ref, upper_ref, lower_ref, eid_ref, pos_ref, gate_ref, cnt_ref, cnt_sc):
    ti = pl.program_id(0)
    tm = x_ref.shape[0]

    @pl.when(ti == 0)
    def _():
        cnt_sc[...] = jnp.zeros(cnt_sc.shape, F32)

    logits = _split_dot_t(w_ref[...], x_ref[...])
    scores = jax.nn.sigmoid(logits)
    biased = scores + b_ref[...]
    groups = [biased[g * GROUP_SIZE:(g + 1) * GROUP_SIZE, :] for g in range(N_GROUPS)]
    sub = lax.broadcasted_iota(jnp.int32, (GROUP_SIZE, tm), 0)
    gs = []
    for bg in groups:
        m1 = jnp.max(bg, axis=0, keepdims=True)
        i1 = jnp.min(jnp.where(bg == m1, sub, GROUP_SIZE), axis=0, keepdims=True)
        m2 = jnp.max(jnp.where(sub == i1, -jnp.inf, bg), axis=0, keepdims=True)
        gs.append(m1 + m2)
    masked = []
    for g in range(N_GROUPS):
        rank = jnp.zeros((1, tm), F32)
        for g2 in range(N_GROUPS):
            if g2 != g:
                ahead = (gs[g2] >= gs[g]) if g2 < g else (gs[g2] > gs[g])
                rank = rank + jnp.where(ahead, 1.0, 0.0)
        masked.append(jnp.where(rank < TOPK_GROUPS, groups[g], -jnp.inf))
    sels = []
    for g in range(N_GROUPS):
        mine = masked[g]
        rank = jnp.zeros((GROUP_SIZE, tm), F32)
        for g2 in range(N_GROUPS):
            for s2 in range(GROUP_SIZE):
                other = jnp.broadcast_to(masked[g2][s2:s2 + 1, :], (GROUP_SIZE, tm))
                if g2 < g:
                    rank = rank + jnp.where(other >= mine, 1.0, 0.0)
                elif g2 > g:
                    rank = rank + jnp.where(other > mine, 1.0, 0.0)
                else:
                    tie = jnp.where(sub > s2, 1.0, 0.0)
                    rank = rank + jnp.where(other > mine, 1.0, jnp.where(other == mine, tie, 0.0))
        sels.append(jnp.where(rank < TOP_K, 1.0, 0.0))
    self32 = jnp.concatenate(sels, axis=0)
    sel = self32 > 0.5
    picked = jnp.where(sel, scores, 0.0)
    gate = picked / jnp.sum(picked, axis=0, keepdims=True) * ROUTED_SCALE
    selb = self32.astype(BF16)
    pos = jnp.dot(selb, upper_ref[...], preferred_element_type=F32) + cnt_sc[:, :1]
    cnt_sc[...] = cnt_sc[...] + jnp.sum(self32, axis=1, keepdims=True)
    slot = jnp.dot(lower_ref[...], selb, preferred_element_type=F32)
    efl = lax.broadcasted_iota(jnp.int32, (N_EXPERTS, tm), 0).astype(F32)
    for kk in range(TOP_K):
        hit = sel & (slot == float(kk))
        eid_ref[kk:kk + 1, :] = jnp.sum(jnp.where(hit, efl, 0.0), axis=0, keepdims=True).astype(jnp.int32)
        pos_ref[kk:kk + 1, :] = jnp.sum(jnp.where(hit, pos, 0.0), axis=0, keepdims=True).astype(jnp.int32)
        gate_ref[kk:kk + 1, :] = jnp.sum(jnp.where(hit, gate, 0.0), axis=0, keepdims=True)

    @pl.when(ti == pl.num_programs(0) - 1)
    def _():
        cnt_ref[...] = cnt_sc[...]


def _router(h2, w_rt, b_r, tm):
    t, d = h2.shape
    upper = jnp.triu(jnp.ones((tm, tm), F32), 1).astype(BF16)
    lower = jnp.tril(jnp.ones((N_EXPERTS, N_EXPERTS), F32), -1).astype(BF16)
    kt = pl.BlockSpec((TOP_K, tm), lambda i: (0, i))
    cst = lambda a: pl.BlockSpec(a.shape, lambda i: (0,) * a.ndim)
    return pl.pallas_call(
        _router_kernel,
        grid=(t // tm,),
        in_specs=[pl.BlockSpec((tm, d), lambda i: (i, 0)), cst(w_rt), cst(b_r), cst(upper), cst(lower)],
        out_specs=[kt, kt, kt, pl.BlockSpec((N_EXPERTS, LANES), lambda i: (0, 0))],
        out_shape=[jax.ShapeDtypeStruct((TOP_K, t), jnp.int32), jax.ShapeDtypeStruct((TOP_K, t), jnp.int32),
                   jax.ShapeDtypeStruct((TOP_K, t), F32), jax.ShapeDtypeStruct((N_EXPERTS, LANES), F32)],
        scratch_shapes=[pltpu.VMEM((N_EXPERTS, LANES), F32)],
        compiler_params=_cparams(("arbitrary",)),
        name="router",
    )(h2, w_rt, b_r, upper, lower)


PACK_ROWS = 4


def _pack_rows(x):
    t, d = x.shape
    bits = lax.bitcast_convert_type(x.astype(BF16), jnp.uint16).astype(jnp.uint32)
    word = (bits[:, d // 2:] << 16) | bits[:, :d // 2]
    return lax.bitcast_convert_type(word, jnp.int32).reshape(t * PACK_ROWS, LANES)


def _unpack_words(w):
    lo = lax.bitcast_convert_type(lax.shift_left(w, jnp.int32(16)), F32)
    hi = lax.bitcast_convert_type(w & jnp.int32(-65536), F32)
    return lo, hi


def _dispatch_kernel(dest_ref, x_ref, rows_in_ref, rows_ref, sem):
    del rows_in_ref
    tm = x_ref.shape[0] // PACK_ROWS

    def row_copy(t, d):
        return pltpu.make_async_copy(x_ref.at[pl.ds(pl.multiple_of(t * PACK_ROWS, PACK_ROWS), PACK_ROWS), :],
                                     rows_ref.at[pl.ds(pl.multiple_of(d * PACK_ROWS, PACK_ROWS), PACK_ROWS), :], sem)

    def issue(t, c):
        for kk in range(TOP_K):
            row_copy(t, dest_ref[t * TOP_K + kk]).start()
        return c

    def drain(t, c):
        for kk in range(TOP_K):
            row_copy(0, 0).wait()
        return c

    lax.fori_loop(0, tm, issue, 0)
    lax.fori_loop(0, tm, drain, 0)


def _dispatch(dest_flat, h2p, rows0, tm):
    t = h2p.shape[0] // PACK_ROWS
    return pl.pallas_call(
        _dispatch_kernel,
        grid=(t // tm,),
        in_specs=[pl.BlockSpec((tm * TOP_K,), lambda i: (i,), memory_space=pltpu.SMEM),
                  pl.BlockSpec((tm * PACK_ROWS, LANES), lambda i: (i, 0)),
                  pl.BlockSpec(memory_space=pl.ANY)],
        out_specs=pl.BlockSpec(memory_space=pl.ANY),
        out_shape=jax.ShapeDtypeStruct(rows0.shape, jnp.int32),
        scratch_shapes=[pltpu.SemaphoreType.DMA(())],
        input_output_aliases={2: 0},
        compiler_params=_cparams(("arbitrary",)),
        name="moe_dispatch",
    )(dest_flat, h2p, rows0)


def _expert_kernel(be_ref, nu_ref, rows_ref, wgu_ref, wd_ref, o_ref, wgu_sc, wd_sc):
    i = pl.program_id(0)
    prev = be_ref[jnp.maximum(i - 1, 0)]
    fresh = (i == 0) | (be_ref[i] != prev)

    @pl.when(fresh)
    def _():
        wgu_sc[...] = wgu_ref[...].astype(BF16)
        wd_sc[...] = wd_ref[...].astype(BF16)

    @pl.when(i < nu_ref[0])
    def _():
        blk = rows_ref.shape[0] // PACK_ROWS
        halves = [_unpack_words(rows_ref[pl.ds(j, blk, stride=PACK_ROWS), :]) for j in range(PACK_ROWS)]
        x = jnp.concatenate([h[0] for h in halves] + [h[1] for h in halves], axis=1).astype(BF16)
        gu = jnp.dot(x, wgu_sc[...], preferred_element_type=F32)
        act = _silu(gu[:, :D_EXPERT]) * gu[:, D_EXPERT:]
        y = jnp.dot(act.astype(BF16), wd_sc[...], preferred_element_type=F32)
        bits = lax.bitcast_convert_type(y.astype(BF16).astype(F32), jnp.int32)
        for j in range(PACK_ROWS):
            lo = lax.shift_right_logical(bits[:, j * LANES:(j + 1) * LANES], jnp.int32(16))
            hi = bits[:, (j + PACK_ROWS) * LANES:(j + PACK_ROWS + 1) * LANES] & jnp.int32(-65536)
            o_ref[pl.ds(j, blk, stride=PACK_ROWS), :] = hi | lo


def _experts(block_e, n_used, rows, w_gu, w_down, li, blk):
    n_rows = rows.shape[0] // PACK_ROWS
    n_blocks = n_rows // blk
    d, de2 = w_gu.shape[-2:]
    de = w_down.shape[-2]

    def rmap(i, be, nu):
        return (jnp.minimum(i, nu[0] - 1), 0)

    grid_spec = pltpu.PrefetchScalarGridSpec(
        num_scalar_prefetch=2,
        grid=(n_blocks,),
        in_specs=[pl.BlockSpec((blk * PACK_ROWS, LANES), rmap),
                  pl.BlockSpec((None, None, d, de2), lambda i, be, nu: (li, be[i], 0, 0)),
                  pl.BlockSpec((None, None, de, d), lambda i, be, nu: (li, be[i], 0, 0))],
        out_specs=pl.BlockSpec((blk * PACK_ROWS, LANES), rmap),
        scratch_shapes=[pltpu.VMEM((d, de2), BF16), pltpu.VMEM((de, d), BF16)],
    )
    return pl.pallas_call(
        _expert_kernel,
        grid_spec=grid_spec,
        out_shape=jax.ShapeDtypeStruct(rows.shape, jnp.int32),
        compiler_params=_cparams(("arbitrary",)),
        name="moe_experts",
    )(block_e, n_used, rows, w_gu, w_down)


def _combine_kernel(dest_ref, dnext_ref, gate_ref, orow_ref, h2_ref, x1_ref, wsgu_ref, wsd_ref, g3_ref, gt2_ref,
                    y_ref, buf, sems):
    bt, lt, d = x1_ref.shape
    tm = bt * lt
    ns = PACK_ROWS
    step = pl.program_id(0) * pl.num_programs(1) + pl.program_id(1)
    nsteps = pl.num_programs(0) * pl.num_programs(1)
    slot = step % 2

    def row_copy(src_row, t, kk, sl):
        return pltpu.make_async_copy(orow_ref.at[pl.ds(pl.multiple_of(src_row * ns, ns), ns), :],
                                     buf.at[sl, kk, pl.ds(pl.multiple_of(t * ns, ns), ns), :],
                                     sems.at[sl])

    def issue_tile(dref, sl):
        def issue(t, c):
            for kk in range(TOP_K):
                row_copy(dref[t * TOP_K + kk], t, kk, sl).start()
            return c
        lax.fori_loop(0, tm, issue, 0)

    @pl.when(step == 0)
    def _():
        issue_tile(dest_ref, 0)

    @pl.when(step + 1 < nsteps)
    def _():
        issue_tile(dnext_ref, 1 - slot)

    hb = h2_ref[...].astype(BF16)
    gu = jnp.dot(hb, wsgu_ref[...], preferred_element_type=F32)
    ds = wsd_ref.shape[0]
    f = jnp.dot((_silu(gu[:, :ds]) * gu[:, ds:]).astype(BF16), wsd_ref[...], preferred_element_type=F32)

    def drain(t, c):
        for kk in range(TOP_K):
            row_copy(0, 0, 0, slot).wait()
        return c

    lax.fori_loop(0, tm, drain, 0)
    gate = gate_ref[...]
    fs = [f[:, j * LANES:(j + 1) * LANES] for j in range(2 * ns)]
    for kk in range(TOP_K):
        gk = jnp.broadcast_to(gate[:, kk:kk + 1], (tm, LANES))
        for j in range(ns):
            lo, hi = _unpack_words(buf[slot, kk, pl.ds(j, tm, stride=ns), :])
            fs[j] = fs[j] + lo * gk
            fs[j + ns] = fs[j + ns] + hi * gk
    f = jnp.concatenate(fs, axis=1)
    y_ref[...] = x1_ref[...] + gt2_ref[...] * _rms(f, g3_ref[...]).reshape(bt, lt, d)


def _combine(dest_flat, gate_tk, out_rows, h2, x1, ws_gu, ws_down, g3, gt2, bt, lt):
    b, l, d = x1.shape
    tm = bt * lt
    nl = l // lt
    nsteps = (b // bt) * nl
    tok = pl.BlockSpec((bt, lt, d), lambda i, j: (i, j, 0))
    flat = lambda w: pl.BlockSpec((tm, w), lambda i, j: (i * nl + j, 0))
    return pl.pallas_call(
        _combine_kernel,
        grid=(b // bt, nl),
        in_specs=[pl.BlockSpec((tm * TOP_K,), lambda i, j: (i * nl + j,), memory_space=pltpu.SMEM),
                  pl.BlockSpec((tm * TOP_K,), lambda i, j: (jnp.minimum(i * nl + j + 1, nsteps - 1),),
                               memory_space=pltpu.SMEM),
                  flat(TOP_K), pl.BlockSpec(memory_space=pl.ANY), flat(d), tok,
                  pl.BlockSpec(ws_gu.shape, lambda i, j: (0, 0)), pl.BlockSpec(ws_down.shape, lambda i, j: (0, 0)),
                  pl.BlockSpec((1, 1, d), lambda i, j: (0, 0, 0)),
                  pl.BlockSpec((bt, 1, d), lambda i, j: (i, 0, 0))],
        out_specs=tok,
        out_shape=jax.ShapeDtypeStruct((b, l, d), F32),
        scratch_shapes=[pltpu.VMEM((2, TOP_K, tm * PACK_ROWS, LANES), jnp.int32), pltpu.SemaphoreType.DMA((2,))],
        compiler_params=_cparams(("arbitrary", "arbitrary")),
        name="moe_combine",
    )(dest_flat, dest_flat, gate_tk, out_rows, h2, x1, ws_gu, ws_down, g3, gt2)


def _moe(groups, g3, w_rt, b_r, w_exp_gu, w_exp_down, ws_gu, ws_down, li, blk, rows_init):
    d = groups[0][0].shape[-1]
    sizes = [g[0].shape[0] * g[0].shape[1] for g in groups]
    t = sum(sizes)
    flats = [g[0].reshape(n, d) for g, n in zip(groups, sizes)]
    h2f = flats[0] if len(flats) == 1 else jnp.concatenate(flats, axis=0)
    eid, pos, gate, cnt = _router(h2f, w_rt, b_r, _largest_tile(t, 512))
    counts = cnt[:, 0].astype(jnp.int32)
    padded = (counts + blk - 1) // blk * blk
    pends = jnp.cumsum(padded)
    pstart = pends - padded
    n_blocks = -(-(t * TOP_K) // blk) + N_EXPERTS
    starts = jnp.arange(n_blocks, dtype=jnp.int32) * blk
    block_e = jnp.minimum(jnp.sum((pends[None, :] <= starts[:, None]).astype(jnp.int32), axis=1), N_EXPERTS - 1)
    n_used = (pends[-1:] // blk).astype(jnp.int32)
    first_row = jnp.sum(jnp.where(eid[:, :, None] == jnp.arange(N_EXPERTS, dtype=jnp.int32), pstart, 0), axis=-1)
    dest = (first_row + pos).T.reshape(t * TOP_K)
    gate_tk = gate.T
    if rows_init is None:
        rows_init = jnp.zeros((n_blocks * blk * PACK_ROWS, LANES), jnp.int32)
    rows = _dispatch(dest, _pack_rows(h2f), rows_init, _largest_tile(t, 512))
    out_rows = _experts(block_e, n_used, rows, w_exp_gu, w_exp_down, li, blk)
    outs, off = [], 0
    for (h2, x1, gt2, bt, lt), n, flat in zip(groups, sizes, flats):
        outs.append(_combine(dest[off * TOP_K:(off + n) * TOP_K], gate_tk[off:off + n], out_rows, flat, x1,
                             ws_gu, ws_down, g3, gt2, bt, lt))
        off += n
    return outs, rows


def _permute_w_in(w_in):
    dp, d, _ = w_in.shape
    hd = MLA_NOPE + MLA_ROPE
    mq = w_in[:, :, C_QN:C_QN + MLA_HEADS * hd].reshape(dp, d, MLA_HEADS, hd)
    qn = mq[..., :MLA_NOPE].reshape(dp, d, MLA_HEADS * MLA_NOPE)
    qr = mq[..., MLA_NOPE:].reshape(dp, d, MLA_HEADS * MLA_ROPE)
    mc0 = C_QN + MLA_HEADS * hd
    mc = w_in[:, :, mc0:mc0 + KV_RANK]
    mk = w_in[:, :, mc0 + KV_RANK:mc0 + KV_RANK + MLA_ROPE]
    return jnp.concatenate([w_in[:, :, :C_QN], qn, qr, mc] + [mk] * (LANES // MLA_ROPE), axis=-1).astype(BF16)


def _rope_tables(pos, dim):
    half = dim // 2
    inv = ROPE_BASE ** (-jnp.arange(0, dim, 2, dtype=F32) / dim)
    ang = pos.astype(F32)[:, None] * inv[None, :]
    cos, sin = jnp.cos(ang), jnp.sin(ang)
    reps = LANES // dim
    cos_t = jnp.tile(jnp.concatenate([cos, cos], axis=1), (1, reps))
    sin_t = jnp.tile(jnp.concatenate([-sin, sin], axis=1), (1, reps))
    return cos_t, sin_t


def _block_diag(blocks):
    n, a, b = blocks.shape
    eye = jnp.eye(n, dtype=blocks.dtype)
    return (eye[:, None, :, None] * blocks[:, :, None, :]).reshape(n * a, n * b)


def _mixer(x, mod, pos_tabs, li, p, past, tiles):
    b, l, d = x.shape
    bt, lt = tiles['bt'], tiles['lt']
    sh1, sc1, gt1, sh2, sc2, gt2 = [m[:, None, :] for m in jnp.split(mod, 6, axis=-1)]
    g = p['norm_gains']
    gains = [g[i][None, None, :] for i in range(4)]
    gain = p['grp_gain']
    rq, rk, rv, rg, lx, lg, qcat, kcat, ckv, kr = _inproj(
        x, sc1, sh1, gains[0], p['w_in_p'], p['wuk_bd'], p['kv_norm'][None, :], pos_tabs, bt, lt, tiles['qdtype'])
    if past is None:
        r0 = jnp.zeros((b, RET_HEADS, RET_DK, RET_DV), F32)
        buf0 = jnp.zeros((b, CONV_W - 1, LRU_WIDTH), F32)
        h0 = jnp.zeros((b, 1, LRU_WIDTH), F32)
    else:
        r0, buf0, h0 = past['ret'], past['conv'], past['lru'][:, None, :]
    ret_out, r_new = _retention(rq, rk, rv, rg, r0, gain[None, :RET_WIDTH], tiles['rbb'])
    lru_out, conv_new, h_new = _rglru(
        lx, lg, buf0, h0, p['conv_w'], p['conv_b'][None, :], p['wai'], p['bai'], p['lru_lambda'][None, :],
        gain[None, RET_WIDTH:RET_WIDTH + LRU_WIDTH], tiles['tc'])
    mla_gain = gain[None, RET_WIDTH + LRU_WIDTH:]
    if past is None:
        mla_out = _mla_prompt(qcat, kcat, p['wuv_exp'], mla_gain, tiles['qb'], tiles['kb'])
    else:
        mla_out = _mla_sample(qcat, kcat, past['cache_ckv'], past['cache_kr_t'], past['page_table'], li,
                              p['wuv_exp'], mla_gain)
    x1, h2 = _outproj(ret_out, lru_out, mla_out, x, p['w_out'], gains[1], gt1, gains[2], sc2, sh2, bt, lt)
    return (h2, x1, gt2, tiles['cbt'], tiles['clt']), (ckv, kr, r_new, conv_new, h_new[:, 0])


def _largest_tile(n, cap):
    t = min(n, cap)
    while n % t:
        t //= 2
    return t


def kernel(x_prompt, x_sample, cache_ckv, cache_krope, state_ret, state_conv, state_lru, page_table, c_prompt, c_sample, w_ada, b_ada, norm_gains, w_in, w_out, grp_gain, conv_w, conv_b, w_lru_a, b_lru_a, w_lru_i, b_lru_i, lru_lambda, kv_norm, w_ukv, w_router, b_router, w_exp_gu, w_exp_down, w_sh_gu, w_sh_down):
    depth = w_in.shape[0]
    bp, lp, d = x_prompt.shape
    bs, ls, _ = x_sample.shape
    past_len = page_table.shape[1] * cache_ckv.shape[2]

    n_c = bp + bs
    n_cp = -(-n_c // SUBLANES) * SUBLANES
    c_all = jnp.concatenate([c_prompt, c_sample, jnp.zeros((n_cp - n_c, d), F32)], axis=0)
    mod_all = _adaln(c_all, w_ada, b_ada[:, None, :])

    w_in_p = _permute_w_in(w_in)
    w_uk = w_ukv[..., :MLA_NOPE]
    w_uv = w_ukv[..., MLA_NOPE:]
    eye_h = jnp.eye(MLA_HEADS, dtype=F32)

    pos_p = jnp.arange(lp)
    pos_s = past_len + jnp.arange(ls)
    tabs_p = _rope_tables(pos_p, RET_DK) + _rope_tables(pos_p, MLA_ROPE)
    tabs_s = _rope_tables(pos_s, RET_DK) + _rope_tables(pos_s, MLA_ROPE)

    lt_p = _largest_tile(lp, 512)
    kb_p = _largest_tile(lp, 1024)
    tiles_p = dict(bt=1, lt=lt_p, rbb=1, qdtype=BF16, tc=_largest_tile(lp, 512), qb=_largest_tile(lp, min(256, kb_p // 2)),
                   kb=kb_p, cbt=1, clt=_largest_tile(lp, 256))
    bt_s = _largest_tile(bs, 32)
    tiles_s = dict(bt=bt_s, lt=ls, qdtype=F32, tc=ls, rbb=_largest_tile(bs, 8),
                   cbt=_largest_tile(bs, 32), clt=ls)

    cache_kr_t = jnp.swapaxes(cache_krope, 2, 3)
    y_p, y_s = x_prompt, x_sample
    blk = 256 if bp * lp + bs * ls >= 4096 else 128
    moe_rows = None
    st_p, st_s = [], []
    for li in range(depth):
        wuk_bd = _block_diag(jnp.transpose(w_uk[li], (1, 2, 0))).astype(BF16)
        wuv_h = jnp.transpose(w_uv[li], (1, 0, 2))
        wuv_exp = (wuv_h[:, :, None, :] * eye_h[:, None, :, None]).reshape(
            MLA_HEADS, KV_RANK, MLA_WIDTH).astype(BF16)
        wai = jnp.concatenate([_block_diag(w_lru_a[li]), _block_diag(w_lru_i[li])], axis=1).astype(BF16)
        bai = jnp.concatenate([b_lru_a[li], b_lru_i[li]])[None, :]
        p = dict(norm_gains=norm_gains[li], w_in_p=w_in_p[li], wuk_bd=wuk_bd, wuv_exp=wuv_exp,
                 w_out=w_out[li].astype(BF16), grp_gain=grp_gain[li], conv_w=conv_w[li], conv_b=conv_b[li],
                 wai=wai, bai=bai, lru_lambda=lru_lambda[li], kv_norm=kv_norm[li],
                 w_rt=w_router[li].T, b_router=b_router[li], w_exp_gu=w_exp_gu, w_exp_down=w_exp_down,
                 ws_gu=w_sh_gu[li].astype(BF16), ws_down=w_sh_down[li].astype(BF16))
        grp_p, sp = _mixer(y_p, mod_all[li, :bp], tabs_p, li, p, None, tiles_p)
        past = dict(ret=state_ret[li], conv=state_conv[li], lru=state_lru[li], cache_ckv=cache_ckv,
                    cache_kr_t=cache_kr_t, page_table=page_table)
        grp_s, ss = _mixer(y_s, mod_all[li, bp:bp + bs], tabs_s, li, p, past, tiles_s)
        (y_p, y_s), moe_rows = _moe([grp_p, grp_s], norm_gains[li][3][None, None, :], p['w_rt'], p['b_router'][:, None],
                                    p['w_exp_gu'], p['w_exp_down'], p['ws_gu'], p['ws_down'], li, blk, moe_rows)
        st_p.append(sp)
        st_s.append(ss)
    stack = lambda sts, i: jnp.stack([s[i] for s in sts])
    return (y_p, y_s) + tuple(stack(st_p, i) for i in range(5)) + tuple(stack(st_s, i) for i in range(5))
```

```python
import functools

import jax
import jax.numpy as jnp
from jax import lax
from jax.experimental import pallas as pl
from jax.experimental.pallas import tpu as pltpu

F32 = jnp.float32
BF16 = jnp.bfloat16
EPS = 1e-6
ROPE_BASE = 10000.0

RET_HEADS = 4
RET_DK = 64
RET_DV = 64
RET_WIDTH = RET_HEADS * RET_DV
RET_CHUNK = 128
LRU_WIDTH = 256
LRU_BLOCKS = 4
LRU_BLOCK = LRU_WIDTH // LRU_BLOCKS
CONV_W = 4
LRU_C = 8.0
MLA_HEADS = 8
MLA_NOPE = 64
MLA_ROPE = 32
MLA_V = 64
KV_RANK = 128
MLA_WIDTH = MLA_HEADS * MLA_V
N_EXPERTS = 64
TOP_K = 8
N_GROUPS = 8
GROUP_SIZE = N_EXPERTS // N_GROUPS
TOPK_GROUPS = 4
D_EXPERT = 256
ROUTED_SCALE = 2.5

LANES = 128
SUBLANES = 8
QCAT = 2 * LANES
C_RQ, C_RK, C_RV, C_RG, C_LX, C_LG = 0, 256, 512, 768, 1024, 1280
C_QN = 1536
C_QR = C_QN + MLA_HEADS * MLA_NOPE
C_MC = C_QR + MLA_HEADS * MLA_ROPE
C_MK = C_MC + KV_RANK
N_PROJ = C_MK + LANES


def _cparams(sem, vmem_mb=48):
    return pltpu.CompilerParams(dimension_semantics=sem, vmem_limit_bytes=vmem_mb * 2 ** 20)


def _rms(x, g):
    return x * lax.rsqrt(jnp.mean(x * x, axis=-1, keepdims=True) + EPS) * g


def _silu(x):
    return x * jax.nn.sigmoid(x)


def _adaln_kernel(c_ref, w_ref, b_ref, o_ref):
    c = c_ref[...]
    s = _silu(c).astype(BF16)
    o_ref[...] = jnp.dot(s, w_ref[...].astype(BF16), preferred_element_type=F32) + b_ref[...]


def _adaln(c, w_ada, b_ada):
    depth, d, n = w_ada.shape
    bp = c.shape[0]
    tn = 768
    return pl.pallas_call(
        _adaln_kernel,
        grid=(depth, n // tn),
        in_specs=[pl.BlockSpec((bp, d), lambda l, j: (0, 0)),
                  pl.BlockSpec((None, d, tn), lambda l, j: (l, 0, j)),
                  pl.BlockSpec((None, 1, tn), lambda l, j: (l, 0, j))],
        out_specs=pl.BlockSpec((None, bp, tn), lambda l, j: (l, 0, j)),
        out_shape=jax.ShapeDtypeStruct((depth, bp, n), F32),
        compiler_params=_cparams(("arbitrary", "arbitrary")),
        name="adaln",
    )(c, w_ada, b_ada)


def _rope_slab(v, cos, sin_signed, half):
    lane = lax.broadcasted_iota(jnp.int32, v.shape, 1)
    first = (lane % (2 * half)) < half
    partner = jnp.where(first, pltpu.roll(v, LANES - half, 1), pltpu.roll(v, half, 1))
    return v * cos + partner * sin_signed


def _rope(v, cos, sin_signed, half):
    n = v.shape[1] // LANES
    outs = [_rope_slab(v[:, i * LANES:(i + 1) * LANES], cos, sin_signed, half) for i in range(n)]
    return outs[0] if n == 1 else jnp.concatenate(outs, axis=1)


def _inproj_kernel(x_ref, sc_ref, sh_ref, g_ref, w_ref, wuk_ref, kvg_ref, c64_ref, s64_ref, c32_ref, s32_ref,
                   rq_ref, rk_ref, rv_ref, rg_ref, lx_ref, lg_ref, qcat_ref, kcat_ref, ckv_ref, kr_ref):
    bt, lt, d = x_ref.shape
    rows = bt * lt
    h = _rms(x_ref[...], g_ref[...]) * (1.0 + sc_ref[...]) + sh_ref[...]
    hb = h.reshape(rows, d).astype(BF16)

    def proj(a, b):
        return jnp.dot(hb, w_ref[:, a:b], preferred_element_type=F32)

    def table(ref):
        t = ref[...]
        if bt > 1:
            t = jnp.broadcast_to(t[None], (bt, lt, LANES)).reshape(rows, LANES)
        return t

    c64, s64, c32, s32 = table(c64_ref), table(s64_ref), table(c32_ref), table(s32_ref)
    rq_ref[...] = _rope(proj(C_RQ, C_RK), c64, s64, RET_DK // 2).reshape(bt, lt, RET_WIDTH)
    rk_ref[...] = _rope(proj(C_RK, C_RV), c64, s64, RET_DK // 2).reshape(bt, lt, RET_WIDTH)
    rv_ref[...] = proj(C_RV, C_RG).reshape(bt, lt, RET_WIDTH)
    rg_ref[...] = proj(C_RG, C_LX).reshape(bt, lt, RET_WIDTH)
    lx_ref[...] = proj(C_LX, C_LG).reshape(bt, lt, LRU_WIDTH)
    lg_ref[...] = proj(C_LG, C_QN).reshape(bt, lt, LRU_WIDTH)

    scale = (MLA_NOPE + MLA_ROPE) ** -0.5
    q_lat = jnp.dot(proj(C_QN, C_QR).astype(BF16), wuk_ref[...], preferred_element_type=F32) * scale
    q_rope = _rope(proj(C_QR, C_MC), c32, s32, MLA_ROPE // 2) * scale
    lane = lax.broadcasted_iota(jnp.int32, (rows, LANES), 1)
    heads_per_slab = LANES // MLA_ROPE
    for hd in range(MLA_HEADS):
        slab = q_rope[:, (hd // heads_per_slab) * LANES:(hd // heads_per_slab + 1) * LANES]
        own = jnp.where(lane // MLA_ROPE == hd % heads_per_slab, slab, 0.0)
        qh = jnp.concatenate([q_lat[:, hd * KV_RANK:(hd + 1) * KV_RANK], own], axis=1)
        qcat_ref[:, hd, :, :] = qh.reshape(bt, lt, QCAT).astype(qcat_ref.dtype)

    ckv = _rms(proj(C_MC, C_MK), kvg_ref[...])
    kr4 = _rope(proj(C_MK, N_PROJ), c32, s32, MLA_ROPE // 2)
    ckv_ref[...] = ckv.reshape(bt, lt, KV_RANK)
    kr_ref[...] = kr4[:, :MLA_ROPE].reshape(bt, lt, MLA_ROPE)
    kcat_ref[...] = jnp.concatenate([ckv, kr4], axis=1).reshape(bt, lt, QCAT).astype(kcat_ref.dtype)


def _inproj(x, sc, sh, g, w_in_p, wuk_bd, kv_gain, tabs, bt, lt, qdtype):
    b, l, d = x.shape
    grid = (b // bt, l // lt)
    c64, s64, c32, s32 = tabs
    tok = lambda w: pl.BlockSpec((bt, lt, w), lambda i, j: (i, j, 0))
    mod = pl.BlockSpec((bt, 1, d), lambda i, j: (i, 0, 0))
    const2 = lambda a: pl.BlockSpec(a.shape, lambda i, j: (0, 0))
    tab = pl.BlockSpec((lt, LANES), lambda i, j: (j, 0))
    out_shape = [jax.ShapeDtypeStruct((b, l, 256), F32)] * 6 + [
        jax.ShapeDtypeStruct((b, MLA_HEADS, l, QCAT), qdtype),
        jax.ShapeDtypeStruct((b, l, QCAT), qdtype),
        jax.ShapeDtypeStruct((b, l, KV_RANK), F32),
        jax.ShapeDtypeStruct((b, l, MLA_ROPE), F32)]
    out_specs = [tok(256)] * 6 + [
        pl.BlockSpec((bt, MLA_HEADS, lt, QCAT), lambda i, j: (i, 0, j, 0)),
        tok(QCAT), tok(KV_RANK), tok(MLA_ROPE)]
    return pl.pallas_call(
        _inproj_kernel,
        grid=grid,
        in_specs=[tok(d), mod, mod, pl.BlockSpec((1, 1, d), lambda i, j: (0, 0, 0)),
                  const2(w_in_p), const2(wuk_bd), const2(kv_gain), tab, tab, tab, tab],
        out_specs=out_specs,
        out_shape=out_shape,
        compiler_params=_cparams(("arbitrary", "arbitrary")),
        name="inproj",
    )(x, sc, sh, g, w_in_p, wuk_bd, kv_gain, c64, s64, c32, s32)


def _retention_kernel(q_ref, k_ref, v_ref, g_ref, r0_ref, dmat_ref, xi_ref, zeta_ref, dec_ref, gain_ref,
                      o_ref, rfin_ref, r_sc):
    ci = pl.program_id(1)

    @pl.when(ci == 0)
    def _():
        r_sc[...] = r0_ref[...]

    for bi in range(q_ref.shape[0]):
        q = q_ref[bi]
        k = k_ref[bi] * (RET_DK ** -0.5)
        v = v_ref[bi]
        outs = []
        for hd in range(RET_HEADS):
            sl = slice(hd * RET_DK, (hd + 1) * RET_DK)
            qh = q[:, sl].astype(BF16)
            kh = k[:, sl]
            vh = v[:, sl].astype(BF16)
            s = lax.dot_general(qh, kh.astype(BF16), (((1,), (1,)), ((), ())), preferred_element_type=F32)
            inner = jnp.dot((s * dmat_ref[hd]).astype(BF16), vh, preferred_element_type=F32)
            r = r_sc[bi, hd]
            cross = jnp.dot(qh, r.astype(BF16), preferred_element_type=F32) * xi_ref[:, hd:hd + 1]
            o = inner + cross
            mu = jnp.mean(o, axis=-1, keepdims=True)
            oc = o - mu
            var = jnp.mean(oc * oc, axis=-1, keepdims=True)
            outs.append(oc * lax.rsqrt(var + EPS))
            kz = (kh * zeta_ref[:, hd:hd + 1]).astype(BF16)
            kv = lax.dot_general(kz, vh, (((0,), (0,)), ((), ())), preferred_element_type=F32)
            r_sc[bi, hd] = dec_ref[hd] * r + kv
        o_ref[bi] = jnp.concatenate(outs, axis=1) * gain_ref[...] * _silu(g_ref[bi])

    @pl.when(ci == pl.num_programs(1) - 1)
    def _():
        rfin_ref[...] = r_sc[...]


def _retention_consts(c):
    log_g = jnp.log1p(-jnp.exp2(-5.0 - jnp.arange(RET_HEADS, dtype=F32)))
    i = jnp.arange(c, dtype=F32)
    diff = i[:, None] - i[None, :]
    dmat = jnp.where(diff >= 0, jnp.exp(jnp.maximum(diff, 0.0)[None] * log_g[:, None, None]), 0.0)
    xi = jnp.exp((i[:, None] + 1.0) * log_g[None, :])
    zeta = jnp.exp((c - 1.0 - i)[:, None] * log_g[None, :])
    dec = jnp.broadcast_to(jnp.exp(c * log_g)[:, None, None], (RET_HEADS, 1, RET_DV))
    return dmat, xi, zeta, dec


def _retention(rq, rk, rv, rg, r0, gain, bb):
    b, l, _ = rq.shape
    c = RET_CHUNK if l % RET_CHUNK == 0 else l
    n = l // c
    dmat, xi, zeta, dec = _retention_consts(c)
    tok = pl.BlockSpec((bb, c, RET_WIDTH), lambda i, j: (i, j, 0))
    st = pl.BlockSpec((bb, RET_HEADS, RET_DK, RET_DV), lambda i, j: (i, 0, 0, 0))
    cst = lambda a: pl.BlockSpec(a.shape, lambda i, j: (0,) * a.ndim)
    return pl.pallas_call(
        _retention_kernel,
        grid=(b // bb, n),
        in_specs=[tok, tok, tok, tok, st, cst(dmat), cst(xi), cst(zeta), cst(dec), cst(gain)],
        out_specs=[tok, st],
        out_shape=[jax.ShapeDtypeStruct((b, l, RET_WIDTH), F32),
                   jax.ShapeDtypeStruct((b, RET_HEADS, RET_DK, RET_DV), F32)],
        scratch_shapes=[pltpu.VMEM((bb, RET_HEADS, RET_DK, RET_DV), F32)],
        compiler_params=_cparams(("arbitrary", "arbitrary")),
        name="retention",
    )(rq, rk, rv, rg, r0, dmat, xi, zeta, dec, gain)


def _rglru_kernel(x_ref, gate_ref, buf_ref, h0_ref, cw_ref, cb_ref, wai_ref, bai_ref, lam_ref, gain_ref,
                  o_ref, conv_ref, hl_ref, xin_sc, a_sc, b_sc, h_sc, hc_sc):
    ci = pl.program_id(1)
    tc = x_ref.shape[1]
    pad = SUBLANES - (CONV_W - 1)

    @pl.when(ci == 0)
    def _():
        xin_sc[pad:SUBLANES, :] = buf_ref[0]
        hc_sc[...] = h0_ref[0]

    xin_sc[SUBLANES:SUBLANES + tc, :] = x_ref[0]
    xc = cb_ref[...]
    for kk in range(CONV_W):
        xc = xc + xin_sc[pad + kk:pad + kk + tc, :] * cw_ref[kk:kk + 1, :]
    tail = xin_sc[SUBLANES + tc - (CONV_W - 1):SUBLANES + tc, :]
    xin_sc[pad:SUBLANES, :] = tail

    z = jnp.dot(xc.astype(BF16), wai_ref[...], preferred_element_type=F32) + bai_ref[...]
    r = jax.nn.sigmoid(z[:, :LRU_WIDTH])
    ig = jax.nn.sigmoid(z[:, LRU_WIDTH:])
    log_a = -LRU_C * r * jax.nn.softplus(-lam_ref[...])
    a = jnp.exp(log_a)
    a_sc[...] = a
    b_sc[...] = jnp.sqrt(-jnp.tanh(log_a) * (a * a + 1.0)) * ig * xc

    def body(t, h):
        h = a_sc[pl.ds(t, 1), :] * h + b_sc[pl.ds(t, 1), :]
        h_sc[pl.ds(t, 1), :] = h
        return h

    h_last = lax.fori_loop(0, tc, body, hc_sc[...], unroll=8)
    hc_sc[...] = h_last
    y = h_sc[...] * jax.nn.gelu(gate_ref[0])
    o_ref[0] = _rms(y, gain_ref[...])

    @pl.when(ci == pl.num_programs(1) - 1)
    def _():
        conv_ref[0] = tail
        hl_ref[0] = h_last


def _rglru(lx, lg, buf0, h0, conv_w, conv_b, wai, bai, lam, gain, tc):
    b, l, w = lx.shape
    tok = pl.BlockSpec((1, tc, w), lambda i, j: (i, j, 0))
    cst = lambda a: pl.BlockSpec(a.shape, lambda i, j: (0,) * a.ndim)
    bufs = pl.BlockSpec((1, CONV_W - 1, w), lambda i, j: (i, 0, 0))
    hs = pl.BlockSpec((1, 1, w), lambda i, j: (i, 0, 0))
    return pl.pallas_call(
        _rglru_kernel,
        grid=(b, l // tc),
        in_specs=[tok, tok, bufs, hs, cst(conv_w), cst(conv_b), cst(wai), cst(bai), cst(lam), cst(gain)],
        out_specs=[tok, bufs, hs],
        out_shape=[jax.ShapeDtypeStruct((b, l, w), F32),
                   jax.ShapeDtypeStruct((b, CONV_W - 1, w), F32),
                   jax.ShapeDtypeStruct((b, 1, w), F32)],
        scratch_shapes=[pltpu.VMEM((SUBLANES + tc, w), F32), pltpu.VMEM((tc, w), F32), pltpu.VMEM((tc, w), F32),
                        pltpu.VMEM((tc, w), F32), pltpu.VMEM((1, w), F32)],
        compiler_params=_cparams(("arbitrary", "arbitrary")),
        name="rglru",
    )(lx, lg, buf0, h0, conv_w, conv_b, wai, bai, lam, gain)


def _softmax_update(s, v, m_sc, l_sc, acc_sc, rows=slice(None)):
    n = s.shape[1]
    m_prev = m_sc[rows, :]
    m_new = jnp.maximum(m_prev, jnp.max(s, axis=-1, keepdims=True))
    corr = jnp.exp(m_prev - m_new)
    p = jnp.exp(s - jnp.concatenate([m_new] * (n // LANES), axis=1))
    l_sc[rows, :] = l_sc[rows, :] * corr + jnp.sum(p, axis=-1, keepdims=True)
    acc_sc[rows, :] = acc_sc[rows, :] * corr + jnp.dot(p.astype(BF16), v, preferred_element_type=F32)
    m_sc[rows, :] = m_new


def _attn_finish(o, wuv_ref, gain_ref, rows_per_head):
    o = o.astype(BF16)
    om = None
    for hd in range(MLA_HEADS):
        part = jnp.dot(o[hd * rows_per_head:(hd + 1) * rows_per_head], wuv_ref[hd], preferred_element_type=F32)
        om = part if om is None else om + part
    return _rms(om, gain_ref[...])


def _flash_kernel(q_ref, k_ref, wuv_ref, gain_ref, o_ref, m_sc, l_sc, acc_sc, *, qb, kb):
    qi = pl.program_id(1)
    r = MLA_HEADS * qb
    m_sc[...] = jnp.full((r, LANES), -jnp.inf, F32)
    l_sc[...] = jnp.zeros((r, LANES), F32)
    acc_sc[...] = jnp.zeros((r, LANES), F32)
    q0 = qi * qb
    nfull = q0 // kb
    halves = [slice(i * (r // 2), (i + 1) * (r // 2)) for i in range(2)]
    q = q_ref[0].reshape(r, QCAT)
    qs = [q[h] for h in halves]

    def step(start, width, masked):
        kblk = k_ref[0, pl.ds(start, width), :]
        v = kblk[:, :KV_RANK]
        for h, qh in zip(halves, qs):
            s = lax.dot_general(qh, kblk, (((1,), (1,)), ((), ())), preferred_element_type=F32)
            if masked:
                row = lax.broadcasted_iota(jnp.int32, s.shape, 0)
                col = lax.broadcasted_iota(jnp.int32, s.shape, 1)
                s = jnp.where(start + col <= q0 + row % qb, s, -jnp.inf)
            _softmax_update(s, v, m_sc, l_sc, acc_sc, h)

    def body(j, carry):
        step(pl.multiple_of(j * kb, kb), kb, False)
        return carry

    lax.fori_loop(0, nfull, body, 0)
    half = kb // 2
    base = pl.multiple_of(nfull * kb, kb)
    upper = (q0 - base) >= half

    @pl.when(upper)
    def _():
        step(base, half, False)

    step(pl.multiple_of(base + jnp.where(upper, half, 0), half), half, True)
    o_ref[0] = _attn_finish(acc_sc[...] / l_sc[...], wuv_ref, gain_ref, qb)


def _mla_prompt(qcat, kcat, wuv_exp, gain, qb, kb):
    b, _, s, _ = qcat.shape
    r = MLA_HEADS * qb
    return pl.pallas_call(
        functools.partial(_flash_kernel, qb=qb, kb=kb),
        grid=(b, s // qb),
        in_specs=[pl.BlockSpec((1, MLA_HEADS, qb, QCAT), lambda i, j: (i, 0, j, 0)),
                  pl.BlockSpec((1, s, QCAT), lambda i, j: (i, 0, 0)),
                  pl.BlockSpec(wuv_exp.shape, lambda i, j: (0, 0, 0)),
                  pl.BlockSpec(gain.shape, lambda i, j: (0, 0))],
        out_specs=pl.BlockSpec((1, qb, MLA_WIDTH), lambda i, j: (i, j, 0)),
        out_shape=jax.ShapeDtypeStruct((b, s, MLA_WIDTH), F32),
        scratch_shapes=[pltpu.VMEM((r, LANES), F32)] * 3,
        compiler_params=_cparams(("arbitrary", "arbitrary")),
        name="mla_prompt",
    )(qcat, kcat, wuv_exp, gain)


def _mla_sample_kernel(pt_ref, q_ref, knew_ref, ck_hbm, kp_hbm, wuv_ref, gain_ref, o_ref, ckbuf, kpbuf, sems,
                       *, li, n_pages, page, t_new, chunk):
    bi = pl.program_id(0)
    nb = pl.num_programs(0)
    r = MLA_HEADS * t_new

    def page_copies(pg_of, slot):
        out = []
        for j in range(n_pages):
            pg = pg_of(j)
            out.append(pltpu.make_async_copy(ck_hbm.at[li, pg], ckbuf.at[slot, pl.ds(j * page, page), :],
                                             sems.at[0, slot]))
            out.append(pltpu.make_async_copy(kp_hbm.at[li, pg], kpbuf.at[slot, :, pl.ds(j * page, page)],
                                             sems.at[1, slot]))
        return out

    @pl.when(bi == 0)
    def _():
        for c in page_copies(lambda j: pt_ref[0, j], 0):
            c.start()

    @pl.when(bi + 1 < nb)
    def _():
        for c in page_copies(lambda j: pt_ref[bi + 1, j], (bi + 1) % 2):
            c.start()

    slot = bi % 2
    q = q_ref[0].reshape(r, QCAT)
    qb = q.astype(BF16)
    q_lat = qb[:, :KV_RANK]
    q_rot = q[:, KV_RANK:].astype(F32)
    q_r = q_rot[:, :MLA_ROPE]
    for i in range(1, LANES // MLA_ROPE):
        q_r = q_r + q_rot[:, i * MLA_ROPE:(i + 1) * MLA_ROPE]
    q_r = q_r.astype(BF16)

    kn = knew_ref[0].astype(BF16)
    s = lax.dot_general(qb, kn, (((1,), (1,)), ((), ())), preferred_element_type=F32)
    row = lax.broadcasted_iota(jnp.int32, s.shape, 0)
    col = lax.broadcasted_iota(jnp.int32, s.shape, 1)
    s = jnp.where(row % t_new >= col, s, -jnp.inf)
    m = jnp.max(s, axis=-1, keepdims=True)
    p = jnp.exp(s - m)
    l = jnp.sum(p, axis=-1, keepdims=True)
    acc = jnp.dot(p.astype(BF16), kn[:, :KV_RANK], preferred_element_type=F32)

    for c in page_copies(lambda j: 0, slot):
        c.wait()

    for c in range(n_pages * page // chunk):
        ck = ckbuf[slot, pl.ds(c * chunk, chunk), :].astype(BF16)
        kp = kpbuf[slot, :, pl.ds(c * chunk, chunk)].astype(BF16)
        s = (lax.dot_general(q_lat, ck, (((1,), (1,)), ((), ())), preferred_element_type=F32)
             + jnp.dot(q_r, kp, preferred_element_type=F32))
        m_new = jnp.maximum(m, jnp.max(s, axis=-1, keepdims=True))
        corr = jnp.exp(m - m_new)
        p = jnp.exp(s - m_new)
        l = l * corr + jnp.sum(p, axis=-1, keepdims=True)
        acc = acc * corr + jnp.dot(p.astype(BF16), ck, preferred_element_type=F32)
        m = m_new
    o_ref[0] = _attn_finish(acc / l, wuv_ref, gain_ref, t_new)


def _mla_sample(qcat, kcat, cache_ckv, cache_kr_t, page_table, li, wuv_exp, gain):
    b, _, t, _ = qcat.shape
    n_pages = page_table.shape[1]
    page, rank = cache_ckv.shape[2], cache_ckv.shape[3]
    rope_w = cache_kr_t.shape[2]
    chunk = min(8192, n_pages * page)
    grid_spec = pltpu.PrefetchScalarGridSpec(
        num_scalar_prefetch=1,
        grid=(b,),
        in_specs=[pl.BlockSpec((1, MLA_HEADS, t, QCAT), lambda bi, pt: (bi, 0, 0, 0)),
                  pl.BlockSpec((1, t, QCAT), lambda bi, pt: (bi, 0, 0)),
                  pl.BlockSpec(memory_space=pl.ANY),
                  pl.BlockSpec(memory_space=pl.ANY),
                  pl.BlockSpec(wuv_exp.shape, lambda bi, pt: (0, 0, 0)),
                  pl.BlockSpec(gain.shape, lambda bi, pt: (0, 0))],
        out_specs=pl.BlockSpec((1, t, MLA_WIDTH), lambda bi, pt: (bi, 0, 0)),
        scratch_shapes=[pltpu.VMEM((2, n_pages * page, rank), F32),
                        pltpu.VMEM((2, rope_w, n_pages * page), F32),
                        pltpu.SemaphoreType.DMA((2, 2))],
    )
    return pl.pallas_call(
        functools.partial(_mla_sample_kernel, li=li, n_pages=n_pages, page=page, t_new=t, chunk=chunk),
        grid_spec=grid_spec,
        out_shape=jax.ShapeDtypeStruct((b, t, MLA_WIDTH), F32),
        compiler_params=_cparams(("arbitrary",)),
        name="mla_sample",
    )(page_table, qcat, kcat, cache_ckv, cache_kr_t, wuv_exp, gain)


def _outproj_kernel(ret_ref, lru_ref, mla_ref, x_ref, w_ref, g1_ref, gt1_ref, g2_ref, sc2_ref, sh2_ref,
                    x1_ref, h2_ref, h2p_ref):
    bt, lt, d = x_ref.shape
    rows = bt * lt
    mix = jnp.concatenate([ret_ref[...].reshape(rows, RET_WIDTH), lru_ref[...].reshape(rows, LRU_WIDTH),
                           mla_ref[...].reshape(rows, MLA_WIDTH)], axis=1).astype(BF16)
    y = jnp.dot(mix, w_ref[...], preferred_element_type=F32).reshape(bt, lt, d)
    x1 = x_ref[...] + gt1_ref[...] * _rms(y, g1_ref[...])
    x1_ref[...] = x1
    h2 = _rms(x1, g2_ref[...]) * (1.0 + sc2_ref[...]) + sh2_ref[...]
    h2_ref[...] = h2
    _store_packed(h2p_ref, h2.reshape(rows, d))


def _outproj(ret, lru, mla, x, w_out, g1, gt1, g2, sc2, sh2, bt, lt):
    b, l, d = x.shape
    tok = lambda w: pl.BlockSpec((bt, lt, w), lambda i, j: (i, j, 0))
    mod = pl.BlockSpec((bt, 1, d), lambda i, j: (i, 0, 0))
    gsp = pl.BlockSpec((1, 1, d), lambda i, j: (0, 0, 0))
    return pl.pallas_call(
        _outproj_kernel,
        grid=(b // bt, l // lt),
        in_specs=[tok(RET_WIDTH), tok(LRU_WIDTH), tok(MLA_WIDTH), tok(d),
                  pl.BlockSpec(w_out.shape, lambda i, j: (0, 0)), gsp, mod, gsp, mod, mod],
        out_specs=[tok(d), tok(d),
                   pl.BlockSpec((bt * lt * PACK_ROWS, LANES), lambda i, j: (i * (l // lt) + j, 0))],
        out_shape=[jax.ShapeDtypeStruct((b, l, d), F32)] * 2
        + [jax.ShapeDtypeStruct((b * l * PACK_ROWS, LANES), jnp.int32)],
        compiler_params=_cparams(("arbitrary", "arbitrary")),
        name="outproj",
    )(ret, lru, mla, x, w_out, g1, gt1, g2, sc2, sh2)


def _split_dot_t(w, x):
    dn = (((1,), (1,)), ((), ()))
    wh = w.astype(BF16)
    wl = (w - wh.astype(F32)).astype(BF16)
    xh = x.astype(BF16)
    xl = (x - xh.astype(F32)).astype(BF16)
    return (lax.dot_general(wh, xh, dn, preferred_element_type=F32)
            + lax.dot_general(wh, xl, dn, preferred_element_type=F32)
            + lax.dot_general(wl, xh, dn, preferred_element_type=F32))


def _router_kernel(x_ref, w_ref, b_ref, upper_ref, lower_ref, eid_ref, pos_ref, gate_ref, cnt_ref, cnt_sc):
    ti = pl.program_id(0)
    tm = x_ref.shape[0]

    @pl.when(ti == 0)
    def _():
        cnt_sc[...] = jnp.zeros(cnt_sc.shape, F32)

    logits = _split_dot_t(w_ref[...], x_ref[...])
    scores = jax.nn.sigmoid(logits)
    biased = scores + b_ref[...]
    groups = [biased[g * GROUP_SIZE:(g + 1) * GROUP_SIZE, :] for g in range(N_GROUPS)]
    sub = lax.broadcasted_iota(jnp.int32, (GROUP_SIZE, tm), 0)
    gs = []
    for bg in groups:
        m1 = jnp.max(bg, axis=0, keepdims=True)
        i1 = jnp.min(jnp.where(bg == m1, sub, GROUP_SIZE), axis=0, keepdims=True)
        m2 = jnp.max(jnp.where(sub == i1, -jnp.inf, bg), axis=0, keepdims=True)
        gs.append(m1 + m2)
    masked = []
    for g in range(N_GROUPS):
        rank = jnp.zeros((1, tm), F32)
        for g2 in range(N_GROUPS):
            if g2 != g:
                ahead = (gs[g2] >= gs[g]) if g2 < g else (gs[g2] > gs[g])
                rank = rank + jnp.where(ahead, 1.0, 0.0)
        masked.append(jnp.where(rank < TOPK_GROUPS, groups[g], -jnp.inf))
    sels = []
    for g in range(N_GROUPS):
        mine = masked[g]
        rank = jnp.zeros((GROUP_SIZE, tm), F32)
        for g2 in range(N_GROUPS):
            for s2 in range(GROUP_SIZE):
                other = jnp.broadcast_to(masked[g2][s2:s2 + 1, :], (GROUP_SIZE, tm))
                if g2 < g:
                    rank = rank + jnp.where(other >= mine, 1.0, 0.0)
                elif g2 > g:
                    rank = rank + jnp.where(other > mine, 1.0, 0.0)
                else:
                    tie = jnp.where(sub > s2, 1.0, 0.0)
                    rank = rank + jnp.where(other > mine, 1.0, jnp.where(other == mine, tie, 0.0))
        sels.append(jnp.where(rank < TOP_K, 1.0, 0.0))
    self32 = jnp.concatenate(sels, axis=0)
    sel = self32 > 0.5
    picked = jnp.where(sel, scores, 0.0)
    gate = picked / jnp.sum(picked, axis=0, keepdims=True) * ROUTED_SCALE
    selb = self32.astype(BF16)
    pos = jnp.dot(selb, upper_ref[...], preferred_element_type=F32) + cnt_sc[:, :1]
    cnt_sc[...] = cnt_sc[...] + jnp.sum(self32, axis=1, keepdims=True)
    slot = jnp.dot(lower_ref[...], selb, preferred_element_type=F32)
    efl = lax.broadcasted_iota(jnp.int32, (N_EXPERTS, tm), 0).astype(F32)
    for kk in range(TOP_K):
        hit = sel & (slot == float(kk))
        eid_ref[kk:kk + 1, :] = jnp.sum(jnp.where(hit, efl, 0.0), axis=0, keepdims=True).astype(jnp.int32)
        pos_ref[kk:kk + 1, :] = jnp.sum(jnp.where(hit, pos, 0.0), axis=0, keepdims=True).astype(jnp.int32)
        gate_ref[kk:kk + 1, :] = jnp.sum(jnp.where(hit, gate, 0.0), axis=0, keepdims=True)

    @pl.when(ti == pl.num_programs(0) - 1)
    def _():
        cnt_ref[...] = cnt_sc[...]


def _router(h2, w_rt, b_r, tm):
    t, d = h2.shape
    upper = jnp.triu(jnp.ones((tm, tm), F32), 1).astype(BF16)
    lower = jnp.tril(jnp.ones((N_EXPERTS, N_EXPERTS), F32), -1).astype(BF16)
    kt = pl.BlockSpec((TOP_K, tm), lambda i: (0, i))
    cst = lambda a: pl.BlockSpec(a.shape, lambda i: (0,) * a.ndim)
    return pl.pallas_call(
        _router_kernel,
        grid=(t // tm,),
        in_specs=[pl.BlockSpec((tm, d), lambda i: (i, 0)), cst(w_rt), cst(b_r), cst(upper), cst(lower)],
        out_specs=[kt, kt, kt, pl.BlockSpec((N_EXPERTS, LANES), lambda i: (0, 0))],
        out_shape=[jax.ShapeDtypeStruct((TOP_K, t), jnp.int32), jax.ShapeDtypeStruct((TOP_K, t), jnp.int32),
                   jax.ShapeDtypeStruct((TOP_K, t), F32), jax.ShapeDtypeStruct((N_EXPERTS, LANES), F32)],
        scratch_shapes=[pltpu.VMEM((N_EXPERTS, LANES), F32)],
        compiler_params=_cparams(("arbitrary",)),
        name="router",
    )(h2, w_rt, b_r, upper, lower)


PACK_ROWS = 4


def _store_packed(o_ref, x):
    r = x.shape[0]
    bits = lax.bitcast_convert_type(x.astype(BF16).astype(F32), jnp.int32)
    for j in range(PACK_ROWS):
        lo = lax.shift_right_logical(bits[:, j * LANES:(j + 1) * LANES], jnp.int32(16))
        hi = bits[:, (j + PACK_ROWS) * LANES:(j + PACK_ROWS + 1) * LANES] & jnp.int32(-65536)
        o_ref[pl.ds(j, r, stride=PACK_ROWS), :] = hi | lo


def _unpack_words(w):
    lo = lax.bitcast_convert_type(lax.shift_left(w, jnp.int32(16)), F32)
    hi = lax.bitcast_convert_type(w & jnp.int32(-65536), F32)
    return lo, hi


def _dispatch_kernel(dest_ref, x_ref, rows_in_ref, rows_ref, sem):
    del rows_in_ref
    tm = x_ref.shape[0] // PACK_ROWS

    def row_copy(t, d):
        return pltpu.make_async_copy(x_ref.at[pl.ds(pl.multiple_of(t * PACK_ROWS, PACK_ROWS), PACK_ROWS), :],
                                     rows_ref.at[pl.ds(pl.multiple_of(d * PACK_ROWS, PACK_ROWS), PACK_ROWS), :], sem)

    def issue(t, c):
        for kk in range(TOP_K):
            row_copy(t, dest_ref[t * TOP_K + kk]).start()
        return c

    def drain(t, c):
        for kk in range(TOP_K):
            row_copy(0, 0).wait()
        return c

    lax.fori_loop(0, tm, issue, 0)
    lax.fori_loop(0, tm, drain, 0)


def _dispatch(dest_flat, h2p, rows0, tm):
    t = h2p.shape[0] // PACK_ROWS
    return pl.pallas_call(
        _dispatch_kernel,
        grid=(t // tm,),
        in_specs=[pl.BlockSpec((tm * TOP_K,), lambda i: (i,), memory_space=pltpu.SMEM),
                  pl.BlockSpec((tm * PACK_ROWS, LANES), lambda i: (i, 0)),
                  pl.BlockSpec(memory_space=pl.ANY)],
        out_specs=pl.BlockSpec(memory_space=pl.ANY),
        out_shape=jax.ShapeDtypeStruct(rows0.shape, jnp.int32),
        scratch_shapes=[pltpu.SemaphoreType.DMA(())],
        input_output_aliases={2: 0},
        compiler_params=_cparams(("arbitrary",)),
        name="moe_dispatch",
    )(dest_flat, h2p, rows0)


def _expert_kernel(be_ref, nu_ref, rows_ref, wgu_ref, wd_ref, o_ref, wgu_sc, wd_sc):
    i = pl.program_id(0)
    prev = be_ref[jnp.maximum(i - 1, 0)]
    fresh = (i == 0) | (be_ref[i] != prev)

    @pl.when(fresh)
    def _():
        wgu_sc[...] = wgu_ref[...].astype(BF16)
        wd_sc[...] = wd_ref[...].astype(BF16)

    @pl.when(i < nu_ref[0])
    def _():
        blk = rows_ref.shape[0] // PACK_ROWS
        halves = [_unpack_words(rows_ref[pl.ds(j, blk, stride=PACK_ROWS), :]) for j in range(PACK_ROWS)]
        x = jnp.concatenate([h[0] for h in halves] + [h[1] for h in halves], axis=1).astype(BF16)
        gu = jnp.dot(x, wgu_sc[...], preferred_element_type=F32)
        act = _silu(gu[:, :D_EXPERT]) * gu[:, D_EXPERT:]
        y = jnp.dot(act.astype(BF16), wd_sc[...], preferred_element_type=F32)
        _store_packed(o_ref, y)


def _experts(block_e, n_used, rows, w_gu, w_down, li, blk):
    n_rows = rows.shape[0] // PACK_ROWS
    n_blocks = n_rows // blk
    d, de2 = w_gu.shape[-2:]
    de = w_down.shape[-2]

    def rmap(i, be, nu):
        return (jnp.minimum(i, nu[0] - 1), 0)

    grid_spec = pltpu.PrefetchScalarGridSpec(
        num_scalar_prefetch=2,
        grid=(n_blocks,),
        in_specs=[pl.BlockSpec((blk * PACK_ROWS, LANES), rmap),
                  pl.BlockSpec((None, None, d, de2), lambda i, be, nu: (li, be[i], 0, 0)),
                  pl.BlockSpec((None, None, de, d), lambda i, be, nu: (li, be[i], 0, 0))],
        out_specs=pl.BlockSpec((blk * PACK_ROWS, LANES), rmap),
        scratch_shapes=[pltpu.VMEM((d, de2), BF16), pltpu.VMEM((de, d), BF16)],
    )
    return pl.pallas_call(
        _expert_kernel,
        grid_spec=grid_spec,
        out_shape=jax.ShapeDtypeStruct(rows.shape, jnp.int32),
        compiler_params=_cparams(("arbitrary",)),
        name="moe_experts",
    )(block_e, n_used, rows, w_gu, w_down)


def _combine_kernel(dest_ref, dnext_ref, gate_ref, orow_ref, h2_ref, x1_ref, wsgu_ref, wsd_ref, g3_ref, gt2_ref,
                    y_ref, buf, sems):
    bt, lt, d = x1_ref.shape
    tm = bt * lt
    ns = PACK_ROWS
    step = pl.program_id(0) * pl.num_programs(1) + pl.program_id(1)
    nsteps = pl.num_programs(0) * pl.num_programs(1)
    slot = step % 2

    def row_copy(src_row, t, kk, sl):
        return pltpu.make_async_copy(orow_ref.at[pl.ds(pl.multiple_of(src_row * ns, ns), ns), :],
                                     buf.at[sl, kk, pl.ds(pl.multiple_of(t * ns, ns), ns), :],
                                     sems.at[sl])

    def issue_tile(dref, sl):
        def issue(t, c):
            for kk in range(TOP_K):
                row_copy(dref[t * TOP_K + kk], t, kk, sl).start()
            return c
        lax.fori_loop(0, tm, issue, 0)

    @pl.when(step == 0)
    def _():
        issue_tile(dest_ref, 0)

    @pl.when(step + 1 < nsteps)
    def _():
        issue_tile(dnext_ref, 1 - slot)

    hb = h2_ref[...].astype(BF16)
    gu = jnp.dot(hb, wsgu_ref[...], preferred_element_type=F32)
    ds = wsd_ref.shape[0]
    f = jnp.dot((_silu(gu[:, :ds]) * gu[:, ds:]).astype(BF16), wsd_ref[...], preferred_element_type=F32)

    def drain(t, c):
        for kk in range(TOP_K):
            row_copy(0, 0, 0, slot).wait()
        return c

    lax.fori_loop(0, tm, drain, 0)
    gate = gate_ref[...]
    fs = [f[:, j * LANES:(j + 1) * LANES] for j in range(2 * ns)]
    for kk in range(TOP_K):
        gk = jnp.broadcast_to(gate[:, kk:kk + 1], (tm, LANES))
        for j in range(ns):
            lo, hi = _unpack_words(buf[slot, kk, pl.ds(j, tm, stride=ns), :])
            fs[j] = fs[j] + lo * gk
            fs[j + ns] = fs[j + ns] + hi * gk
    f = jnp.concatenate(fs, axis=1)
    y_ref[...] = x1_ref[...] + gt2_ref[...] * _rms(f, g3_ref[...]).reshape(bt, lt, d)


def _combine(dest_flat, gate_tk, out_rows, h2, x1, ws_gu, ws_down, g3, gt2, bt, lt):
    b, l, d = x1.shape
    tm = bt * lt
    nl = l // lt
    nsteps = (b // bt) * nl
    tok = pl.BlockSpec((bt, lt, d), lambda i, j: (i, j, 0))
    flat = lambda w: pl.BlockSpec((tm, w), lambda i, j: (i * nl + j, 0))
    return pl.pallas_call(
        _combine_kernel,
        grid=(b // bt, nl),
        in_specs=[pl.BlockSpec((tm * TOP_K,), lambda i, j: (i * nl + j,), memory_space=pltpu.SMEM),
                  pl.BlockSpec((tm * TOP_K,), lambda i, j: (jnp.minimum(i * nl + j + 1, nsteps - 1),),
                               memory_space=pltpu.SMEM),
                  flat(TOP_K), pl.BlockSpec(memory_space=pl.ANY), flat(d), tok,
                  pl.BlockSpec(ws_gu.shape, lambda i, j: (0, 0)), pl.BlockSpec(ws_down.shape, lambda i, j: (0, 0)),
                  pl.BlockSpec((1, 1, d), lambda i, j: (0, 0, 0)),
                  pl.BlockSpec((bt, 1, d), lambda i, j: (i, 0, 0))],
        out_specs=tok,
        out_shape=jax.ShapeDtypeStruct((b, l, d), F32),
        scratch_shapes=[pltpu.VMEM((2, TOP_K, tm * PACK_ROWS, LANES), jnp.int32), pltpu.SemaphoreType.DMA((2,))],
        compiler_params=_cparams(("arbitrary", "arbitrary")),
        name="moe_combine",
    )(dest_flat, dest_flat, gate_tk, out_rows, h2, x1, ws_gu, ws_down, g3, gt2)


def _moe(groups, g3, w_rt, b_r, w_exp_gu, w_exp_down, ws_gu, ws_down, li, blk, rows_init):
    d = groups[0][0].shape[-1]
    sizes = [g[0].shape[0] * g[0].shape[1] for g in groups]
    t = sum(sizes)
    flats = [g[0].reshape(n, d) for g, n in zip(groups, sizes)]
    h2f = flats[0] if len(flats) == 1 else jnp.concatenate(flats, axis=0)
    h2p = groups[0][1] if len(groups) == 1 else jnp.concatenate([g[1] for g in groups], axis=0)
    eid, pos, gate, cnt = _router(h2f, w_rt, b_r, _largest_tile(t, 512))
    counts = cnt[:, 0].astype(jnp.int32)
    padded = (counts + blk - 1) // blk * blk
    pends = jnp.cumsum(padded)
    pstart = pends - padded
    n_blocks = -(-(t * TOP_K) // blk) + N_EXPERTS
    starts = jnp.arange(n_blocks, dtype=jnp.int32) * blk
    block_e = jnp.minimum(jnp.sum((pends[None, :] <= starts[:, None]).astype(jnp.int32), axis=1), N_EXPERTS - 1)
    n_used = (pends[-1:] // blk).astype(jnp.int32)
    first_row = jnp.sum(jnp.where(eid[:, :, None] == jnp.arange(N_EXPERTS, dtype=jnp.int32), pstart, 0), axis=-1)
    dest = (first_row + pos).T.reshape(t * TOP_K)
    gate_tk = gate.T
    if rows_init is None:
        rows_init = jnp.zeros((n_blocks * blk * PACK_ROWS, LANES), jnp.int32)
    rows = _dispatch(dest, h2p, rows_init, _largest_tile(t, 512))
    out_rows = _experts(block_e, n_used, rows, w_exp_gu, w_exp_down, li, blk)
    outs, off = [], 0
    for (h2, _, x1, gt2, bt, lt), n, flat in zip(groups, sizes, flats):
        outs.append(_combine(dest[off * TOP_K:(off + n) * TOP_K], gate_tk[off:off + n], out_rows, flat, x1,
                             ws_gu, ws_down, g3, gt2, bt, lt))
        off += n
    return outs, rows


def _permute_w_in(w_in):
    dp, d, _ = w_in.shape
    hd = MLA_NOPE + MLA_ROPE
    mq = w_in[:, :, C_QN:C_QN + MLA_HEADS * hd].reshape(dp, d, MLA_HEADS, hd)
    qn = mq[..., :MLA_NOPE].reshape(dp, d, MLA_HEADS * MLA_NOPE)
    qr = mq[..., MLA_NOPE:].reshape(dp, d, MLA_HEADS * MLA_ROPE)
    mc0 = C_QN + MLA_HEADS * hd
    mc = w_in[:, :, mc0:mc0 + KV_RANK]
    mk = w_in[:, :, mc0 + KV_RANK:mc0 + KV_RANK + MLA_ROPE]
    return jnp.concatenate([w_in[:, :, :C_QN], qn, qr, mc] + [mk] * (LANES // MLA_ROPE), axis=-1).astype(BF16)


def _rope_tables(pos, dim):
    half = dim // 2
    inv = ROPE_BASE ** (-jnp.arange(0, dim, 2, dtype=F32) / dim)
    ang = pos.astype(F32)[:, None] * inv[None, :]
    cos, sin = jnp.cos(ang), jnp.sin(ang)
    reps = LANES // dim
    cos_t = jnp.tile(jnp.concatenate([cos, cos], axis=1), (1, reps))
    sin_t = jnp.tile(jnp.concatenate([-sin, sin], axis=1), (1, reps))
    return cos_t, sin_t


def _block_diag(blocks):
    n, a, b = blocks.shape
    eye = jnp.eye(n, dtype=blocks.dtype)
    return (eye[:, None, :, None] * blocks[:, :, None, :]).reshape(n * a, n * b)


def _mixer(x, mod, pos_tabs, li, p, past, tiles):
    b, l, d = x.shape
    bt, lt = tiles['bt'], tiles['lt']
    sh1, sc1, gt1, sh2, sc2, gt2 = [m[:, None, :] for m in jnp.split(mod, 6, axis=-1)]
    g = p['norm_gains']
    gains = [g[i][None, None, :] for i in range(4)]
    gain = p['grp_gain']
    rq, rk, rv, rg, lx, lg, qcat, kcat, ckv, kr = _inproj(
        x, sc1, sh1, gains[0], p['w_in_p'], p['wuk_bd'], p['kv_norm'][None, :], pos_tabs, bt, lt, tiles['qdtype'])
    if past is None:
        r0 = jnp.zeros((b, RET_HEADS, RET_DK, RET_DV), F32)
        buf0 = jnp.zeros((b, CONV_W - 1, LRU_WIDTH), F32)
        h0 = jnp.zeros((b, 1, LRU_WIDTH), F32)
    else:
        r0, buf0, h0 = past['ret'], past['conv'], past['lru'][:, None, :]
    ret_out, r_new = _retention(rq, rk, rv, rg, r0, gain[None, :RET_WIDTH], tiles['rbb'])
    lru_out, conv_new, h_new = _rglru(
        lx, lg, buf0, h0, p['conv_w'], p['conv_b'][None, :], p['wai'], p['bai'], p['lru_lambda'][None, :],
        gain[None, RET_WIDTH:RET_WIDTH + LRU_WIDTH], tiles['tc'])
    mla_gain = gain[None, RET_WIDTH + LRU_WIDTH:]
    if past is None:
        mla_out = _mla_prompt(qcat, kcat, p['wuv_exp'], mla_gain, tiles['qb'], tiles['kb'])
    else:
        mla_out = _mla_sample(qcat, kcat, past['cache_ckv'], past['cache_kr_t'], past['page_table'], li,
                              p['wuv_exp'], mla_gain)
    x1, h2, h2p = _outproj(ret_out, lru_out, mla_out, x, p['w_out'], gains[1], gt1, gains[2], sc2, sh2, bt, lt)
    return (h2, h2p, x1, gt2, tiles['cbt'], tiles['clt']), (ckv, kr, r_new, conv_new, h_new[:, 0])


def _largest_tile(n, cap):
    t = min(n, cap)
    while n % t:
        t //= 2
    return t


def kernel(x_prompt, x_sample, cache_ckv, cache_krope, state_ret, state_conv, state_lru, page_table, c_prompt, c_sample, w_ada, b_ada, norm_gains, w_in, w_out, grp_gain, conv_w, conv_b, w_lru_a, b_lru_a, w_lru_i, b_lru_i, lru_lambda, kv_norm, w_ukv, w_router, b_router, w_exp_gu, w_exp_down, w_sh_gu, w_sh_down):
    depth = w_in.shape[0]
    bp, lp, d = x_prompt.shape
    bs, ls, _ = x_sample.shape
    past_len = page_table.shape[1] * cache_ckv.shape[2]

    n_c = bp + bs
    n_cp = -(-n_c // SUBLANES) * SUBLANES
    c_all = jnp.concatenate([c_prompt, c_sample, jnp.zeros((n_cp - n_c, d), F32)], axis=0)
    mod_all = _adaln(c_all, w_ada, b_ada[:, None, :])

    w_in_p = _permute_w_in(w_in)
    w_uk = w_ukv[..., :MLA_NOPE]
    w_uv = w_ukv[..., MLA_NOPE:]
    eye_h = jnp.eye(MLA_HEADS, dtype=F32)

    pos_p = jnp.arange(lp)
    pos_s = past_len + jnp.arange(ls)
    tabs_p = _rope_tables(pos_p, RET_DK) + _rope_tables(pos_p, MLA_ROPE)
    tabs_s = _rope_tables(pos_s, RET_DK) + _rope_tables(pos_s, MLA_ROPE)

    lt_p = _largest_tile(lp, 512)
    kb_p = _largest_tile(lp, 1024)
    tiles_p = dict(bt=1, lt=lt_p, rbb=1, qdtype=BF16, tc=_largest_tile(lp, 512), qb=_largest_tile(lp, min(256, kb_p // 2)),
                   kb=kb_p, cbt=1, clt=_largest_tile(lp, 128))
    bt_s = _largest_tile(bs, 32)
    tiles_s = dict(bt=bt_s, lt=ls, qdtype=F32, tc=ls, rbb=_largest_tile(bs, 8),
                   cbt=_largest_tile(bs, 32), clt=ls)

    cache_kr_t = jnp.swapaxes(cache_krope, 2, 3)
    y_p, y_s = x_prompt, x_sample
    blk = 256 if bp * lp + bs * ls >= 4096 else 128
    moe_rows = None
    st_p, st_s = [], []
    for li in range(depth):
        wuk_bd = _block_diag(jnp.transpose(w_uk[li], (1, 2, 0))).astype(BF16)
        wuv_h = jnp.transpose(w_uv[li], (1, 0, 2))
        wuv_exp = (wuv_h[:, :, None, :] * eye_h[:, None, :, None]).reshape(
            MLA_HEADS, KV_RANK, MLA_WIDTH).astype(BF16)
        wai = jnp.concatenate([_block_diag(w_lru_a[li]), _block_diag(w_lru_i[li])], axis=1).astype(BF16)
        bai = jnp.concatenate([b_lru_a[li], b_lru_i[li]])[None, :]
        p = dict(norm_gains=norm_gains[li], w_in_p=w_in_p[li], wuk_bd=wuk_bd, wuv_exp=wuv_exp,
                 w_out=w_out[li].astype(BF16), grp_gain=grp_gain[li], conv_w=conv_w[li], conv_b=conv_b[li],
                 wai=wai, bai=bai, lru_lambda=lru_lambda[li], kv_norm=kv_norm[li],
                 w_rt=w_router[li].T, b_router=b_router[li], w_exp_gu=w_exp_gu, w_exp_down=w_exp_down,
                 ws_gu=w_sh_gu[li].astype(BF16), ws_down=w_sh_down[li].astype(BF16))
        grp_p, sp = _mixer(y_p, mod_all[li, :bp], tabs_p, li, p, None, tiles_p)
        past = dict(ret=state_ret[li], conv=state_conv[li], lru=state_lru[li], cache_ckv=cache_ckv,
                    cache_kr_t=cache_kr_t, page_table=page_table)
        grp_s, ss = _mixer(y_s, mod_all[li, bp:bp + bs], tabs_s, li, p, past, tiles_s)
        (y_p, y_s), moe_rows = _moe([grp_p, grp_s], norm_gains[li][3][None, None, :], p['w_rt'], p['b_router'][:, None],
                                    p['w_exp_gu'], p['w_exp_down'], p['ws_gu'], p['ws_down'], li, blk, moe_rows)
        st_p.append(sp)
        st_s.append(ss)
    stack = lambda sts, i: jnp.stack([s[i] for s in sts])
    return (y_p, y_s) + tuple(stack(st_p, i) for i in range(5)) + tuple(stack(st_s, i) for i in range(5))
```

```python
import functools

import jax
import jax.numpy as jnp
from jax import lax
from jax.experimental import pallas as pl
from jax.experimental.pallas import tpu as pltpu

F32 = jnp.float32
BF16 = jnp.bfloat16
EPS = 1e-6
ROPE_BASE = 10000.0

RET_HEADS = 4
RET_DK = 64
RET_DV = 64
RET_WIDTH = RET_HEADS * RET_DV
RET_CHUNK = 128
LRU_WIDTH = 256
LRU_BLOCKS = 4
LRU_BLOCK = LRU_WIDTH // LRU_BLOCKS
CONV_W = 4
LRU_C = 8.0
MLA_HEADS = 8
MLA_NOPE = 64
MLA_ROPE = 32
MLA_V = 64
KV_RANK = 128
MLA_WIDTH = MLA_HEADS * MLA_V
N_EXPERTS = 64
TOP_K = 8
N_GROUPS = 8
GROUP_SIZE = N_EXPERTS // N_GROUPS
TOPK_GROUPS = 4
D_EXPERT = 256
ROUTED_SCALE = 2.5

LANES = 128
SUBLANES = 8
QCAT = 2 * LANES
C_RQ, C_RK, C_RV, C_RG, C_LX, C_LG = 0, 256, 512, 768, 1024, 1280
C_QN = 1536
C_QR = C_QN + MLA_HEADS * MLA_NOPE
C_MC = C_QR + MLA_HEADS * MLA_ROPE
C_MK = C_MC + KV_RANK
N_PROJ = C_MK + LANES


def _cparams(sem, vmem_mb=48):
    return pltpu.CompilerParams(dimension_semantics=sem, vmem_limit_bytes=vmem_mb * 2 ** 20)


def _rms(x, g):
    return x * lax.rsqrt(jnp.mean(x * x, axis=-1, keepdims=True) + EPS) * g


def _silu(x):
    return x * jax.nn.sigmoid(x)


def _adaln_kernel(c_ref, w_ref, b_ref, o_ref):
    c = c_ref[...]
    s = _silu(c).astype(BF16)
    o_ref[...] = jnp.dot(s, w_ref[...].astype(BF16), preferred_element_type=F32) + b_ref[...]


def _adaln(c, w_ada, b_ada):
    depth, d, n = w_ada.shape
    bp = c.shape[0]
    tn = 768
    return pl.pallas_call(
        _adaln_kernel,
        grid=(depth, n // tn),
        in_specs=[pl.BlockSpec((bp, d), lambda l, j: (0, 0)),
                  pl.BlockSpec((None, d, tn), lambda l, j: (l, 0, j)),
                  pl.BlockSpec((None, 1, tn), lambda l, j: (l, 0, j))],
        out_specs=pl.BlockSpec((None, bp, tn), lambda l, j: (l, 0, j)),
        out_shape=jax.ShapeDtypeStruct((depth, bp, n), F32),
        compiler_params=_cparams(("arbitrary", "arbitrary")),
        name="adaln",
    )(c, w_ada, b_ada)


def _rope_slab(v, cos, sin_signed, half):
    lane = lax.broadcasted_iota(jnp.int32, v.shape, 1)
    first = (lane % (2 * half)) < half
    partner = jnp.where(first, pltpu.roll(v, LANES - half, 1), pltpu.roll(v, half, 1))
    return v * cos + partner * sin_signed


def _rope(v, cos, sin_signed, half):
    n = v.shape[1] // LANES
    outs = [_rope_slab(v[:, i * LANES:(i + 1) * LANES], cos, sin_signed, half) for i in range(n)]
    return outs[0] if n == 1 else jnp.concatenate(outs, axis=1)


def _inproj_kernel(x_ref, sc_ref, sh_ref, g_ref, w_ref, wuk_ref, kvg_ref, c64_ref, s64_ref, c32_ref, s32_ref,
                   rq_ref, rk_ref, rv_ref, rg_ref, lx_ref, lg_ref, qcat_ref, kcat_ref, ckv_ref, kr_ref):
    bt, lt, d = x_ref.shape
    rows = bt * lt
    h = _rms(x_ref[...], g_ref[...]) * (1.0 + sc_ref[...]) + sh_ref[...]
    hb = h.reshape(rows, d).astype(BF16)

    def proj(a, b):
        return jnp.dot(hb, w_ref[:, a:b], preferred_element_type=F32)

    def table(ref):
        t = ref[...]
        if bt > 1:
            t = jnp.broadcast_to(t[None], (bt, lt, LANES)).reshape(rows, LANES)
        return t

    c64, s64, c32, s32 = table(c64_ref), table(s64_ref), table(c32_ref), table(s32_ref)
    rq_ref[...] = _rope(proj(C_RQ, C_RK), c64, s64, RET_DK // 2).reshape(bt, lt, RET_WIDTH)
    rk_ref[...] = _rope(proj(C_RK, C_RV), c64, s64, RET_DK // 2).reshape(bt, lt, RET_WIDTH)
    rv_ref[...] = proj(C_RV, C_RG).reshape(bt, lt, RET_WIDTH)
    rg_ref[...] = proj(C_RG, C_LX).reshape(bt, lt, RET_WIDTH)
    lx_ref[...] = proj(C_LX, C_LG).reshape(bt, lt, LRU_WIDTH)
    lg_ref[...] = proj(C_LG, C_QN).reshape(bt, lt, LRU_WIDTH)

    scale = (MLA_NOPE + MLA_ROPE) ** -0.5
    q_lat = jnp.dot(proj(C_QN, C_QR).astype(BF16), wuk_ref[...], preferred_element_type=F32) * scale
    q_rope = _rope(proj(C_QR, C_MC), c32, s32, MLA_ROPE // 2) * scale
    lane = lax.broadcasted_iota(jnp.int32, (rows, LANES), 1)
    heads_per_slab = LANES // MLA_ROPE
    for hd in range(MLA_HEADS):
        slab = q_rope[:, (hd // heads_per_slab) * LANES:(hd // heads_per_slab + 1) * LANES]
        own = jnp.where(lane // MLA_ROPE == hd % heads_per_slab, slab, 0.0)
        qh = jnp.concatenate([q_lat[:, hd * KV_RANK:(hd + 1) * KV_RANK], own], axis=1)
        qcat_ref[:, hd, :, :] = qh.reshape(bt, lt, QCAT).astype(qcat_ref.dtype)

    ckv = _rms(proj(C_MC, C_MK), kvg_ref[...])
    kr4 = _rope(proj(C_MK, N_PROJ), c32, s32, MLA_ROPE // 2)
    ckv_ref[...] = ckv.reshape(bt, lt, KV_RANK)
    kr_ref[...] = kr4[:, :MLA_ROPE].reshape(bt, lt, MLA_ROPE)
    kcat_ref[...] = jnp.concatenate([ckv, kr4], axis=1).reshape(bt, lt, QCAT).astype(kcat_ref.dtype)


def _inproj(x, sc, sh, g, w_in_p, wuk_bd, kv_gain, tabs, bt, lt, qdtype):
    b, l, d = x.shape
    grid = (b // bt, l // lt)
    c64, s64, c32, s32 = tabs
    tok = lambda w: pl.BlockSpec((bt, lt, w), lambda i, j: (i, j, 0))
    mod = pl.BlockSpec((bt, 1, d), lambda i, j: (i, 0, 0))
    const2 = lambda a: pl.BlockSpec(a.shape, lambda i, j: (0, 0))
    tab = pl.BlockSpec((lt, LANES), lambda i, j: (j, 0))
    out_shape = [jax.ShapeDtypeStruct((b, l, 256), F32)] * 6 + [
        jax.ShapeDtypeStruct((b, MLA_HEADS, l, QCAT), qdtype),
        jax.ShapeDtypeStruct((b, l, QCAT), qdtype),
        jax.ShapeDtypeStruct((b, l, KV_RANK), F32),
        jax.ShapeDtypeStruct((b, l, MLA_ROPE), F32)]
    out_specs = [tok(256)] * 6 + [
        pl.BlockSpec((bt, MLA_HEADS, lt, QCAT), lambda i, j: (i, 0, j, 0)),
        tok(QCAT), tok(KV_RANK), tok(MLA_ROPE)]
    return pl.pallas_call(
        _inproj_kernel,
        grid=grid,
        in_specs=[tok(d), mod, mod, pl.BlockSpec((1, 1, d), lambda i, j: (0, 0, 0)),
                  const2(w_in_p), const2(wuk_bd), const2(kv_gain), tab, tab, tab, tab],
        out_specs=out_specs,
        out_shape=out_shape,
        compiler_params=_cparams(("arbitrary", "arbitrary")),
        name="inproj",
    )(x, sc, sh, g, w_in_p, wuk_bd, kv_gain, c64, s64, c32, s32)


def _retention_kernel(q_ref, k_ref, v_ref, g_ref, r0_ref, dmat_ref, xi_ref, zeta_ref, dec_ref, gain_ref,
                      o_ref, rfin_ref, r_sc):
    ci = pl.program_id(1)

    @pl.when(ci == 0)
    def _():
        r_sc[...] = r0_ref[...]

    for bi in range(q_ref.shape[0]):
        q = q_ref[bi]
        k = k_ref[bi] * (RET_DK ** -0.5)
        v = v_ref[bi]
        outs = []
        for hd in range(RET_HEADS):
            sl = slice(hd * RET_DK, (hd + 1) * RET_DK)
            qh = q[:, sl].astype(BF16)
            kh = k[:, sl]
            vh = v[:, sl].astype(BF16)
            s = lax.dot_general(qh, kh.astype(BF16), (((1,), (1,)), ((), ())), preferred_element_type=F32)
            inner = jnp.dot((s * dmat_ref[hd]).astype(BF16), vh, preferred_element_type=F32)
            r = r_sc[bi, hd]
            cross = jnp.dot(qh, r.astype(BF16), preferred_element_type=F32) * xi_ref[:, hd:hd + 1]
            o = inner + cross
            mu = jnp.mean(o, axis=-1, keepdims=True)
            oc = o - mu
            var = jnp.mean(oc * oc, axis=-1, keepdims=True)
            outs.append(oc * lax.rsqrt(var + EPS))
            kz = (kh * zeta_ref[:, hd:hd + 1]).astype(BF16)
            kv = lax.dot_general(kz, vh, (((0,), (0,)), ((), ())), preferred_element_type=F32)
            r_sc[bi, hd] = dec_ref[hd] * r + kv
        o_ref[bi] = jnp.concatenate(outs, axis=1) * gain_ref[...] * _silu(g_ref[bi])

    @pl.when(ci == pl.num_programs(1) - 1)
    def _():
        rfin_ref[...] = r_sc[...]


def _retention_consts(c):
    log_g = jnp.log1p(-jnp.exp2(-5.0 - jnp.arange(RET_HEADS, dtype=F32)))
    i = jnp.arange(c, dtype=F32)
    diff = i[:, None] - i[None, :]
    dmat = jnp.where(diff >= 0, jnp.exp(jnp.maximum(diff, 0.0)[None] * log_g[:, None, None]), 0.0)
    xi = jnp.exp((i[:, None] + 1.0) * log_g[None, :])
    zeta = jnp.exp((c - 1.0 - i)[:, None] * log_g[None, :])
    dec = jnp.broadcast_to(jnp.exp(c * log_g)[:, None, None], (RET_HEADS, 1, RET_DV))
    return dmat, xi, zeta, dec


def _retention(rq, rk, rv, rg, r0, gain, bb):
    b, l, _ = rq.shape
    c = RET_CHUNK if l % RET_CHUNK == 0 else l
    n = l // c
    dmat, xi, zeta, dec = _retention_consts(c)
    tok = pl.BlockSpec((bb, c, RET_WIDTH), lambda i, j: (i, j, 0))
    st = pl.BlockSpec((bb, RET_HEADS, RET_DK, RET_DV), lambda i, j: (i, 0, 0, 0))
    cst = lambda a: pl.BlockSpec(a.shape, lambda i, j: (0,) * a.ndim)
    return pl.pallas_call(
        _retention_kernel,
        grid=(b // bb, n),
        in_specs=[tok, tok, tok, tok, st, cst(dmat), cst(xi), cst(zeta), cst(dec), cst(gain)],
        out_specs=[tok, st],
        out_shape=[jax.ShapeDtypeStruct((b, l, RET_WIDTH), F32),
                   jax.ShapeDtypeStruct((b, RET_HEADS, RET_DK, RET_DV), F32)],
        scratch_shapes=[pltpu.VMEM((bb, RET_HEADS, RET_DK, RET_DV), F32)],
        compiler_params=_cparams(("arbitrary", "arbitrary")),
        name="retention",
    )(rq, rk, rv, rg, r0, dmat, xi, zeta, dec, gain)


def _rglru_kernel(x_ref, gate_ref, buf_ref, h0_ref, cw_ref, cb_ref, wai_ref, bai_ref, lam_ref, gain_ref,
                  o_ref, conv_ref, hl_ref, xin_sc, a_sc, b_sc, h_sc, hc_sc):
    ci = pl.program_id(1)
    tc = x_ref.shape[1]
    pad = SUBLANES - (CONV_W - 1)

    @pl.when(ci == 0)
    def _():
        xin_sc[pad:SUBLANES, :] = buf_ref[0]
        hc_sc[...] = h0_ref[0]

    xin_sc[SUBLANES:SUBLANES + tc, :] = x_ref[0]
    xc = cb_ref[...]
    for kk in range(CONV_W):
        xc = xc + xin_sc[pad + kk:pad + kk + tc, :] * cw_ref[kk:kk + 1, :]
    tail = xin_sc[SUBLANES + tc - (CONV_W - 1):SUBLANES + tc, :]
    xin_sc[pad:SUBLANES, :] = tail

    z = jnp.dot(xc.astype(BF16), wai_ref[...], preferred_element_type=F32) + bai_ref[...]
    r = jax.nn.sigmoid(z[:, :LRU_WIDTH])
    ig = jax.nn.sigmoid(z[:, LRU_WIDTH:])
    log_a = -LRU_C * r * jax.nn.softplus(-lam_ref[...])
    a = jnp.exp(log_a)
    a_sc[...] = a
    b_sc[...] = jnp.sqrt(-jnp.tanh(log_a) * (a * a + 1.0)) * ig * xc

    def body(t, h):
        h = a_sc[pl.ds(t, 1), :] * h + b_sc[pl.ds(t, 1), :]
        h_sc[pl.ds(t, 1), :] = h
        return h

    h_last = lax.fori_loop(0, tc, body, hc_sc[...], unroll=8)
    hc_sc[...] = h_last
    y = h_sc[...] * jax.nn.gelu(gate_ref[0])
    o_ref[0] = _rms(y, gain_ref[...])

    @pl.when(ci == pl.num_programs(1) - 1)
    def _():
        conv_ref[0] = tail
        hl_ref[0] = h_last


def _rglru(lx, lg, buf0, h0, conv_w, conv_b, wai, bai, lam, gain, tc):
    b, l, w = lx.shape
    tok = pl.BlockSpec((1, tc, w), lambda i, j: (i, j, 0))
    cst = lambda a: pl.BlockSpec(a.shape, lambda i, j: (0,) * a.ndim)
    bufs = pl.BlockSpec((1, CONV_W - 1, w), lambda i, j: (i, 0, 0))
    hs = pl.BlockSpec((1, 1, w), lambda i, j: (i, 0, 0))
    return pl.pallas_call(
        _rglru_kernel,
        grid=(b, l // tc),
        in_specs=[tok, tok, bufs, hs, cst(conv_w), cst(conv_b), cst(wai), cst(bai), cst(lam), cst(gain)],
        out_specs=[tok, bufs, hs],
        out_shape=[jax.ShapeDtypeStruct((b, l, w), F32),
                   jax.ShapeDtypeStruct((b, CONV_W - 1, w), F32),
                   jax.ShapeDtypeStruct((b, 1, w), F32)],
        scratch_shapes=[pltpu.VMEM((SUBLANES + tc, w), F32), pltpu.VMEM((tc, w), F32), pltpu.VMEM((tc, w), F32),
                        pltpu.VMEM((tc, w), F32), pltpu.VMEM((1, w), F32)],
        compiler_params=_cparams(("arbitrary", "arbitrary")),
        name="rglru",
    )(lx, lg, buf0, h0, conv_w, conv_b, wai, bai, lam, gain)


def _softmax_update(s, v, m_sc, l_sc, acc_sc, rows=slice(None)):
    n = s.shape[1]
    m_prev = m_sc[rows, :]
    m_new = jnp.maximum(m_prev, jnp.max(s, axis=-1, keepdims=True))
    corr = jnp.exp(m_prev - m_new)
    p = jnp.exp(s - jnp.concatenate([m_new] * (n // LANES), axis=1))
    l_sc[rows, :] = l_sc[rows, :] * corr + jnp.sum(p, axis=-1, keepdims=True)
    acc_sc[rows, :] = acc_sc[rows, :] * corr + jnp.dot(p.astype(BF16), v, preferred_element_type=F32)
    m_sc[rows, :] = m_new


def _attn_finish(o, wuv_ref, gain_ref, rows_per_head):
    o = o.astype(BF16)
    om = None
    for hd in range(MLA_HEADS):
        part = jnp.dot(o[hd * rows_per_head:(hd + 1) * rows_per_head], wuv_ref[hd], preferred_element_type=F32)
        om = part if om is None else om + part
    return _rms(om, gain_ref[...])


def _flash_kernel(q_ref, k_ref, wuv_ref, gain_ref, o_ref, m_sc, l_sc, acc_sc, *, qb, kb):
    qi = pl.program_id(1)
    r = MLA_HEADS * qb
    m_sc[...] = jnp.full((r, LANES), -jnp.inf, F32)
    l_sc[...] = jnp.zeros((r, LANES), F32)
    acc_sc[...] = jnp.zeros((r, LANES), F32)
    q0 = qi * qb
    nfull = q0 // kb
    halves = [slice(i * (r // 2), (i + 1) * (r // 2)) for i in range(2)]
    q = q_ref[0].reshape(r, QCAT)
    qs = [q[h] for h in halves]

    def step(start, width, masked):
        kblk = k_ref[0, pl.ds(start, width), :]
        v = kblk[:, :KV_RANK]
        for h, qh in zip(halves, qs):
            s = lax.dot_general(qh, kblk, (((1,), (1,)), ((), ())), preferred_element_type=F32)
            if masked:
                row = lax.broadcasted_iota(jnp.int32, s.shape, 0)
                col = lax.broadcasted_iota(jnp.int32, s.shape, 1)
                s = jnp.where(start + col <= q0 + row % qb, s, -jnp.inf)
            _softmax_update(s, v, m_sc, l_sc, acc_sc, h)

    def body(j, carry):
        step(pl.multiple_of(j * kb, kb), kb, False)
        return carry

    lax.fori_loop(0, nfull, body, 0)
    half = kb // 2
    base = pl.multiple_of(nfull * kb, kb)
    upper = (q0 - base) >= half

    @pl.when(upper)
    def _():
        step(base, half, False)

    step(pl.multiple_of(base + jnp.where(upper, half, 0), half), half, True)
    o_ref[0] = _attn_finish(acc_sc[...] / l_sc[...], wuv_ref, gain_ref, qb)


def _mla_prompt(qcat, kcat, wuv_exp, gain, qb, kb):
    b, _, s, _ = qcat.shape
    r = MLA_HEADS * qb
    return pl.pallas_call(
        functools.partial(_flash_kernel, qb=qb, kb=kb),
        grid=(b, s // qb),
        in_specs=[pl.BlockSpec((1, MLA_HEADS, qb, QCAT), lambda i, j: (i, 0, j, 0)),
                  pl.BlockSpec((1, s, QCAT), lambda i, j: (i, 0, 0)),
                  pl.BlockSpec(wuv_exp.shape, lambda i, j: (0, 0, 0)),
                  pl.BlockSpec(gain.shape, lambda i, j: (0, 0))],
        out_specs=pl.BlockSpec((1, qb, MLA_WIDTH), lambda i, j: (i, j, 0)),
        out_shape=jax.ShapeDtypeStruct((b, s, MLA_WIDTH), F32),
        scratch_shapes=[pltpu.VMEM((r, LANES), F32)] * 3,
        compiler_params=_cparams(("arbitrary", "arbitrary")),
        name="mla_prompt",
    )(qcat, kcat, wuv_exp, gain)


def _mla_sample_kernel(pt_ref, q_ref, knew_ref, ck_hbm, kp_hbm, wuv_ref, gain_ref, o_ref, ckbuf, kpbuf, sems,
                       *, li, n_pages, page, t_new, chunk):
    bi = pl.program_id(0)
    nb = pl.num_programs(0)
    r = MLA_HEADS * t_new

    def page_copies(pg_of, slot):
        out = []
        for j in range(n_pages):
            pg = pg_of(j)
            out.append(pltpu.make_async_copy(ck_hbm.at[li, pg], ckbuf.at[slot, pl.ds(j * page, page), :],
                                             sems.at[0, slot]))
            out.append(pltpu.make_async_copy(kp_hbm.at[li, pg], kpbuf.at[slot, :, pl.ds(j * page, page)],
                                             sems.at[1, slot]))
        return out

    @pl.when(bi == 0)
    def _():
        for c in page_copies(lambda j: pt_ref[0, j], 0):
            c.start()

    @pl.when(bi + 1 < nb)
    def _():
        for c in page_copies(lambda j: pt_ref[bi + 1, j], (bi + 1) % 2):
            c.start()

    slot = bi % 2
    q = q_ref[0].reshape(r, QCAT)
    qb = q.astype(BF16)
    q_lat = qb[:, :KV_RANK]
    q_rot = q[:, KV_RANK:].astype(F32)
    q_r = q_rot[:, :MLA_ROPE]
    for i in range(1, LANES // MLA_ROPE):
        q_r = q_r + q_rot[:, i * MLA_ROPE:(i + 1) * MLA_ROPE]
    q_r = q_r.astype(BF16)

    kn = knew_ref[0].astype(BF16)
    s = lax.dot_general(qb, kn, (((1,), (1,)), ((), ())), preferred_element_type=F32)
    row = lax.broadcasted_iota(jnp.int32, s.shape, 0)
    col = lax.broadcasted_iota(jnp.int32, s.shape, 1)
    s = jnp.where(row % t_new >= col, s, -jnp.inf)
    m = jnp.max(s, axis=-1, keepdims=True)
    p = jnp.exp(s - m)
    l = jnp.sum(p, axis=-1, keepdims=True)
    acc = jnp.dot(p.astype(BF16), kn[:, :KV_RANK], preferred_element_type=F32)

    for c in page_copies(lambda j: 0, slot):
        c.wait()

    for c in range(n_pages * page // chunk):
        ck = ckbuf[slot, pl.ds(c * chunk, chunk), :].astype(BF16)
        kp = kpbuf[slot, :, pl.ds(c * chunk, chunk)].astype(BF16)
        s = (lax.dot_general(q_lat, ck, (((1,), (1,)), ((), ())), preferred_element_type=F32)
             + jnp.dot(q_r, kp, preferred_element_type=F32))
        m_new = jnp.maximum(m, jnp.max(s, axis=-1, keepdims=True))
        corr = jnp.exp(m - m_new)
        p = jnp.exp(s - m_new)
        l = l * corr + jnp.sum(p, axis=-1, keepdims=True)
        acc = acc * corr + jnp.dot(p.astype(BF16), ck, preferred_element_type=F32)
        m = m_new
    o_ref[0] = _attn_finish(acc / l, wuv_ref, gain_ref, t_new)


def _mla_sample(qcat, kcat, cache_ckv, cache_kr_t, page_table, li, wuv_exp, gain):
    b, _, t, _ = qcat.shape
    n_pages = page_table.shape[1]
    page, rank = cache_ckv.shape[2], cache_ckv.shape[3]
    rope_w = cache_kr_t.shape[2]
    chunk = min(8192, n_pages * page)
    grid_spec = pltpu.PrefetchScalarGridSpec(
        num_scalar_prefetch=1,
        grid=(b,),
        in_specs=[pl.BlockSpec((1, MLA_HEADS, t, QCAT), lambda bi, pt: (bi, 0, 0, 0)),
                  pl.BlockSpec((1, t, QCAT), lambda bi, pt: (bi, 0, 0)),
                  pl.BlockSpec(memory_space=pl.ANY),
                  pl.BlockSpec(memory_space=pl.ANY),
                  pl.BlockSpec(wuv_exp.shape, lambda bi, pt: (0, 0, 0)),
                  pl.BlockSpec(gain.shape, lambda bi, pt: (0, 0))],
        out_specs=pl.BlockSpec((1, t, MLA_WIDTH), lambda bi, pt: (bi, 0, 0)),
        scratch_shapes=[pltpu.VMEM((2, n_pages * page, rank), F32),
                        pltpu.VMEM((2, rope_w, n_pages * page), F32),
                        pltpu.SemaphoreType.DMA((2, 2))],
    )
    return pl.pallas_call(
        functools.partial(_mla_sample_kernel, li=li, n_pages=n_pages, page=page, t_new=t, chunk=chunk),
        grid_spec=grid_spec,
        out_shape=jax.ShapeDtypeStruct((b, t, MLA_WIDTH), F32),
        compiler_params=_cparams(("arbitrary",)),
        name="mla_sample",
    )(page_table, qcat, kcat, cache_ckv, cache_kr_t, wuv_exp, gain)


def _outproj_kernel(ret_ref, lru_ref, mla_ref, x_ref, w_ref, g1_ref, gt1_ref, g2_ref, sc2_ref, sh2_ref,
                    x1_ref, h2_ref, h2p_ref):
    bt, lt, d = x_ref.shape
    rows = bt * lt
    mix = jnp.concatenate([ret_ref[...].reshape(rows, RET_WIDTH), lru_ref[...].reshape(rows, LRU_WIDTH),
                           mla_ref[...].reshape(rows, MLA_WIDTH)], axis=1).astype(BF16)
    y = jnp.dot(mix, w_ref[...], preferred_element_type=F32).reshape(bt, lt, d)
    x1 = x_ref[...] + gt1_ref[...] * _rms(y, g1_ref[...])
    x1_ref[...] = x1
    h2 = _rms(x1, g2_ref[...]) * (1.0 + sc2_ref[...]) + sh2_ref[...]
    h2_ref[...] = h2
    _store_packed(h2p_ref, h2.reshape(rows, d))


def _outproj(ret, lru, mla, x, w_out, g1, gt1, g2, sc2, sh2, bt, lt):
    b, l, d = x.shape
    tok = lambda w: pl.BlockSpec((bt, lt, w), lambda i, j: (i, j, 0))
    mod = pl.BlockSpec((bt, 1, d), lambda i, j: (i, 0, 0))
    gsp = pl.BlockSpec((1, 1, d), lambda i, j: (0, 0, 0))
    return pl.pallas_call(
        _outproj_kernel,
        grid=(b // bt, l // lt),
        in_specs=[tok(RET_WIDTH), tok(LRU_WIDTH), tok(MLA_WIDTH), tok(d),
                  pl.BlockSpec(w_out.shape, lambda i, j: (0, 0)), gsp, mod, gsp, mod, mod],
        out_specs=[tok(d), tok(d),
                   pl.BlockSpec((bt * lt * PACK_ROWS, LANES), lambda i, j: (i * (l // lt) + j, 0))],
        out_shape=[jax.ShapeDtypeStruct((b, l, d), F32)] * 2
        + [jax.ShapeDtypeStruct((b * l * PACK_ROWS, LANES), jnp.int32)],
        compiler_params=_cparams(("arbitrary", "arbitrary")),
        name="outproj",
    )(ret, lru, mla, x, w_out, g1, gt1, g2, sc2, sh2)


def _split_dot_t(w, x):
    dn = (((1,), (1,)), ((), ()))
    wh = w.astype(BF16)
    wl = (w - wh.astype(F32)).astype(BF16)
    xh = x.astype(BF16)
    xl = (x - xh.astype(F32)).astype(BF16)
    return (lax.dot_general(wh, xh, dn, preferred_element_type=F32)
            + lax.dot_general(wh, xl, dn, preferred_element_type=F32)
            + lax.dot_general(wl, xh, dn, preferred_element_type=F32))


def _router_kernel(x_ref, w_ref, b_ref, upper_ref, lower_ref, eid_ref, pos_ref, gate_ref, cnt_ref, cnt_sc):
    ti = pl.program_id(0)
    tm = x_ref.shape[0]

    @pl.when(ti == 0)
    def _():
        cnt_sc[...] = jnp.zeros(cnt_sc.shape, F32)

    logits = _split_dot_t(w_ref[...], x_ref[...])
    scores = jax.nn.sigmoid(logits)
    biased = scores + b_ref[...]
    groups = [biased[g * GROUP_SIZE:(g + 1) * GROUP_SIZE, :] for g in range(N_GROUPS)]
    sub = lax.broadcasted_iota(jnp.int32, (GROUP_SIZE, tm), 0)
    gs = []
    for bg in groups:
        m1 = jnp.max(bg, axis=0, keepdims=True)
        i1 = jnp.min(jnp.where(bg == m1, sub, GROUP_SIZE), axis=0, keepdims=True)
        m2 = jnp.max(jnp.where(sub == i1, -jnp.inf, bg), axis=0, keepdims=True)
        gs.append(m1 + m2)
    masked = []
    for g in range(N_GROUPS):
        rank = jnp.zeros((1, tm), F32)
        for g2 in range(N_GROUPS):
            if g2 != g:
                ahead = (gs[g2] >= gs[g]) if g2 < g else (gs[g2] > gs[g])
                rank = rank + jnp.where(ahead, 1.0, 0.0)
        masked.append(jnp.where(rank < TOPK_GROUPS, groups[g], -jnp.inf))
    sels = []
    for g in range(N_GROUPS):
        mine = masked[g]
        rank = jnp.zeros((GROUP_SIZE, tm), F32)
        for g2 in range(N_GROUPS):
            for s2 in range(GROUP_SIZE):
                other = jnp.broadcast_to(masked[g2][s2:s2 + 1, :], (GROUP_SIZE, tm))
                if g2 < g:
                    rank = rank + jnp.where(other >= mine, 1.0, 0.0)
                elif g2 > g:
                    rank = rank + jnp.where(other > mine, 1.0, 0.0)
                else:
                    tie = jnp.where(sub > s2, 1.0, 0.0)
                    rank = rank + jnp.where(other > mine, 1.0, jnp.where(other == mine, tie, 0.0))
        sels.append(jnp.where(rank < TOP_K, 1.0, 0.0))
    self32 = jnp.concatenate(sels, axis=0)
    sel = self32 > 0.5
    picked = jnp.where(sel, scores, 0.0)
    gate = picked / jnp.sum(picked, axis=0, keepdims=True) * ROUTED_SCALE
    selb = self32.astype(BF16)
    pos = jnp.dot(selb, upper_ref[...], preferred_element_type=F32) + cnt_sc[:, :1]
    cnt_sc[...] = cnt_sc[...] + jnp.sum(self32, axis=1, keepdims=True)
    slot = jnp.dot(lower_ref[...], selb, preferred_element_type=F32)
    efl = lax.broadcasted_iota(jnp.int32, (N_EXPERTS, tm), 0).astype(F32)
    for kk in range(TOP_K):
        hit = sel & (slot == float(kk))
        eid_ref[kk:kk + 1, :] = jnp.sum(jnp.where(hit, efl, 0.0), axis=0, keepdims=True).astype(jnp.int32)
        pos_ref[kk:kk + 1, :] = jnp.sum(jnp.where(hit, pos, 0.0), axis=0, keepdims=True).astype(jnp.int32)
        gate_ref[kk:kk + 1, :] = jnp.sum(jnp.where(hit, gate, 0.0), axis=0, keepdims=True)

    @pl.when(ti == pl.num_programs(0) - 1)
    def _():
        cnt_ref[...] = cnt_sc[...]


def _router(h2, w_rt, b_r, tm):
    t, d = h2.shape
    upper = jnp.triu(jnp.ones((tm, tm), F32), 1).astype(BF16)
    lower = jnp.tril(jnp.ones((N_EXPERTS, N_EXPERTS), F32), -1).astype(BF16)
    kt = pl.BlockSpec((TOP_K, tm), lambda i: (0, i))
    cst = lambda a: pl.BlockSpec(a.shape, lambda i: (0,) * a.ndim)
    return pl.pallas_call(
        _router_kernel,
        grid=(t // tm,),
        in_specs=[pl.BlockSpec((tm, d), lambda i: (i, 0)), cst(w_rt), cst(b_r), cst(upper), cst(lower)],
        out_specs=[kt, kt, kt, pl.BlockSpec((N_EXPERTS, LANES), lambda i: (0, 0))],
        out_shape=[jax.ShapeDtypeStruct((TOP_K, t), jnp.int32), jax.ShapeDtypeStruct((TOP_K, t), jnp.int32),
                   jax.ShapeDtypeStruct((TOP_K, t), F32), jax.ShapeDtypeStruct((N_EXPERTS, LANES), F32)],
        scratch_shapes=[pltpu.VMEM((N_EXPERTS, LANES), F32)],
        compiler_params=_cparams(("arbitrary",)),
        name="router",
    )(h2, w_rt, b_r, upper, lower)


PACK_ROWS = 4


def _store_packed(o_ref, x):
    r = x.shape[0]
    bits = lax.bitcast_convert_type(x.astype(BF16).astype(F32), jnp.int32)
    for j in range(PACK_ROWS):
        lo = lax.shift_right_logical(bits[:, j * LANES:(j + 1) * LANES], jnp.int32(16))
        hi = bits[:, (j + PACK_ROWS) * LANES:(j + PACK_ROWS + 1) * LANES] & jnp.int32(-65536)
        o_ref[pl.ds(j, r, stride=PACK_ROWS), :] = hi | lo


def _unpack_words(w):
    lo = lax.bitcast_convert_type(lax.shift_left(w, jnp.int32(16)), F32)
    hi = lax.bitcast_convert_type(w & jnp.int32(-65536), F32)
    return lo, hi


def _dispatch_kernel(dest_ref, x_ref, rows_in_ref, rows_ref, sem):
    del rows_in_ref
    tm = x_ref.shape[0] // PACK_ROWS

    def row_copy(t, d):
        return pltpu.make_async_copy(x_ref.at[pl.ds(pl.multiple_of(t * PACK_ROWS, PACK_ROWS), PACK_ROWS), :],
                                     rows_ref.at[pl.ds(pl.multiple_of(d * PACK_ROWS, PACK_ROWS), PACK_ROWS), :], sem)

    def issue(t, c):
        for kk in range(TOP_K):
            row_copy(t, dest_ref[t * TOP_K + kk]).start()
        return c

    def drain(t, c):
        for kk in range(TOP_K):
            row_copy(0, 0).wait()
        return c

    lax.fori_loop(0, tm, issue, 0)
    lax.fori_loop(0, tm, drain, 0)


def _dispatch(dest_flat, h2p, rows0, tm):
    t = h2p.shape[0] // PACK_ROWS
    return pl.pallas_call(
        _dispatch_kernel,
        grid=(t // tm,),
        in_specs=[pl.BlockSpec((tm * TOP_K,), lambda i: (i,), memory_space=pltpu.SMEM),
                  pl.BlockSpec((tm * PACK_ROWS, LANES), lambda i: (i, 0)),
                  pl.BlockSpec(memory_space=pl.ANY)],
        out_specs=pl.BlockSpec(memory_space=pl.ANY),
        out_shape=jax.ShapeDtypeStruct(rows0.shape, jnp.int32),
        scratch_shapes=[pltpu.SemaphoreType.DMA(())],
        input_output_aliases={2: 0},
        compiler_params=_cparams(("arbitrary",)),
        name="moe_dispatch",
    )(dest_flat, h2p, rows0)


def _expert_kernel(be_ref, nu_ref, rows_ref, wgu_ref, wd_ref, o_ref, wgu_sc, wd_sc):
    i = pl.program_id(0)
    prev = be_ref[jnp.maximum(i - 1, 0)]
    fresh = (i == 0) | (be_ref[i] != prev)

    @pl.when(fresh)
    def _():
        wgu_sc[...] = wgu_ref[...].astype(BF16)
        wd_sc[...] = wd_ref[...].astype(BF16)

    @pl.when(i < nu_ref[0])
    def _():
        blk = rows_ref.shape[0] // PACK_ROWS
        halves = [_unpack_words(rows_ref[pl.ds(j, blk, stride=PACK_ROWS), :]) for j in range(PACK_ROWS)]
        x = jnp.concatenate([h[0] for h in halves] + [h[1] for h in halves], axis=1).astype(BF16)
        gu = jnp.dot(x, wgu_sc[...], preferred_element_type=F32)
        act = _silu(gu[:, :D_EXPERT]) * gu[:, D_EXPERT:]
        y = jnp.dot(act.astype(BF16), wd_sc[...], preferred_element_type=F32)
        _store_packed(o_ref, y)

    @pl.when(i >= nu_ref[0])
    def _():
        o_ref[...] = jnp.zeros(o_ref.shape, o_ref.dtype)


def _experts(block_e, n_used, rows, w_gu, w_down, li, blk):
    n_rows = rows.shape[0] // PACK_ROWS
    n_blocks = n_rows // blk
    d, de2 = w_gu.shape[-2:]
    de = w_down.shape[-2]

    def rmap(i, be, nu):
        return (jnp.minimum(i, nu[0] - 1), 0)

    grid_spec = pltpu.PrefetchScalarGridSpec(
        num_scalar_prefetch=2,
        grid=(n_blocks,),
        in_specs=[pl.BlockSpec((blk * PACK_ROWS, LANES), rmap),
                  pl.BlockSpec((None, None, d, de2), lambda i, be, nu: (li, be[i], 0, 0)),
                  pl.BlockSpec((None, None, de, d), lambda i, be, nu: (li, be[i], 0, 0))],
        out_specs=pl.BlockSpec((blk * PACK_ROWS, LANES), lambda i, be, nu: (i, 0)),
        scratch_shapes=[pltpu.VMEM((d, de2), BF16), pltpu.VMEM((de, d), BF16)],
    )
    return pl.pallas_call(
        _expert_kernel,
        grid_spec=grid_spec,
        out_shape=jax.ShapeDtypeStruct(rows.shape, jnp.int32),
        compiler_params=_cparams(("arbitrary",)),
        name="moe_experts",
    )(block_e, n_used, rows, w_gu, w_down)


def _combine_kernel(dest_ref, dnext_ref, gate_ref, orow_ref, h2_ref, x1_ref, wsgu_ref, wsd_ref, g3_ref, gt2_ref,
                    y_ref, buf, sems):
    bt, lt, d = x1_ref.shape
    tm = bt * lt
    ns = PACK_ROWS
    step = pl.program_id(0) * pl.num_programs(1) + pl.program_id(1)
    nsteps = pl.num_programs(0) * pl.num_programs(1)
    slot = step % 2

    def row_copy(src_row, t, kk, sl):
        return pltpu.make_async_copy(orow_ref.at[pl.ds(pl.multiple_of(src_row * ns, ns), ns), :],
                                     buf.at[sl, kk, pl.ds(pl.multiple_of(t * ns, ns), ns), :],
                                     sems.at[sl])

    def issue_tile(dref, sl):
        def issue(t, c):
            for kk in range(TOP_K):
                row_copy(dref[t * TOP_K + kk], t, kk, sl).start()
            return c
        lax.fori_loop(0, tm, issue, 0)

    @pl.when(step == 0)
    def _():
        issue_tile(dest_ref, 0)

    @pl.when(step + 1 < nsteps)
    def _():
        issue_tile(dnext_ref, 1 - slot)

    hb = h2_ref[...].astype(BF16)
    gu = jnp.dot(hb, wsgu_ref[...], preferred_element_type=F32)
    ds = wsd_ref.shape[0]
    f = jnp.dot((_silu(gu[:, :ds]) * gu[:, ds:]).astype(BF16), wsd_ref[...], preferred_element_type=F32)

    def drain(t, c):
        for kk in range(TOP_K):
            row_copy(0, 0, 0, slot).wait()
        return c

    lax.fori_loop(0, tm, drain, 0)
    gate = gate_ref[...]
    fs = [f[:, j * LANES:(j + 1) * LANES] for j in range(2 * ns)]
    for kk in range(TOP_K):
        gk = jnp.broadcast_to(gate[:, kk:kk + 1], (tm, LANES))
        for j in range(ns):
            lo, hi = _unpack_words(buf[slot, kk, pl.ds(j, tm, stride=ns), :])
            fs[j] = fs[j] + lo * gk
            fs[j + ns] = fs[j + ns] + hi * gk
    f = jnp.concatenate(fs, axis=1)
    y_ref[...] = x1_ref[...] + gt2_ref[...] * _rms(f, g3_ref[...]).reshape(bt, lt, d)


def _combine(dest_flat, gate_tk, out_rows, h2, x1, ws_gu, ws_down, g3, gt2, bt, lt):
    b, l, d = x1.shape
    tm = bt * lt
    nl = l // lt
    nsteps = (b // bt) * nl
    tok = pl.BlockSpec((bt, lt, d), lambda i, j: (i, j, 0))
    flat = lambda w: pl.BlockSpec((tm, w), lambda i, j: (i * nl + j, 0))
    return pl.pallas_call(
        _combine_kernel,
        grid=(b // bt, nl),
        in_specs=[pl.BlockSpec((tm * TOP_K,), lambda i, j: (i * nl + j,), memory_space=pltpu.SMEM),
                  pl.BlockSpec((tm * TOP_K,), lambda i, j: (jnp.minimum(i * nl + j + 1, nsteps - 1),),
                               memory_space=pltpu.SMEM),
                  flat(TOP_K), pl.BlockSpec(memory_space=pl.ANY), flat(d), tok,
                  pl.BlockSpec(ws_gu.shape, lambda i, j: (0, 0)), pl.BlockSpec(ws_down.shape, lambda i, j: (0, 0)),
                  pl.BlockSpec((1, 1, d), lambda i, j: (0, 0, 0)),
                  pl.BlockSpec((bt, 1, d), lambda i, j: (i, 0, 0))],
        out_specs=tok,
        out_shape=jax.ShapeDtypeStruct((b, l, d), F32),
        scratch_shapes=[pltpu.VMEM((2, TOP_K, tm * PACK_ROWS, LANES), jnp.int32), pltpu.SemaphoreType.DMA((2,))],
        compiler_params=_cparams(("arbitrary", "arbitrary")),
        name="moe_combine",
    )(dest_flat, dest_flat, gate_tk, out_rows, h2, x1, ws_gu, ws_down, g3, gt2)


def _moe(groups, g3, w_rt, b_r, w_exp_gu, w_exp_down, ws_gu, ws_down, li, blk, rows_init):
    d = groups[0][0].shape[-1]
    sizes = [g[0].shape[0] * g[0].shape[1] for g in groups]
    t = sum(sizes)
    flats = [g[0].reshape(n, d) for g, n in zip(groups, sizes)]
    h2f = flats[0] if len(flats) == 1 else jnp.concatenate(flats, axis=0)
    h2p = groups[0][1] if len(groups) == 1 else jnp.concatenate([g[1] for g in groups], axis=0)
    eid, pos, gate, cnt = _router(h2f, w_rt, b_r, _largest_tile(t, 512))
    counts = cnt[:, 0].astype(jnp.int32)
    padded = (counts + blk - 1) // blk * blk
    pends = jnp.cumsum(padded)
    pstart = pends - padded
    n_blocks = -(-(t * TOP_K) // blk) + N_EXPERTS
    starts = jnp.arange(n_blocks, dtype=jnp.int32) * blk
    block_e = jnp.minimum(jnp.sum((pends[None, :] <= starts[:, None]).astype(jnp.int32), axis=1), N_EXPERTS - 1)
    n_used = (pends[-1:] // blk).astype(jnp.int32)
    first_row = jnp.sum(jnp.where(eid[:, :, None] == jnp.arange(N_EXPERTS, dtype=jnp.int32), pstart, 0), axis=-1)
    dest = (first_row + pos).T.reshape(t * TOP_K)
    gate_tk = gate.T
    if rows_init is None:
        rows_init = jnp.zeros((n_blocks * blk * PACK_ROWS, LANES), jnp.int32)
    rows = _dispatch(dest, h2p, rows_init, _largest_tile(t, 512))
    out_rows = _experts(block_e, n_used, rows, w_exp_gu, w_exp_down, li, blk)
    outs, off = [], 0
    for (h2, _, x1, gt2, bt, lt), n, flat in zip(groups, sizes, flats):
        outs.append(_combine(dest[off * TOP_K:(off + n) * TOP_K], gate_tk[off:off + n], out_rows, flat, x1,
                             ws_gu, ws_down, g3, gt2, bt, lt))
        off += n
    return outs, rows


def _permute_w_in(w_in):
    dp, d, _ = w_in.shape
    hd = MLA_NOPE + MLA_ROPE
    mq = w_in[:, :, C_QN:C_QN + MLA_HEADS * hd].reshape(dp, d, MLA_HEADS, hd)
    qn = mq[..., :MLA_NOPE].reshape(dp, d, MLA_HEADS * MLA_NOPE)
    qr = mq[..., MLA_NOPE:].reshape(dp, d, MLA_HEADS * MLA_ROPE)
    mc0 = C_QN + MLA_HEADS * hd
    mc = w_in[:, :, mc0:mc0 + KV_RANK]
    mk = w_in[:, :, mc0 + KV_RANK:mc0 + KV_RANK + MLA_ROPE]
    return jnp.concatenate([w_in[:, :, :C_QN], qn, qr, mc] + [mk] * (LANES // MLA_ROPE), axis=-1).astype(BF16)


def _rope_tables(pos, dim):
    half = dim // 2
    inv = ROPE_BASE ** (-jnp.arange(0, dim, 2, dtype=F32) / dim)
    ang = pos.astype(F32)[:, None] * inv[None, :]
    cos, sin = jnp.cos(ang), jnp.sin(ang)
    reps = LANES // dim
    cos_t = jnp.tile(jnp.concatenate([cos, cos], axis=1), (1, reps))
    sin_t = jnp.tile(jnp.concatenate([-sin, sin], axis=1), (1, reps))
    return cos_t, sin_t


def _block_diag(blocks):
    n, a, b = blocks.shape
    eye = jnp.eye(n, dtype=blocks.dtype)
    return (eye[:, None, :, None] * blocks[:, :, None, :]).reshape(n * a, n * b)


def _mixer(x, mod, pos_tabs, li, p, past, tiles):
    b, l, d = x.shape
    bt, lt = tiles['bt'], tiles['lt']
    sh1, sc1, gt1, sh2, sc2, gt2 = [m[:, None, :] for m in jnp.split(mod, 6, axis=-1)]
    g = p['norm_gains']
    gains = [g[i][None, None, :] for i in range(4)]
    gain = p['grp_gain']
    rq, rk, rv, rg, lx, lg, qcat, kcat, ckv, kr = _inproj(
        x, sc1, sh1, gains[0], p['w_in_p'], p['wuk_bd'], p['kv_norm'][None, :], pos_tabs, bt, lt, tiles['qdtype'])
    if past is None:
        r0 = jnp.zeros((b, RET_HEADS, RET_DK, RET_DV), F32)
        buf0 = jnp.zeros((b, CONV_W - 1, LRU_WIDTH), F32)
        h0 = jnp.zeros((b, 1, LRU_WIDTH), F32)
    else:
        r0, buf0, h0 = past['ret'], past['conv'], past['lru'][:, None, :]
    ret_out, r_new = _retention(rq, rk, rv, rg, r0, gain[None, :RET_WIDTH], tiles['rbb'])
    lru_out, conv_new, h_new = _rglru(
        lx, lg, buf0, h0, p['conv_w'], p['conv_b'][None, :], p['wai'], p['bai'], p['lru_lambda'][None, :],
        gain[None, RET_WIDTH:RET_WIDTH + LRU_WIDTH], tiles['tc'])
    mla_gain = gain[None, RET_WIDTH + LRU_WIDTH:]
    if past is None:
        mla_out = _mla_prompt(qcat, kcat, p['wuv_exp'], mla_gain, tiles['qb'], tiles['kb'])
    else:
        mla_out = _mla_sample(qcat, kcat, past['cache_ckv'], past['cache_kr_t'], past['page_table'], li,
                              p['wuv_exp'], mla_gain)
    x1, h2, h2p = _outproj(ret_out, lru_out, mla_out, x, p['w_out'], gains[1], gt1, gains[2], sc2, sh2, bt, lt)
    return (h2, h2p, x1, gt2, tiles['cbt'], tiles['clt']), (ckv, kr, r_new, conv_new, h_new[:, 0])


def _largest_tile(n, cap):
    t = min(n, cap)
    while n % t:
        t //= 2
    return t


def kernel(x_prompt, x_sample, cache_ckv, cache_krope, state_ret, state_conv, state_lru, page_table, c_prompt, c_sample, w_ada, b_ada, norm_gains, w_in, w_out, grp_gain, conv_w, conv_b, w_lru_a, b_lru_a, w_lru_i, b_lru_i, lru_lambda, kv_norm, w_ukv, w_router, b_router, w_exp_gu, w_exp_down, w_sh_gu, w_sh_down):
    depth = w_in.shape[0]
    bp, lp, d = x_prompt.shape
    bs, ls, _ = x_sample.shape
    past_len = page_table.shape[1] * cache_ckv.shape[2]

    n_c = bp + bs
    n_cp = -(-n_c // SUBLANES) * SUBLANES
    c_all = jnp.concatenate([c_prompt, c_sample, jnp.zeros((n_cp - n_c, d), F32)], axis=0)
    mod_all = _adaln(c_all, w_ada, b_ada[:, None, :])

    w_in_p = _permute_w_in(w_in)
    w_uk = w_ukv[..., :MLA_NOPE]
    w_uv = w_ukv[..., MLA_NOPE:]
    eye_h = jnp.eye(MLA_HEADS, dtype=F32)

    pos_p = jnp.arange(lp)
    pos_s = past_len + jnp.arange(ls)
    tabs_p = _rope_tables(pos_p, RET_DK) + _rope_tables(pos_p, MLA_ROPE)
    tabs_s = _rope_tables(pos_s, RET_DK) + _rope_tables(pos_s, MLA_ROPE)

    lt_p = _largest_tile(lp, 512)
    kb_p = _largest_tile(lp, 1024)
    tiles_p = dict(bt=1, lt=lt_p, rbb=1, qdtype=BF16, tc=_largest_tile(lp, 512), qb=_largest_tile(lp, min(256, kb_p // 2)),
                   kb=kb_p, cbt=1, clt=_largest_tile(lp, 128))
    bt_s = _largest_tile(bs, 32)
    tiles_s = dict(bt=bt_s, lt=ls, qdtype=F32, tc=ls, rbb=_largest_tile(bs, 8),
                   cbt=_largest_tile(bs, 32), clt=ls)

    cache_kr_t = jnp.swapaxes(cache_krope, 2, 3)
    y_p, y_s = x_prompt, x_sample
    blk = 256 if bp * lp + bs * ls >= 4096 else 128
    moe_rows = None
    st_p, st_s = [], []
    for li in range(depth):
        wuk_bd = _block_diag(jnp.transpose(w_uk[li], (1, 2, 0))).astype(BF16)
        wuv_h = jnp.transpose(w_uv[li], (1, 0, 2))
        wuv_exp = (wuv_h[:, :, None, :] * eye_h[:, None, :, None]).reshape(
            MLA_HEADS, KV_RANK, MLA_WIDTH).astype(BF16)
        wai = jnp.concatenate([_block_diag(w_lru_a[li]), _block_diag(w_lru_i[li])], axis=1).astype(BF16)
        bai = jnp.concatenate([b_lru_a[li], b_lru_i[li]])[None, :]
        p = dict(norm_gains=norm_gains[li], w_in_p=w_in_p[li], wuk_bd=wuk_bd, wuv_exp=wuv_exp,
                 w_out=w_out[li].astype(BF16), grp_gain=grp_gain[li], conv_w=conv_w[li], conv_b=conv_b[li],
                 wai=wai, bai=bai, lru_lambda=lru_lambda[li], kv_norm=kv_norm[li],
                 w_rt=w_router[li].T, b_router=b_router[li], w_exp_gu=w_exp_gu, w_exp_down=w_exp_down,
                 ws_gu=w_sh_gu[li].astype(BF16), ws_down=w_sh_down[li].astype(BF16))
        grp_p, sp = _mixer(y_p, mod_all[li, :bp], tabs_p, li, p, None, tiles_p)
        past = dict(ret=state_ret[li], conv=state_conv[li], lru=state_lru[li], cache_ckv=cache_ckv,
                    cache_kr_t=cache_kr_t, page_table=page_table)
        grp_s, ss = _mixer(y_s, mod_all[li, bp:bp + bs], tabs_s, li, p, past, tiles_s)
        (y_p, y_s), moe_rows = _moe([grp_p, grp_s], norm_gains[li][3][None, None, :], p['w_rt'], p['b_router'][:, None],
                                    p['w_exp_gu'], p['w_exp_down'], p['ws_gu'], p['ws_down'], li, blk, moe_rows)
        st_p.append(sp)
        st_s.append(ss)
    stack = lambda sts, i: jnp.stack([s[i] for s in sts])
    return (y_p, y_s) + tuple(stack(st_p, i) for i in range(5)) + tuple(stack(st_s, i) for i in range(5))
```

```python
import functools

import jax
import jax.numpy as jnp
from jax import lax
from jax.experimental import pallas as pl
from jax.experimental.pallas import tpu as pltpu

F32 = jnp.float32
BF16 = jnp.bfloat16
EPS = 1e-6
ROPE_BASE = 10000.0

RET_HEADS = 4
RET_DK = 64
RET_DV = 64
RET_WIDTH = RET_HEADS * RET_DV
RET_CHUNK = 128
LRU_WIDTH = 256
LRU_BLOCKS = 4
LRU_BLOCK = LRU_WIDTH // LRU_BLOCKS
CONV_W = 4
LRU_C = 8.0
MLA_HEADS = 8
MLA_NOPE = 64
MLA_ROPE = 32
MLA_V = 64
KV_RANK = 128
MLA_WIDTH = MLA_HEADS * MLA_V
N_EXPERTS = 64
TOP_K = 8
N_GROUPS = 8
GROUP_SIZE = N_EXPERTS // N_GROUPS
TOPK_GROUPS = 4
D_EXPERT = 256
ROUTED_SCALE = 2.5

LANES = 128
SUBLANES = 8
QCAT = 2 * LANES
C_RQ, C_RK, C_RV, C_RG, C_LX, C_LG = 0, 256, 512, 768, 1024, 1280
C_QN = 1536
C_QR = C_QN + MLA_HEADS * MLA_NOPE
C_MC = C_QR + MLA_HEADS * MLA_ROPE
C_MK = C_MC + KV_RANK
N_PROJ = C_MK + LANES


def _cparams(sem, vmem_mb=48):
    return pltpu.CompilerParams(dimension_semantics=sem, vmem_limit_bytes=vmem_mb * 2 ** 20)


def _rms(x, g):
    return x * lax.rsqrt(jnp.mean(x * x, axis=-1, keepdims=True) + EPS) * g


def _silu(x):
    return x * jax.nn.sigmoid(x)


def _adaln_kernel(c_ref, w_ref, b_ref, o_ref):
    c = c_ref[...]
    s = _silu(c).astype(BF16)
    o_ref[...] = jnp.dot(s, w_ref[...].astype(BF16), preferred_element_type=F32) + b_ref[...]


def _adaln(c, w_ada, b_ada):
    depth, d, n = w_ada.shape
    bp = c.shape[0]
    tn = 768
    return pl.pallas_call(
        _adaln_kernel,
        grid=(depth, n // tn),
        in_specs=[pl.BlockSpec((bp, d), lambda l, j: (0, 0)),
                  pl.BlockSpec((None, d, tn), lambda l, j: (l, 0, j)),
                  pl.BlockSpec((None, 1, tn), lambda l, j: (l, 0, j))],
        out_specs=pl.BlockSpec((None, bp, tn), lambda l, j: (l, 0, j)),
        out_shape=jax.ShapeDtypeStruct((depth, bp, n), F32),
        compiler_params=_cparams(("arbitrary", "arbitrary")),
        name="adaln",
    )(c, w_ada, b_ada)


def _rope_slab(v, cos, sin_signed, half):
    lane = lax.broadcasted_iota(jnp.int32, v.shape, 1)
    first = (lane % (2 * half)) < half
    partner = jnp.where(first, pltpu.roll(v, LANES - half, 1), pltpu.roll(v, half, 1))
    return v * cos + partner * sin_signed


def _rope(v, cos, sin_signed, half):
    n = v.shape[1] // LANES
    outs = [_rope_slab(v[:, i * LANES:(i + 1) * LANES], cos, sin_signed, half) for i in range(n)]
    return outs[0] if n == 1 else jnp.concatenate(outs, axis=1)


def _inproj_kernel(x_ref, sc_ref, sh_ref, g_ref, w_ref, wuk_ref, kvg_ref, c64_ref, s64_ref, c32_ref, s32_ref,
                   rq_ref, rk_ref, rv_ref, rg_ref, lx_ref, lg_ref, qcat_ref, kcat_ref, ckv_ref, kr_ref):
    bt, lt, d = x_ref.shape
    rows = bt * lt
    h = _rms(x_ref[...], g_ref[...]) * (1.0 + sc_ref[...]) + sh_ref[...]
    hb = h.reshape(rows, d).astype(BF16)

    def proj(a, b):
        return jnp.dot(hb, w_ref[:, a:b], preferred_element_type=F32)

    def table(ref):
        t = ref[...]
        if bt > 1:
            t = jnp.broadcast_to(t[None], (bt, lt, LANES)).reshape(rows, LANES)
        return t

    c64, s64, c32, s32 = table(c64_ref), table(s64_ref), table(c32_ref), table(s32_ref)
    rq_ref[...] = _rope(proj(C_RQ, C_RK), c64, s64, RET_DK // 2).reshape(bt, lt, RET_WIDTH)
    rk_ref[...] = _rope(proj(C_RK, C_RV), c64, s64, RET_DK // 2).reshape(bt, lt, RET_WIDTH)
    rv_ref[...] = proj(C_RV, C_RG).reshape(bt, lt, RET_WIDTH)
    rg_ref[...] = proj(C_RG, C_LX).reshape(bt, lt, RET_WIDTH)
    lx_ref[...] = proj(C_LX, C_LG).reshape(bt, lt, LRU_WIDTH)
    lg_ref[...] = proj(C_LG, C_QN).reshape(bt, lt, LRU_WIDTH)

    scale = (MLA_NOPE + MLA_ROPE) ** -0.5
    q_lat = jnp.dot(proj(C_QN, C_QR).astype(BF16), wuk_ref[...], preferred_element_type=F32) * scale
    q_rope = _rope(proj(C_QR, C_MC), c32, s32, MLA_ROPE // 2) * scale
    lane = lax.broadcasted_iota(jnp.int32, (rows, LANES), 1)
    heads_per_slab = LANES // MLA_ROPE
    for hd in range(MLA_HEADS):
        slab = q_rope[:, (hd // heads_per_slab) * LANES:(hd // heads_per_slab + 1) * LANES]
        own = jnp.where(lane // MLA_ROPE == hd % heads_per_slab, slab, 0.0)
        qh = jnp.concatenate([q_lat[:, hd * KV_RANK:(hd + 1) * KV_RANK], own], axis=1)
        qcat_ref[:, hd, :, :] = qh.reshape(bt, lt, QCAT).astype(qcat_ref.dtype)

    ckv = _rms(proj(C_MC, C_MK), kvg_ref[...])
    kr4 = _rope(proj(C_MK, N_PROJ), c32, s32, MLA_ROPE // 2)
    ckv_ref[...] = ckv.reshape(bt, lt, KV_RANK)
    kr_ref[...] = kr4[:, :MLA_ROPE].reshape(bt, lt, MLA_ROPE)
    kcat_ref[...] = jnp.concatenate([ckv, kr4], axis=1).reshape(bt, lt, QCAT).astype(kcat_ref.dtype)


def _inproj(x, sc, sh, g, w_in_p, wuk_bd, kv_gain, tabs, bt, lt, qdtype):
    b, l, d = x.shape
    grid = (b // bt, l // lt)
    c64, s64, c32, s32 = tabs
    tok = lambda w: pl.BlockSpec((bt, lt, w), lambda i, j: (i, j, 0))
    mod = pl.BlockSpec((bt, 1, d), lambda i, j: (i, 0, 0))
    const2 = lambda a: pl.BlockSpec(a.shape, lambda i, j: (0, 0))
    tab = pl.BlockSpec((lt, LANES), lambda i, j: (j, 0))
    out_shape = [jax.ShapeDtypeStruct((b, l, 256), F32)] * 6 + [
        jax.ShapeDtypeStruct((b, MLA_HEADS, l, QCAT), qdtype),
        jax.ShapeDtypeStruct((b, l, QCAT), qdtype),
        jax.ShapeDtypeStruct((b, l, KV_RANK), F32),
        jax.ShapeDtypeStruct((b, l, MLA_ROPE), F32)]
    out_specs = [tok(256)] * 6 + [
        pl.BlockSpec((bt, MLA_HEADS, lt, QCAT), lambda i, j: (i, 0, j, 0)),
        tok(QCAT), tok(KV_RANK), tok(MLA_ROPE)]
    return pl.pallas_call(
        _inproj_kernel,
        grid=grid,
        in_specs=[tok(d), mod, mod, pl.BlockSpec((1, 1, d), lambda i, j: (0, 0, 0)),
                  const2(w_in_p), const2(wuk_bd), const2(kv_gain), tab, tab, tab, tab],
        out_specs=out_specs,
        out_shape=out_shape,
        compiler_params=_cparams(("arbitrary", "arbitrary")),
        name="inproj",
    )(x, sc, sh, g, w_in_p, wuk_bd, kv_gain, c64, s64, c32, s32)


def _retention_kernel(q_ref, k_ref, v_ref, g_ref, r0_ref, dmat_ref, xi_ref, zeta_ref, dec_ref, gain_ref,
                      o_ref, rfin_ref, r_sc):
    ci = pl.program_id(1)

    @pl.when(ci == 0)
    def _():
        r_sc[...] = r0_ref[...]

    for bi in range(q_ref.shape[0]):
        q = q_ref[bi]
        k = k_ref[bi] * (RET_DK ** -0.5)
        v = v_ref[bi]
        outs = []
        for hd in range(RET_HEADS):
            sl = slice(hd * RET_DK, (hd + 1) * RET_DK)
            qh = q[:, sl].astype(BF16)
            kh = k[:, sl]
            vh = v[:, sl].astype(BF16)
            s = lax.dot_general(qh, kh.astype(BF16), (((1,), (1,)), ((), ())), preferred_element_type=F32)
            inner = jnp.dot((s * dmat_ref[hd]).astype(BF16), vh, preferred_element_type=F32)
            r = r_sc[bi, hd]
            cross = jnp.dot(qh, r.astype(BF16), preferred_element_type=F32) * xi_ref[:, hd:hd + 1]
            o = inner + cross
            mu = jnp.mean(o, axis=-1, keepdims=True)
            oc = o - mu
            var = jnp.mean(oc * oc, axis=-1, keepdims=True)
            outs.append(oc * lax.rsqrt(var + EPS))
            kz = (kh * zeta_ref[:, hd:hd + 1]).astype(BF16)
            kv = lax.dot_general(kz, vh, (((0,), (0,)), ((), ())), preferred_element_type=F32)
            r_sc[bi, hd] = dec_ref[hd] * r + kv
        o_ref[bi] = jnp.concatenate(outs, axis=1) * gain_ref[...] * _silu(g_ref[bi])

    @pl.when(ci == pl.num_programs(1) - 1)
    def _():
        rfin_ref[...] = r_sc[...]


def _retention_consts(c):
    log_g = jnp.log1p(-jnp.exp2(-5.0 - jnp.arange(RET_HEADS, dtype=F32)))
    i = jnp.arange(c, dtype=F32)
    diff = i[:, None] - i[None, :]
    dmat = jnp.where(diff >= 0, jnp.exp(jnp.maximum(diff, 0.0)[None] * log_g[:, None, None]), 0.0)
    xi = jnp.exp((i[:, None] + 1.0) * log_g[None, :])
    zeta = jnp.exp((c - 1.0 - i)[:, None] * log_g[None, :])
    dec = jnp.broadcast_to(jnp.exp(c * log_g)[:, None, None], (RET_HEADS, 1, RET_DV))
    return dmat, xi, zeta, dec


def _retention(rq, rk, rv, rg, r0, gain, bb):
    b, l, _ = rq.shape
    c = RET_CHUNK if l % RET_CHUNK == 0 else l
    n = l // c
    dmat, xi, zeta, dec = _retention_consts(c)
    tok = pl.BlockSpec((bb, c, RET_WIDTH), lambda i, j: (i, j, 0))
    st = pl.BlockSpec((bb, RET_HEADS, RET_DK, RET_DV), lambda i, j: (i, 0, 0, 0))
    cst = lambda a: pl.BlockSpec(a.shape, lambda i, j: (0,) * a.ndim)
    return pl.pallas_call(
        _retention_kernel,
        grid=(b // bb, n),
        in_specs=[tok, tok, tok, tok, st, cst(dmat), cst(xi), cst(zeta), cst(dec), cst(gain)],
        out_specs=[tok, st],
        out_shape=[jax.ShapeDtypeStruct((b, l, RET_WIDTH), F32),
                   jax.ShapeDtypeStruct((b, RET_HEADS, RET_DK, RET_DV), F32)],
        scratch_shapes=[pltpu.VMEM((bb, RET_HEADS, RET_DK, RET_DV), F32)],
        compiler_params=_cparams(("arbitrary", "arbitrary")),
        name="retention",
    )(rq, rk, rv, rg, r0, dmat, xi, zeta, dec, gain)


def _rglru_kernel(x_ref, gate_ref, buf_ref, h0_ref, cw_ref, cb_ref, wai_ref, bai_ref, lam_ref, gain_ref,
                  o_ref, conv_ref, hl_ref, xin_sc, a_sc, b_sc, h_sc, hc_sc):
    ci = pl.program_id(1)
    tc = x_ref.shape[1]
    pad = SUBLANES - (CONV_W - 1)

    @pl.when(ci == 0)
    def _():
        xin_sc[pad:SUBLANES, :] = buf_ref[0]
        hc_sc[...] = h0_ref[0]

    xin_sc[SUBLANES:SUBLANES + tc, :] = x_ref[0]
    xc = cb_ref[...]
    for kk in range(CONV_W):
        xc = xc + xin_sc[pad + kk:pad + kk + tc, :] * cw_ref[kk:kk + 1, :]
    tail = xin_sc[SUBLANES + tc - (CONV_W - 1):SUBLANES + tc, :]
    xin_sc[pad:SUBLANES, :] = tail

    z = jnp.dot(xc.astype(BF16), wai_ref[...], preferred_element_type=F32) + bai_ref[...]
    r = jax.nn.sigmoid(z[:, :LRU_WIDTH])
    ig = jax.nn.sigmoid(z[:, LRU_WIDTH:])
    log_a = -LRU_C * r * jax.nn.softplus(-lam_ref[...])
    a = jnp.exp(log_a)
    a_sc[...] = a
    b_sc[...] = jnp.sqrt(-jnp.tanh(log_a) * (a * a + 1.0)) * ig * xc

    def body(t, h):
        h = a_sc[pl.ds(t, 1), :] * h + b_sc[pl.ds(t, 1), :]
        h_sc[pl.ds(t, 1), :] = h
        return h

    h_last = lax.fori_loop(0, tc, body, hc_sc[...], unroll=8)
    hc_sc[...] = h_last
    y = h_sc[...] * jax.nn.gelu(gate_ref[0])
    o_ref[0] = _rms(y, gain_ref[...])

    @pl.when(ci == pl.num_programs(1) - 1)
    def _():
        conv_ref[0] = tail
        hl_ref[0] = h_last


def _rglru(lx, lg, buf0, h0, conv_w, conv_b, wai, bai, lam, gain, tc):
    b, l, w = lx.shape
    tok = pl.BlockSpec((1, tc, w), lambda i, j: (i, j, 0))
    cst = lambda a: pl.BlockSpec(a.shape, lambda i, j: (0,) * a.ndim)
    bufs = pl.BlockSpec((1, CONV_W - 1, w), lambda i, j: (i, 0, 0))
    hs = pl.BlockSpec((1, 1, w), lambda i, j: (i, 0, 0))
    return pl.pallas_call(
        _rglru_kernel,
        grid=(b, l // tc),
        in_specs=[tok, tok, bufs, hs, cst(conv_w), cst(conv_b), cst(wai), cst(bai), cst(lam), cst(gain)],
        out_specs=[tok, bufs, hs],
        out_shape=[jax.ShapeDtypeStruct((b, l, w), F32),
                   jax.ShapeDtypeStruct((b, CONV_W - 1, w), F32),
                   jax.ShapeDtypeStruct((b, 1, w), F32)],
        scratch_shapes=[pltpu.VMEM((SUBLANES + tc, w), F32), pltpu.VMEM((tc, w), F32), pltpu.VMEM((tc, w), F32),
                        pltpu.VMEM((tc, w), F32), pltpu.VMEM((1, w), F32)],
        compiler_params=_cparams(("arbitrary", "arbitrary")),
        name="rglru",
    )(lx, lg, buf0, h0, conv_w, conv_b, wai, bai, lam, gain)


def _softmax_update(s, v, m_sc, l_sc, acc_sc, rows=slice(None)):
    n = s.shape[1]
    m_prev = m_sc[rows, :]
    m_new = jnp.maximum(m_prev, jnp.max(s, axis=-1, keepdims=True))
    corr = jnp.exp(m_prev - m_new)
    p = jnp.exp(s - jnp.concatenate([m_new] * (n // LANES), axis=1))
    l_sc[rows, :] = l_sc[rows, :] * corr + jnp.sum(p, axis=-1, keepdims=True)
    acc_sc[rows, :] = acc_sc[rows, :] * corr + jnp.dot(p.astype(BF16), v, preferred_element_type=F32)
    m_sc[rows, :] = m_new


def _attn_finish(o, wuv_ref, gain_ref, rows_per_head):
    o = o.astype(BF16)
    om = None
    for hd in range(MLA_HEADS):
        part = jnp.dot(o[hd * rows_per_head:(hd + 1) * rows_per_head], wuv_ref[hd], preferred_element_type=F32)
        om = part if om is None else om + part
    return _rms(om, gain_ref[...])


def _flash_kernel(q_ref, k_ref, wuv_ref, gain_ref, o_ref, m_sc, l_sc, acc_sc, *, qb, kb):
    qi = pl.program_id(1)
    r = MLA_HEADS * qb
    m_sc[...] = jnp.full((r, LANES), -jnp.inf, F32)
    l_sc[...] = jnp.zeros((r, LANES), F32)
    acc_sc[...] = jnp.zeros((r, LANES), F32)
    q0 = qi * qb
    nfull = q0 // kb
    halves = [slice(i * (r // 2), (i + 1) * (r // 2)) for i in range(2)]
    q = q_ref[0].reshape(r, QCAT)
    qs = [q[h] for h in halves]

    def step(start, width, masked):
        kblk = k_ref[0, pl.ds(start, width), :]
        v = kblk[:, :KV_RANK]
        for h, qh in zip(halves, qs):
            s = lax.dot_general(qh, kblk, (((1,), (1,)), ((), ())), preferred_element_type=F32)
            if masked:
                row = lax.broadcasted_iota(jnp.int32, s.shape, 0)
                col = lax.broadcasted_iota(jnp.int32, s.shape, 1)
                s = jnp.where(start + col <= q0 + row % qb, s, -jnp.inf)
            _softmax_update(s, v, m_sc, l_sc, acc_sc, h)

    def body(j, carry):
        step(pl.multiple_of(j * kb, kb), kb, False)
        return carry

    lax.fori_loop(0, nfull, body, 0)
    half = kb // 2
    base = pl.multiple_of(nfull * kb, kb)
    upper = (q0 - base) >= half

    @pl.when(upper)
    def _():
        step(base, half, False)

    step(pl.multiple_of(base + jnp.where(upper, half, 0), half), half, True)
    o_ref[0] = _attn_finish(acc_sc[...] / l_sc[...], wuv_ref, gain_ref, qb)


def _mla_prompt(qcat, kcat, wuv_exp, gain, qb, kb):
    b, _, s, _ = qcat.shape
    r = MLA_HEADS * qb
    return pl.pallas_call(
        functools.partial(_flash_kernel, qb=qb, kb=kb),
        grid=(b, s // qb),
        in_specs=[pl.BlockSpec((1, MLA_HEADS, qb, QCAT), lambda i, j: (i, 0, j, 0)),
                  pl.BlockSpec((1, s, QCAT), lambda i, j: (i, 0, 0)),
                  pl.BlockSpec(wuv_exp.shape, lambda i, j: (0, 0, 0)),
                  pl.BlockSpec(gain.shape, lambda i, j: (0, 0))],
        out_specs=pl.BlockSpec((1, qb, MLA_WIDTH), lambda i, j: (i, j, 0)),
        out_shape=jax.ShapeDtypeStruct((b, s, MLA_WIDTH), F32),
        scratch_shapes=[pltpu.VMEM((r, LANES), F32)] * 3,
        compiler_params=_cparams(("arbitrary", "arbitrary")),
        name="mla_prompt",
    )(qcat, kcat, wuv_exp, gain)


def _mla_sample_kernel(pt_ref, q_ref, knew_ref, ck_hbm, kp_hbm, wuv_ref, gain_ref, o_ref, ckbuf, kpbuf, sems,
                       *, li, n_pages, page, t_new, chunk):
    bi = pl.program_id(0)
    nb = pl.num_programs(0)
    r = MLA_HEADS * t_new

    def page_copies(pg_of, slot):
        out = []
        for j in range(n_pages):
            pg = pg_of(j)
            out.append(pltpu.make_async_copy(ck_hbm.at[li, pg], ckbuf.at[slot, pl.ds(j * page, page), :],
                                             sems.at[0, slot]))
            out.append(pltpu.make_async_copy(kp_hbm.at[li, pg], kpbuf.at[slot, :, pl.ds(j * page, page)],
                                             sems.at[1, slot]))
        return out

    @pl.when(bi == 0)
    def _():
        for c in page_copies(lambda j: pt_ref[0, j], 0):
            c.start()

    @pl.when(bi + 1 < nb)
    def _():
        for c in page_copies(lambda j: pt_ref[bi + 1, j], (bi + 1) % 2):
            c.start()

    slot = bi % 2
    q = q_ref[0].reshape(r, QCAT)
    qb = q.astype(BF16)
    q_lat = qb[:, :KV_RANK]
    q_rot = q[:, KV_RANK:].astype(F32)
    q_r = q_rot[:, :MLA_ROPE]
    for i in range(1, LANES // MLA_ROPE):
        q_r = q_r + q_rot[:, i * MLA_ROPE:(i + 1) * MLA_ROPE]
    q_r = q_r.astype(BF16)

    kn = knew_ref[0].astype(BF16)
    s = lax.dot_general(qb, kn, (((1,), (1,)), ((), ())), preferred_element_type=F32)
    row = lax.broadcasted_iota(jnp.int32, s.shape, 0)
    col = lax.broadcasted_iota(jnp.int32, s.shape, 1)
    s = jnp.where(row % t_new >= col, s, -jnp.inf)
    m = jnp.max(s, axis=-1, keepdims=True)
    p = jnp.exp(s - m)
    l = jnp.sum(p, axis=-1, keepdims=True)
    acc = jnp.dot(p.astype(BF16), kn[:, :KV_RANK], preferred_element_type=F32)

    for c in page_copies(lambda j: 0, slot):
        c.wait()

    for c in range(n_pages * page // chunk):
        ck = ckbuf[slot, pl.ds(c * chunk, chunk), :].astype(BF16)
        kp = kpbuf[slot, :, pl.ds(c * chunk, chunk)].astype(BF16)
        s = (lax.dot_general(q_lat, ck, (((1,), (1,)), ((), ())), preferred_element_type=F32)
             + jnp.dot(q_r, kp, preferred_element_type=F32))
        m_new = jnp.maximum(m, jnp.max(s, axis=-1, keepdims=True))
        corr = jnp.exp(m - m_new)
        p = jnp.exp(s - m_new)
        l = l * corr + jnp.sum(p, axis=-1, keepdims=True)
        acc = acc * corr + jnp.dot(p.astype(BF16), ck, preferred_element_type=F32)
        m = m_new
    o_ref[0] = _attn_finish(acc / l, wuv_ref, gain_ref, t_new)


def _mla_sample(qcat, kcat, cache_ckv, cache_kr_t, page_table, li, wuv_exp, gain):
    b, _, t, _ = qcat.shape
    n_pages = page_table.shape[1]
    page, rank = cache_ckv.shape[2], cache_ckv.shape[3]
    rope_w = cache_kr_t.shape[2]
    chunk = min(8192, n_pages * page)
    grid_spec = pltpu.PrefetchScalarGridSpec(
        num_scalar_prefetch=1,
        grid=(b,),
        in_specs=[pl.BlockSpec((1, MLA_HEADS, t, QCAT), lambda bi, pt: (bi, 0, 0, 0)),
                  pl.BlockSpec((1, t, QCAT), lambda bi, pt: (bi, 0, 0)),
                  pl.BlockSpec(memory_space=pl.ANY),
                  pl.BlockSpec(memory_space=pl.ANY),
                  pl.BlockSpec(wuv_exp.shape, lambda bi, pt: (0, 0, 0)),
                  pl.BlockSpec(gain.shape, lambda bi, pt: (0, 0))],
        out_specs=pl.BlockSpec((1, t, MLA_WIDTH), lambda bi, pt: (bi, 0, 0)),
        scratch_shapes=[pltpu.VMEM((2, n_pages * page, rank), F32),
                        pltpu.VMEM((2, rope_w, n_pages * page), F32),
                        pltpu.SemaphoreType.DMA((2, 2))],
    )
    return pl.pallas_call(
        functools.partial(_mla_sample_kernel, li=li, n_pages=n_pages, page=page, t_new=t, chunk=chunk),
        grid_spec=grid_spec,
        out_shape=jax.ShapeDtypeStruct((b, t, MLA_WIDTH), F32),
        compiler_params=_cparams(("arbitrary",)),
        name="mla_sample",
    )(page_table, qcat, kcat, cache_ckv, cache_kr_t, wuv_exp, gain)


def _outproj_kernel(ret_ref, lru_ref, mla_ref, x_ref, w_ref, g1_ref, gt1_ref, g2_ref, sc2_ref, sh2_ref,
                    x1_ref, h2_ref, h2p_ref):
    bt, lt, d = x_ref.shape
    rows = bt * lt
    mix = jnp.concatenate([ret_ref[...].reshape(rows, RET_WIDTH), lru_ref[...].reshape(rows, LRU_WIDTH),
                           mla_ref[...].reshape(rows, MLA_WIDTH)], axis=1).astype(BF16)
    y = jnp.dot(mix, w_ref[...], preferred_element_type=F32).reshape(bt, lt, d)
    x1 = x_ref[...] + gt1_ref[...] * _rms(y, g1_ref[...])
    x1_ref[...] = x1
    h2 = _rms(x1, g2_ref[...]) * (1.0 + sc2_ref[...]) + sh2_ref[...]
    h2_ref[...] = h2
    _store_packed(h2p_ref, h2.reshape(rows, d))


def _outproj(ret, lru, mla, x, w_out, g1, gt1, g2, sc2, sh2, bt, lt):
    b, l, d = x.shape
    tok = lambda w: pl.BlockSpec((bt, lt, w), lambda i, j: (i, j, 0))
    mod = pl.BlockSpec((bt, 1, d), lambda i, j: (i, 0, 0))
    gsp = pl.BlockSpec((1, 1, d), lambda i, j: (0, 0, 0))
    return pl.pallas_call(
        _outproj_kernel,
        grid=(b // bt, l // lt),
        in_specs=[tok(RET_WIDTH), tok(LRU_WIDTH), tok(MLA_WIDTH), tok(d),
                  pl.BlockSpec(w_out.shape, lambda i, j: (0, 0)), gsp, mod, gsp, mod, mod],
        out_specs=[tok(d), tok(d),
                   pl.BlockSpec((bt * lt * PACK_ROWS, LANES), lambda i, j: (i * (l // lt) + j, 0))],
        out_shape=[jax.ShapeDtypeStruct((b, l, d), F32)] * 2
        + [jax.ShapeDtypeStruct((b * l * PACK_ROWS, LANES), jnp.int32)],
        compiler_params=_cparams(("arbitrary", "arbitrary")),
        name="outproj",
    )(ret, lru, mla, x, w_out, g1, gt1, g2, sc2, sh2)


def _split_dot_t(w, x):
    dn = (((1,), (1,)), ((), ()))
    wh = w.astype(BF16)
    wl = (w - wh.astype(F32)).astype(BF16)
    xh = x.astype(BF16)
    xl = (x - xh.astype(F32)).astype(BF16)
    return (lax.dot_general(wh, xh, dn, preferred_element_type=F32)
            + lax.dot_general(wh, xl, dn, preferred_element_type=F32)
            + lax.dot_general(wl, xh, dn, preferred_element_type=F32))


def _router_kernel(x_ref, w_ref, b_ref, upper_ref, lower_ref, eid_ref, pos_ref, gate_ref, cnt_ref, cnt_sc):
    ti = pl.program_id(0)
    tm = x_ref.shape[0]

    @pl.when(ti == 0)
    def _():
        cnt_sc[...] = jnp.zeros(cnt_sc.shape, F32)

    logits = _split_dot_t(w_ref[...], x_ref[...])
    scores = jax.nn.sigmoid(logits)
    biased = scores + b_ref[...]
    groups = [biased[g * GROUP_SIZE:(g + 1) * GROUP_SIZE, :] for g in range(N_GROUPS)]
    sub = lax.broadcasted_iota(jnp.int32, (GROUP_SIZE, tm), 0)
    gs = []
    for bg in groups:
        m1 = jnp.max(bg, axis=0, keepdims=True)
        i1 = jnp.min(jnp.where(bg == m1, sub, GROUP_SIZE), axis=0, keepdims=True)
        m2 = jnp.max(jnp.where(sub == i1, -jnp.inf, bg), axis=0, keepdims=True)
        gs.append(m1 + m2)
    masked = []
    for g in range(N_GROUPS):
        rank = jnp.zeros((1, tm), F32)
        for g2 in range(N_GROUPS):
            if g2 != g:
                ahead = (gs[g2] >= gs[g]) if g2 < g else (gs[g2] > gs[g])
                rank = rank + jnp.where(ahead, 1.0, 0.0)
        masked.append(jnp.where(rank < TOPK_GROUPS, groups[g], -jnp.inf))
    sels = []
    for g in range(N_GROUPS):
        mine = masked[g]
        rank = jnp.zeros((GROUP_SIZE, tm), F32)
        for g2 in range(N_GROUPS):
            for s2 in range(GROUP_SIZE):
                other = jnp.broadcast_to(masked[g2][s2:s2 + 1, :], (GROUP_SIZE, tm))
                if g2 < g:
                    rank = rank + jnp.where(other >= mine, 1.0, 0.0)
                elif g2 > g:
                    rank = rank + jnp.where(other > mine, 1.0, 0.0)
                else:
                    tie = jnp.where(sub > s2, 1.0, 0.0)
                    rank = rank + jnp.where(other > mine, 1.0, jnp.where(other == mine, tie, 0.0))
        sels.append(jnp.where(rank < TOP_K, 1.0, 0.0))
    self32 = jnp.concatenate(sels, axis=0)
    sel = self32 > 0.5
    picked = jnp.where(sel, scores, 0.0)
    gate = picked / jnp.sum(picked, axis=0, keepdims=True) * ROUTED_SCALE
    selb = self32.astype(BF16)
    pos = jnp.dot(selb, upper_ref[...], preferred_element_type=F32) + cnt_sc[:, :1]
    cnt_sc[...] = cnt_sc[...] + jnp.sum(self32, axis=1, keepdims=True)
    slot = jnp.dot(lower_ref[...], selb, preferred_element_type=F32)
    efl = lax.broadcasted_iota(jnp.int32, (N_EXPERTS, tm), 0).astype(F32)
    for kk in range(TOP_K):
        hit = sel & (slot == float(kk))
        eid_ref[kk:kk + 1, :] = jnp.sum(jnp.where(hit, efl, 0.0), axis=0, keepdims=True).astype(jnp.int32)
        pos_ref[kk:kk + 1, :] = jnp.sum(jnp.where(hit, pos, 0.0), axis=0, keepdims=True).astype(jnp.int32)
        gate_ref[kk:kk + 1, :] = jnp.sum(jnp.where(hit, gate, 0.0), axis=0, keepdims=True)

    @pl.when(ti == pl.num_programs(0) - 1)
    def _():
        cnt_ref[...] = cnt_sc[...]


def _router(h2, w_rt, b_r, tm):
    t, d = h2.shape
    upper = jnp.triu(jnp.ones((tm, tm), F32), 1).astype(BF16)
    lower = jnp.tril(jnp.ones((N_EXPERTS, N_EXPERTS), F32), -1).astype(BF16)
    kt = pl.BlockSpec((TOP_K, tm), lambda i: (0, i))
    cst = lambda a: pl.BlockSpec(a.shape, lambda i: (0,) * a.ndim)
    return pl.pallas_call(
        _router_kernel,
        grid=(t // tm,),
        in_specs=[pl.BlockSpec((tm, d), lambda i: (i, 0)), cst(w_rt), cst(b_r), cst(upper), cst(lower)],
        out_specs=[kt, kt, kt, pl.BlockSpec((N_EXPERTS, LANES), lambda i: (0, 0))],
        out_shape=[jax.ShapeDtypeStruct((TOP_K, t), jnp.int32), jax.ShapeDtypeStruct((TOP_K, t), jnp.int32),
                   jax.ShapeDtypeStruct((TOP_K, t), F32), jax.ShapeDtypeStruct((N_EXPERTS, LANES), F32)],
        scratch_shapes=[pltpu.VMEM((N_EXPERTS, LANES), F32)],
        compiler_params=_cparams(("arbitrary",)),
        name="router",
    )(h2, w_rt, b_r, upper, lower)


PACK_ROWS = 4


def _store_packed(o_ref, x):
    r = x.shape[0]
    bits = lax.bitcast_convert_type(x.astype(BF16).astype(F32), jnp.int32)
    for j in range(PACK_ROWS):
        lo = lax.shift_right_logical(bits[:, j * LANES:(j + 1) * LANES], jnp.int32(16))
        hi = bits[:, (j + PACK_ROWS) * LANES:(j + PACK_ROWS + 1) * LANES] & jnp.int32(-65536)
        o_ref[pl.ds(j, r, stride=PACK_ROWS), :] = hi | lo


def _unpack_words(w):
    lo = lax.bitcast_convert_type(lax.shift_left(w, jnp.int32(16)), F32)
    hi = lax.bitcast_convert_type(w & jnp.int32(-65536), F32)
    return lo, hi


def _dispatch_kernel(dest_ref, x_ref, rows_in_ref, rows_ref, sem):
    del rows_in_ref
    tm = x_ref.shape[0] // PACK_ROWS

    def row_copy(t, d):
        return pltpu.make_async_copy(x_ref.at[pl.ds(pl.multiple_of(t * PACK_ROWS, PACK_ROWS), PACK_ROWS), :],
                                     rows_ref.at[pl.ds(pl.multiple_of(d * PACK_ROWS, PACK_ROWS), PACK_ROWS), :], sem)

    def issue(t, c):
        for kk in range(TOP_K):
            row_copy(t, dest_ref[t * TOP_K + kk]).start(priority=kk % 2)
        return c

    def drain(t, c):
        for kk in range(TOP_K):
            row_copy(0, 0).wait()
        return c

    lax.fori_loop(0, tm, issue, 0)
    lax.fori_loop(0, tm, drain, 0)


def _dispatch(dest_flat, h2p, rows0, tm):
    t = h2p.shape[0] // PACK_ROWS
    return pl.pallas_call(
        _dispatch_kernel,
        grid=(t // tm,),
        in_specs=[pl.BlockSpec((tm * TOP_K,), lambda i: (i,), memory_space=pltpu.SMEM),
                  pl.BlockSpec((tm * PACK_ROWS, LANES), lambda i: (i, 0)),
                  pl.BlockSpec(memory_space=pl.ANY)],
        out_specs=pl.BlockSpec(memory_space=pl.ANY),
        out_shape=jax.ShapeDtypeStruct(rows0.shape, jnp.int32),
        scratch_shapes=[pltpu.SemaphoreType.DMA(())],
        input_output_aliases={2: 0},
        compiler_params=_cparams(("arbitrary",)),
        name="moe_dispatch",
    )(dest_flat, h2p, rows0)


def _expert_kernel(be_ref, nu_ref, rows_ref, wgu_ref, wd_ref, o_ref, wgu_sc, wd_sc):
    i = pl.program_id(0)
    prev = be_ref[jnp.maximum(i - 1, 0)]
    fresh = (i == 0) | (be_ref[i] != prev)

    @pl.when(fresh)
    def _():
        wgu_sc[...] = wgu_ref[...].astype(BF16)
        wd_sc[...] = wd_ref[...].astype(BF16)

    @pl.when(i < nu_ref[0])
    def _():
        blk = rows_ref.shape[0] // PACK_ROWS
        halves = [_unpack_words(rows_ref[pl.ds(j, blk, stride=PACK_ROWS), :]) for j in range(PACK_ROWS)]
        x = jnp.concatenate([h[0] for h in halves] + [h[1] for h in halves], axis=1).astype(BF16)
        gu = jnp.dot(x, wgu_sc[...], preferred_element_type=F32)
        act = _silu(gu[:, :D_EXPERT]) * gu[:, D_EXPERT:]
        y = jnp.dot(act.astype(BF16), wd_sc[...], preferred_element_type=F32)
        _store_packed(o_ref, y)

    @pl.when(i >= nu_ref[0])
    def _():
        o_ref[...] = jnp.zeros(o_ref.shape, o_ref.dtype)


def _experts(block_e, n_used, rows, w_gu, w_down, li, blk):
    n_rows = rows.shape[0] // PACK_ROWS
    n_blocks = n_rows // blk
    d, de2 = w_gu.shape[-2:]
    de = w_down.shape[-2]

    def rmap(i, be, nu):
        return (jnp.minimum(i, nu[0] - 1), 0)

    grid_spec = pltpu.PrefetchScalarGridSpec(
        num_scalar_prefetch=2,
        grid=(n_blocks,),
        in_specs=[pl.BlockSpec((blk * PACK_ROWS, LANES), rmap),
                  pl.BlockSpec((None, None, d, de2), lambda i, be, nu: (li, be[i], 0, 0)),
                  pl.BlockSpec((None, None, de, d), lambda i, be, nu: (li, be[i], 0, 0))],
        out_specs=pl.BlockSpec((blk * PACK_ROWS, LANES), lambda i, be, nu: (i, 0)),
        scratch_shapes=[pltpu.VMEM((d, de2), BF16), pltpu.VMEM((de, d), BF16)],
    )
    return pl.pallas_call(
        _expert_kernel,
        grid_spec=grid_spec,
        out_shape=jax.ShapeDtypeStruct(rows.shape, jnp.int32),
        compiler_params=_cparams(("arbitrary",)),
        name="moe_experts",
    )(block_e, n_used, rows, w_gu, w_down)


def _combine_kernel(dest_ref, dnext_ref, gate_ref, orow_ref, h2_ref, x1_ref, wsgu_ref, wsd_ref, g3_ref, gt2_ref,
                    y_ref, buf, sems):
    bt, lt, d = x1_ref.shape
    tm = bt * lt
    ns = PACK_ROWS
    step = pl.program_id(0) * pl.num_programs(1) + pl.program_id(1)
    nsteps = pl.num_programs(0) * pl.num_programs(1)
    slot = step % 2

    def row_copy(src_row, t, kk, sl):
        return pltpu.make_async_copy(orow_ref.at[pl.ds(pl.multiple_of(src_row * ns, ns), ns), :],
                                     buf.at[sl, kk, pl.ds(pl.multiple_of(t * ns, ns), ns), :],
                                     sems.at[sl])

    def issue_tile(dref, sl):
        def issue(t, c):
            for kk in range(TOP_K):
                row_copy(dref[t * TOP_K + kk], t, kk, sl).start(priority=kk % 2)
            return c
        lax.fori_loop(0, tm, issue, 0)

    @pl.when(step == 0)
    def _():
        issue_tile(dest_ref, 0)

    @pl.when(step + 1 < nsteps)
    def _():
        issue_tile(dnext_ref, 1 - slot)

    hb = h2_ref[...].astype(BF16)
    gu = jnp.dot(hb, wsgu_ref[...], preferred_element_type=F32)
    ds = wsd_ref.shape[0]
    f = jnp.dot((_silu(gu[:, :ds]) * gu[:, ds:]).astype(BF16), wsd_ref[...], preferred_element_type=F32)

    def drain(t, c):
        for kk in range(TOP_K):
            row_copy(0, 0, 0, slot).wait()
        return c

    lax.fori_loop(0, tm, drain, 0)
    gate = gate_ref[...]
    fs = [f[:, j * LANES:(j + 1) * LANES] for j in range(2 * ns)]
    for kk in range(TOP_K):
        gk = jnp.broadcast_to(gate[:, kk:kk + 1], (tm, LANES))
        for j in range(ns):
            lo, hi = _unpack_words(buf[slot, kk, pl.ds(j, tm, stride=ns), :])
            fs[j] = fs[j] + lo * gk
            fs[j + ns] = fs[j + ns] + hi * gk
    f = jnp.concatenate(fs, axis=1)
    y_ref[...] = x1_ref[...] + gt2_ref[...] * _rms(f, g3_ref[...]).reshape(bt, lt, d)


def _combine(dest_flat, gate_tk, out_rows, h2, x1, ws_gu, ws_down, g3, gt2, bt, lt):
    b, l, d = x1.shape
    tm = bt * lt
    nl = l // lt
    nsteps = (b // bt) * nl
    tok = pl.BlockSpec((bt, lt, d), lambda i, j: (i, j, 0))
    flat = lambda w: pl.BlockSpec((tm, w), lambda i, j: (i * nl + j, 0))
    return pl.pallas_call(
        _combine_kernel,
        grid=(b // bt, nl),
        in_specs=[pl.BlockSpec((tm * TOP_K,), lambda i, j: (i * nl + j,), memory_space=pltpu.SMEM),
                  pl.BlockSpec((tm * TOP_K,), lambda i, j: (jnp.minimum(i * nl + j + 1, nsteps - 1),),
                               memory_space=pltpu.SMEM),
                  flat(TOP_K), pl.BlockSpec(memory_space=pl.ANY), flat(d), tok,
                  pl.BlockSpec(ws_gu.shape, lambda i, j: (0, 0)), pl.BlockSpec(ws_down.shape, lambda i, j: (0, 0)),
                  pl.BlockSpec((1, 1, d), lambda i, j: (0, 0, 0)),
                  pl.BlockSpec((bt, 1, d), lambda i, j: (i, 0, 0))],
        out_specs=tok,
        out_shape=jax.ShapeDtypeStruct((b, l, d), F32),
        scratch_shapes=[pltpu.VMEM((2, TOP_K, tm * PACK_ROWS, LANES), jnp.int32), pltpu.SemaphoreType.DMA((2,))],
        compiler_params=_cparams(("arbitrary", "arbitrary")),
        name="moe_combine",
    )(dest_flat, dest_flat, gate_tk, out_rows, h2, x1, ws_gu, ws_down, g3, gt2)


def _moe(groups, g3, w_rt, b_r, w_exp_gu, w_exp_down, ws_gu, ws_down, li, blk, rows_init):
    d = groups[0][0].shape[-1]
    sizes = [g[0].shape[0] * g[0].shape[1] for g in groups]
    t = sum(sizes)
    flats = [g[0].reshape(n, d) for g, n in zip(groups, sizes)]
    h2f = flats[0] if len(flats) == 1 else jnp.concatenate(flats, axis=0)
    h2p = groups[0][1] if len(groups) == 1 else jnp.concatenate([g[1] for g in groups], axis=0)
    eid, pos, gate, cnt = _router(h2f, w_rt, b_r, _largest_tile(t, 512))
    counts = cnt[:, 0].astype(jnp.int32)
    padded = (counts + blk - 1) // blk * blk
    pends = jnp.cumsum(padded)
    pstart = pends - padded
    n_blocks = -(-(t * TOP_K) // blk) + N_EXPERTS
    starts = jnp.arange(n_blocks, dtype=jnp.int32) * blk
    block_e = jnp.minimum(jnp.sum((pends[None, :] <= starts[:, None]).astype(jnp.int32), axis=1), N_EXPERTS - 1)
    n_used = (pends[-1:] // blk).astype(jnp.int32)
    first_row = jnp.sum(jnp.where(eid[:, :, None] == jnp.arange(N_EXPERTS, dtype=jnp.int32), pstart, 0), axis=-1)
    dest = (first_row + pos).T.reshape(t * TOP_K)
    gate_tk = gate.T
    if rows_init is None:
        rows_init = jnp.zeros((n_blocks * blk * PACK_ROWS, LANES), jnp.int32)
    rows = _dispatch(dest, h2p, rows_init, _largest_tile(t, 512))
    out_rows = _experts(block_e, n_used, rows, w_exp_gu, w_exp_down, li, blk)
    outs, off = [], 0
    for (h2, _, x1, gt2, bt, lt), n, flat in zip(groups, sizes, flats):
        outs.append(_combine(dest[off * TOP_K:(off + n) * TOP_K], gate_tk[off:off + n], out_rows, flat, x1,
                             ws_gu, ws_down, g3, gt2, bt, lt))
        off += n
    return outs, rows


def _permute_w_in(w_in):
    dp, d, _ = w_in.shape
    hd = MLA_NOPE + MLA_ROPE
    mq = w_in[:, :, C_QN:C_QN + MLA_HEADS * hd].reshape(dp, d, MLA_HEADS, hd)
    qn = mq[..., :MLA_NOPE].reshape(dp, d, MLA_HEADS * MLA_NOPE)
    qr = mq[..., MLA_NOPE:].reshape(dp, d, MLA_HEADS * MLA_ROPE)
    mc0 = C_QN + MLA_HEADS * hd
    mc = w_in[:, :, mc0:mc0 + KV_RANK]
    mk = w_in[:, :, mc0 + KV_RANK:mc0 + KV_RANK + MLA_ROPE]
    return jnp.concatenate([w_in[:, :, :C_QN], qn, qr, mc] + [mk] * (LANES // MLA_ROPE), axis=-1).astype(BF16)


def _rope_tables(pos, dim):
    half = dim // 2
    inv = ROPE_BASE ** (-jnp.arange(0, dim, 2, dtype=F32) / dim)
    ang = pos.astype(F32)[:, None] * inv[None, :]
    cos, sin = jnp.cos(ang), jnp.sin(ang)
    reps = LANES // dim
    cos_t = jnp.tile(jnp.concatenate([cos, cos], axis=1), (1, reps))
    sin_t = jnp.tile(jnp.concatenate([-sin, sin], axis=1), (1, reps))
    return cos_t, sin_t


def _block_diag(blocks):
    n, a, b = blocks.shape
    eye = jnp.eye(n, dtype=blocks.dtype)
    return (eye[:, None, :, None] * blocks[:, :, None, :]).reshape(n * a, n * b)


def _mixer(x, mod, pos_tabs, li, p, past, tiles):
    b, l, d = x.shape
    bt, lt = tiles['bt'], tiles['lt']
    sh1, sc1, gt1, sh2, sc2, gt2 = [m[:, None, :] for m in jnp.split(mod, 6, axis=-1)]
    g = p['norm_gains']
    gains = [g[i][None, None, :] for i in range(4)]
    gain = p['grp_gain']
    rq, rk, rv, rg, lx, lg, qcat, kcat, ckv, kr = _inproj(
        x, sc1, sh1, gains[0], p['w_in_p'], p['wuk_bd'], p['kv_norm'][None, :], pos_tabs, bt, lt, tiles['qdtype'])
    if past is None:
        r0 = jnp.zeros((b, RET_HEADS, RET_DK, RET_DV), F32)
        buf0 = jnp.zeros((b, CONV_W - 1, LRU_WIDTH), F32)
        h0 = jnp.zeros((b, 1, LRU_WIDTH), F32)
    else:
        r0, buf0, h0 = past['ret'], past['conv'], past['lru'][:, None, :]
    ret_out, r_new = _retention(rq, rk, rv, rg, r0, gain[None, :RET_WIDTH], tiles['rbb'])
    lru_out, conv_new, h_new = _rglru(
        lx, lg, buf0, h0, p['conv_w'], p['conv_b'][None, :], p['wai'], p['bai'], p['lru_lambda'][None, :],
        gain[None, RET_WIDTH:RET_WIDTH + LRU_WIDTH], tiles['tc'])
    mla_gain = gain[None, RET_WIDTH + LRU_WIDTH:]
    if past is None:
        mla_out = _mla_prompt(qcat, kcat, p['wuv_exp'], mla_gain, tiles['qb'], tiles['kb'])
    else:
        mla_out = _mla_sample(qcat, kcat, past['cache_ckv'], past['cache_kr_t'], past['page_table'], li,
                              p['wuv_exp'], mla_gain)
    x1, h2, h2p = _outproj(ret_out, lru_out, mla_out, x, p['w_out'], gains[1], gt1, gains[2], sc2, sh2, bt, lt)
    return (h2, h2p, x1, gt2, tiles['cbt'], tiles['clt']), (ckv, kr, r_new, conv_new, h_new[:, 0])


def _largest_tile(n, cap):
    t = min(n, cap)
    while n % t:
        t //= 2
    return t


def kernel(x_prompt, x_sample, cache_ckv, cache_krope, state_ret, state_conv, state_lru, page_table, c_prompt, c_sample, w_ada, b_ada, norm_gains, w_in, w_out, grp_gain, conv_w, conv_b, w_lru_a, b_lru_a, w_lru_i, b_lru_i, lru_lambda, kv_norm, w_ukv, w_router, b_router, w_exp_gu, w_exp_down, w_sh_gu, w_sh_down):
    depth = w_in.shape[0]
    bp, lp, d = x_prompt.shape
    bs, ls, _ = x_sample.shape
    past_len = page_table.shape[1] * cache_ckv.shape[2]

    n_c = bp + bs
    n_cp = -(-n_c // SUBLANES) * SUBLANES
    c_all = jnp.concatenate([c_prompt, c_sample, jnp.zeros((n_cp - n_c, d), F32)], axis=0)
    mod_all = _adaln(c_all, w_ada, b_ada[:, None, :])

    w_in_p = _permute_w_in(w_in)
    w_uk = w_ukv[..., :MLA_NOPE]
    w_uv = w_ukv[..., MLA_NOPE:]
    eye_h = jnp.eye(MLA_HEADS, dtype=F32)

    pos_p = jnp.arange(lp)
    pos_s = past_len + jnp.arange(ls)
    tabs_p = _rope_tables(pos_p, RET_DK) + _rope_tables(pos_p, MLA_ROPE)
    tabs_s = _rope_tables(pos_s, RET_DK) + _rope_tables(pos_s, MLA_ROPE)

    lt_p = _largest_tile(lp, 512)
    kb_p = _largest_tile(lp, 1024)
    tiles_p = dict(bt=1, lt=lt_p, rbb=1, qdtype=BF16, tc=_largest_tile(lp, 512), qb=_largest_tile(lp, min(256, kb_p // 2)),
                   kb=kb_p, cbt=1, clt=_largest_tile(lp, 128))
    bt_s = _largest_tile(bs, 32)
    tiles_s = dict(bt=bt_s, lt=ls, qdtype=F32, tc=ls, rbb=_largest_tile(bs, 8),
                   cbt=_largest_tile(bs, 32), clt=ls)

    cache_kr_t = jnp.swapaxes(cache_krope, 2, 3)
    y_p, y_s = x_prompt, x_sample
    blk = 256 if bp * lp + bs * ls >= 4096 else 128
    moe_rows = None
    st_p, st_s = [], []
    for li in range(depth):
        wuk_bd = _block_diag(jnp.transpose(w_uk[li], (1, 2, 0))).astype(BF16)
        wuv_h = jnp.transpose(w_uv[li], (1, 0, 2))
        wuv_exp = (wuv_h[:, :, None, :] * eye_h[:, None, :, None]).reshape(
            MLA_HEADS, KV_RANK, MLA_WIDTH).astype(BF16)
        wai = jnp.concatenate([_block_diag(w_lru_a[li]), _block_diag(w_lru_i[li])], axis=1).astype(BF16)
        bai = jnp.concatenate([b_lru_a[li], b_lru_i[li]])[None, :]
        p = dict(norm_gains=norm_gains[li], w_in_p=w_in_p[li], wuk_bd=wuk_bd, wuv_exp=wuv_exp,
                 w_out=w_out[li].astype(BF16), grp_gain=grp_gain[li], conv_w=conv_w[li], conv_b=conv_b[li],
                 wai=wai, bai=bai, lru_lambda=lru_lambda[li], kv_norm=kv_norm[li],
                 w_rt=w_router[li].T, b_router=b_router[li], w_exp_gu=w_exp_gu, w_exp_down=w_exp_down,
                 ws_gu=w_sh_gu[li].astype(BF16), ws_down=w_sh_down[li].astype(BF16))
        grp_p, sp = _mixer(y_p, mod_all[li, :bp], tabs_p, li, p, None, tiles_p)
        past = dict(ret=state_ret[li], conv=state_conv[li], lru=state_lru[li], cache_ckv=cache_ckv,
                    cache_kr_t=cache_kr_t, page_table=page_table)
        grp_s, ss = _mixer(y_s, mod_all[li, bp:bp + bs], tabs_s, li, p, past, tiles_s)
        (y_p, y_s), moe_rows = _moe([grp_p, grp_s], norm_gains[li][3][None, None, :], p['w_rt'], p['b_router'][:, None],
                                    p['w_exp_gu'], p['w_exp_down'], p['ws_gu'], p['ws_down'], li, blk, moe_rows)
        st_p.append(sp)
        st_s.append(ss)
    stack = lambda sts, i: jnp.stack([s[i] for s in sts])
    return (y_p, y_s) + tuple(stack(st_p, i) for i in range(5)) + tuple(stack(st_s, i) for i in range(5))
```
